```python
import jax, jax.numpy as jnp
from jax import lax
import numpy as np

D_MODEL = 1024
BATCH = 32
SEQ = 256
DEPTH = 2
DEC_BATCH = 2
DEC_SEQ = 1024
PAST_LEN = 256

GRID_W = 64
N_MIXERS = 2
N_ATTN_LAYERS = (DEPTH + 1) // 2
N_REC_LAYERS = DEPTH // 2
HEAD_DIM = 64
N_Q_HEADS = D_MODEL // HEAD_DIM
N_KV_HEADS = N_Q_HEADS // 4
GQA_GROUP = N_Q_HEADS // N_KV_HEADS
QKV_DIM = (N_Q_HEADS + 2 * N_KV_HEADS) * HEAD_DIM
WINDOW = 128
BLOCK = 128
ROPE_BASE = 10000.0
REC_EXPAND = 128
REC_HEADS = D_MODEL // REC_EXPAND
REC_DK = REC_EXPAND
REC_DV = D_MODEL // REC_HEADS
REC_IN_DIM = 3 * REC_HEADS * REC_DK + 2 * REC_HEADS * REC_DV
CHUNK = 64
D_FF = 2816
MACARON_WEIGHT = 0.5
EPS = 1e-6
MASK_VALUE = -1e30

kernel_name = 'hybrid_diffusion_swa_hgrn2_macaron_step'


def rmsnorm(x, g):
    xf = x.astype(jnp.float32)
    y = xf * lax.rsqrt(jnp.mean(xf * xf, axis=-1, keepdims=True) + EPS)
    return (y * g.astype(jnp.float32)).astype(x.dtype)


def sublayer_in(x, g, mod, slot):
    return rmsnorm(x, g) * (1 + mod[..., slot, 1, :]) + mod[..., slot, 0, :]


def sublayer_out(x, out, g, mod, slot, weight):
    return x + weight * mod[..., slot, 2, :] * rmsnorm(out, g)


def swiglu(h, w_in, w_out):
    a, b = jnp.split(h @ w_in, 2, axis=-1)
    return (jax.nn.silu(a) * b) @ w_out


def ffn_sublayer(x, g_pre, g_post, mod, slot, w_in, w_out):
    h = sublayer_in(x, g_pre, mod, slot)
    return sublayer_out(x, swiglu(h, w_in, w_out), g_post, mod, slot, MACARON_WEIGHT)


def attn_project(h, w_qkv):
    B, L, _ = h.shape
    q, k, v = jnp.split(h @ w_qkv, [N_Q_HEADS * HEAD_DIM, (N_Q_HEADS + N_KV_HEADS) * HEAD_DIM], axis=-1)
    return (q.reshape(B, L, N_KV_HEADS, GQA_GROUP, HEAD_DIM),
            k.reshape(B, L, N_KV_HEADS, HEAD_DIM),
            v.reshape(B, L, N_KV_HEADS, HEAD_DIM))


def axial_rope(x, T):
    rows = T // GRID_W
    row = jnp.repeat(jnp.arange(rows), GRID_W).astype(jnp.float32)
    col = jnp.tile(jnp.arange(GRID_W), rows).astype(jnp.float32)
    nf = HEAD_DIM // 4
    inv = ROPE_BASE ** (-jnp.arange(nf, dtype=jnp.float32) / nf)
    bshape = (1, T) + (1,) * (x.ndim - 3) + (nf,)

    def rot(xh, pos):
        ang = (pos[:, None] * inv[None, :]).reshape(bshape)
        cos = jnp.cos(ang).astype(x.dtype)
        sin = jnp.sin(ang).astype(x.dtype)
        x1, x2 = jnp.split(xh, 2, axis=-1)
        return jnp.concatenate([x1 * cos - x2 * sin, x1 * sin + x2 * cos], axis=-1)

    half = HEAD_DIM // 2
    return jnp.concatenate([rot(x[..., :half], row), rot(x[..., half:], col)], axis=-1)


def softmax_with_sink(s, sink):
    sk = sink.astype(jnp.float32).reshape(1, N_KV_HEADS, GQA_GROUP, 1, 1)
    m = jnp.maximum(jnp.max(s, axis=-1, keepdims=True), sk)
    e = jnp.exp(s - m)
    return e / (jnp.sum(e, axis=-1, keepdims=True) + jnp.exp(sk - m))


def ctx_attention(q, k, v, sink):
    B, L = q.shape[:2]
    scale = HEAD_DIM ** -0.5

    def one(b):
        qb = lax.dynamic_slice_in_dim(q, b * BLOCK, BLOCK, axis=1)
        s = jnp.einsum('bqhgd,bkhd->bhgqk', qb, k).astype(jnp.float32) * scale
        p = softmax_with_sink(s, sink).astype(v.dtype)
        return jnp.einsum('bhgqk,bkhd->bqhgd', p, v)

    o = lax.map(one, jnp.arange(L // BLOCK))
    return jnp.moveaxis(o, 0, 1).reshape(B, L, N_Q_HEADS * HEAD_DIM)


def latent_attention(q, k, v, k_ctx, v_ctx, sink):
    B, T = q.shape[:2]
    scale = HEAD_DIM ** -0.5
    pad = ((0, 0), (BLOCK, BLOCK), (0, 0), (0, 0))
    k_pad = jnp.pad(k, pad)
    v_pad = jnp.pad(v, pad)

    def one(b):
        start = b * BLOCK
        qb = lax.dynamic_slice_in_dim(q, start, BLOCK, axis=1)
        kw = lax.dynamic_slice_in_dim(k_pad, start, 3 * BLOCK, axis=1)
        vw = lax.dynamic_slice_in_dim(v_pad, start, 3 * BLOCK, axis=1)
        qi = start + jnp.arange(BLOCK)
        kj = start - BLOCK + jnp.arange(3 * BLOCK)
        valid = (jnp.abs(qi[:, None] - kj[None, :]) <= WINDOW) & (kj >= 0)[None, :] & (kj < T)[None, :]
        s_w = jnp.einsum('bqhgd,bkhd->bhgqk', qb, kw).astype(jnp.float32) * scale
        s_w = jnp.where(valid, s_w, MASK_VALUE)
        s_c = jnp.einsum('bqhgd,bkhd->bhgqk', qb, k_ctx).astype(jnp.float32) * scale
        p = softmax_with_sink(jnp.concatenate([s_w, s_c], axis=-1), sink).astype(v.dtype)
        return (jnp.einsum('bhgqk,bkhd->bqhgd', p[..., :3 * BLOCK], vw)
                + jnp.einsum('bhgqk,bkhd->bqhgd', p[..., 3 * BLOCK:], v_ctx))

    o = lax.map(one, jnp.arange(T // BLOCK))
    return jnp.moveaxis(o, 0, 1).reshape(B, T, N_Q_HEADS * HEAD_DIM)


def hgrn_gates(z, lb):
    zf = z.astype(jnp.float32)
    lbf = lb.reshape(REC_HEADS, REC_DK)
    logf = jnp.log(lbf + (1 - lbf) * jax.nn.sigmoid(zf))
    key = (1 - lbf) * jax.nn.sigmoid(-zf)
    return logf, key


def chunk_recurrence(q, k, v, logf, s0):
    B, T, H, _ = q.shape
    nc = T // CHUNK
    rs = lambda t: t.reshape(B, nc, CHUNK, H, t.shape[-1])
    q, k, v, logf = rs(q), rs(k), rs(v), rs(logf)
    bcum = jnp.cumsum(logf, axis=2)
    blast = bcum[:, :, -1]
    q_dec = q * jnp.exp(bcum)
    k_inv = k * jnp.exp(-bcum)
    causal = jnp.tril(jnp.ones((CHUNK, CHUNK), dtype=bool))
    a = jnp.where(causal, jnp.einsum('bnchk,bnshk->bnhcs', q_dec, k_inv), 0.0)
    o_intra = jnp.einsum('bnhcs,bnshv->bnchv', a, v)
    k_end = k * jnp.exp(blast[:, :, None] - bcum)
    upd = jnp.einsum('bnshk,bnshv->bnhkv', k_end, v)
    dec = jnp.exp(blast)

    def step(s, xs):
        d, u = xs
        return d[..., None] * s + u, s

    s_fin, s_start = lax.scan(step, s0, (jnp.moveaxis(dec, 1, 0), jnp.moveaxis(upd, 1, 0)))
    s_start = jnp.moveaxis(s_start, 0, 1)
    o_inter = jnp.einsum('bnchk,bnhkv->bnchv', q_dec, s_start)
    return (o_intra + o_inter).reshape(B, T, H, v.shape[-1]), s_fin


def hgrn_mixer(h, s0, w_in, lb_f, lb_b, g_norm, w_out):
    B, T, _ = h.shape
    q, i_in, z_f, z_b, g = jnp.split(h @ w_in, [REC_HEADS * REC_DK, REC_HEADS * (REC_DK + REC_DV),
                                              REC_HEADS * (2 * REC_DK + REC_DV), REC_HEADS * (3 * REC_DK + REC_DV)], axis=-1)
    shk = (B, T, REC_HEADS, REC_DK)
    qf = jax.nn.silu(q.reshape(shk).astype(jnp.float32)) * (REC_DK ** -0.5)
    vf = i_in.reshape(B, T, REC_HEADS, REC_DV).astype(jnp.float32)
    logf_f, k_f = hgrn_gates(z_f.reshape(shk), lb_f)
    logf_b, k_b = hgrn_gates(z_b.reshape(shk), lb_b)
    s0f = s0.astype(jnp.float32)
    o_f, s_f = chunk_recurrence(qf, k_f, vf, logf_f, s0f[:, 0])
    flip = lambda t: jnp.flip(t, axis=1)
    o_b, s_b = chunk_recurrence(flip(qf), flip(k_b), flip(vf), flip(logf_b), s0f[:, 1])
    o = rmsnorm(o_f + flip(o_b), g_norm.reshape(REC_HEADS, REC_DV)).reshape(B, T, REC_HEADS * REC_DV)
    o = o.astype(h.dtype) * jax.nn.silu(g)
    return o @ w_out, jnp.stack([s_f, s_b], axis=1)


def setup_inputs(seed: int = 0) -> dict:
    key = jax.random.key(seed)
    ks = jax.random.split(key, 20)
    nrm = lambda k, shape, s: jax.random.normal(k, shape, jnp.float32) * s
    D = D_MODEL
    return {
        'x_prompt': nrm(ks[0], (BATCH, SEQ, D), 1.0),
        'x_sample': nrm(ks[1], (DEC_BATCH, DEC_SEQ, D), 1.0),
        'c': nrm(ks[2], (DEC_BATCH, D), 1.0),
        'cache_k': nrm(ks[3], (DEC_BATCH, N_ATTN_LAYERS, PAST_LEN, N_KV_HEADS, HEAD_DIM), 1.0),
        'cache_v': nrm(ks[4], (DEC_BATCH, N_ATTN_LAYERS, PAST_LEN, N_KV_HEADS, HEAD_DIM), 1.0),
        'state_s': nrm(ks[5], (DEC_BATCH, N_REC_LAYERS, 2, REC_HEADS, REC_DK, REC_DV), 0.5),
        'c_ctx': nrm(ks[6], (D,), 1.0),
        'w_ada': nrm(ks[7], (DEPTH, D, 9 * D), 0.5 * D ** -0.5),
        'b_ada': nrm(ks[8], (DEPTH, 9 * D), 0.01),
        'norm_pre': 1.0 + nrm(ks[9], (DEPTH, 3, D), 0.02),
        'norm_post': 1.0 + nrm(ks[10], (DEPTH, 3, D), 0.02),
        'w_ffn_in': nrm(ks[11], (DEPTH, 2, D, 2 * D_FF), D ** -0.5),
        'w_ffn_out': nrm(ks[12], (DEPTH, 2, D_FF, D), D_FF ** -0.5),
        'w_qkv': nrm(ks[13], (N_ATTN_LAYERS, D, QKV_DIM), D ** -0.5),
        'w_attn_out': nrm(ks[14], (N_ATTN_LAYERS, N_Q_HEADS * HEAD_DIM, D), D ** -0.5),
        'attn_sink': nrm(ks[15], (N_ATTN_LAYERS, N_Q_HEADS), 0.5),
        'w_rec_in': nrm(ks[16], (N_REC_LAYERS, D, REC_IN_DIM), D ** -0.5),
        'rec_lb_logits': nrm(ks[17], (2, DEPTH, REC_HEADS * REC_DK), 0.1),
        'rec_norm': 1.0 + nrm(ks[18], (N_REC_LAYERS, REC_HEADS * REC_DV), 0.02),
        'w_rec_out': nrm(ks[19], (N_REC_LAYERS, REC_HEADS * REC_DV, D), D ** -0.5),
    }


def reference(x_prompt, x_sample, c, cache_k, cache_v, state_s, c_ctx, w_ada, b_ada, norm_pre, norm_post,
              w_ffn_in, w_ffn_out, w_qkv, w_attn_out, attn_sink, w_rec_in, rec_lb_logits, rec_norm, w_rec_out):
    xp, xs = x_prompt, x_sample
    Bp = xp.shape[0]
    Bs, T = xs.shape[:2]
    lb_soft = jax.nn.softmax(rec_lb_logits.astype(jnp.float32), axis=1)
    lb_all = jnp.cumsum(lb_soft, axis=1) - lb_soft[:, :1]
    new_k, new_v, new_s = [], [], []
    for i in range(DEPTH):
        mod_p = (jax.nn.silu(c_ctx) @ w_ada[i] + b_ada[i]).reshape(3, 3, D_MODEL)
        mod_s = (jax.nn.silu(c) @ w_ada[i] + b_ada[i]).reshape(Bs, 1, 3, 3, D_MODEL)
        xp = ffn_sublayer(xp, norm_pre[i, 0], norm_post[i, 0], mod_p, 0, w_ffn_in[i, 0], w_ffn_out[i, 0])
        xs = ffn_sublayer(xs, norm_pre[i, 0], norm_post[i, 0], mod_s, 0, w_ffn_in[i, 0], w_ffn_out[i, 0])
        hp = sublayer_in(xp, norm_pre[i, 1], mod_p, 1)
        hs = sublayer_in(xs, norm_pre[i, 1], mod_s, 1)
        j = i // N_MIXERS
        if i % N_MIXERS == 0:
            qp, kp, vp = attn_project(hp, w_qkv[j])
            op = ctx_attention(qp, kp, vp, attn_sink[j]) @ w_attn_out[j]
            qs, ks_, vs = attn_project(hs, w_qkv[j])
            qs = axial_rope(qs, T)
            ks_ = axial_rope(ks_, T)
            os_ = latent_attention(qs, ks_, vs, cache_k[:, j], cache_v[:, j], attn_sink[j]) @ w_attn_out[j]
            new_k.append(kp)
            new_v.append(vp)
        else:
            s_zero = jnp.zeros((Bp, 2, REC_HEADS, REC_DK, REC_DV), jnp.float32)
            op, sp = hgrn_mixer(hp, s_zero, w_rec_in[j], lb_all[0, i], lb_all[1, i], rec_norm[j], w_rec_out[j])
            os_, _ = hgrn_mixer(hs, state_s[:, j], w_rec_in[j], lb_all[0, i], lb_all[1, i], rec_norm[j], w_rec_out[j])
            new_s.append(sp)
        xp = sublayer_out(xp, op, norm_post[i, 1], mod_p, 1, 1.0)
        xs = sublayer_out(xs, os_, norm_post[i, 1], mod_s, 1, 1.0)
        xp = ffn_sublayer(xp, norm_pre[i, 2], norm_post[i, 2], mod_p, 2, w_ffn_in[i, 1], w_ffn_out[i, 1])
        xs = ffn_sublayer(xs, norm_pre[i, 2], norm_post[i, 2], mod_s, 2, w_ffn_in[i, 1], w_ffn_out[i, 1])
    new_cache_k = jnp.stack(new_k, axis=1)
    new_cache_v = jnp.stack(new_v, axis=1)
    new_state_s = jnp.stack(new_s, axis=1)
    return (xp, xs, new_cache_k, new_cache_v, new_state_s)
```

```python
import functools

import jax
import jax.numpy as jnp
from jax import lax
from jax.experimental import pallas as pl
from jax.experimental.pallas import tpu as pltpu

F32 = jnp.float32
BF16 = jnp.bfloat16

D_MODEL = 1024
BATCH = 32
SEQ = 256
DEPTH = 2
DEC_BATCH = 2
DEC_SEQ = 1024
PAST_LEN = 256
GRID_W = 64
HEAD_DIM = 64
N_Q_HEADS = 16
N_KV_HEADS = 4
QKV_DIM = (N_Q_HEADS + 2 * N_KV_HEADS) * HEAD_DIM
ATTN_BLOCK = 128
ROPE_BASE = 10000.0
REC_HEADS = 8
REC_DK = 128
REC_DV = 128
REC_IN_DIM = 5 * D_MODEL
CHUNK = 64
D_FF = 2816
EPS = 1e-6
MASK_VALUE = -1e30

N_PROMPT = BATCH * SEQ
N_SAMPLE = DEC_BATCH * DEC_SEQ
N_TOK = N_PROMPT + N_SAMPLE
N_COND = 1 + DEC_BATCH

LANES = 128
SUBLANES = 8
VMEM_BYTES_V7X = 64 * 1024 * 1024

ROW_TILE = 512
FF_CHUNK = 256
ADA_TILE = 1024
REC_GROUP = 256
REC_HEADS_PER_STEP = 2

PROMPT_TILES = N_PROMPT // ROW_TILE
TILES_PER_SAMPLE = DEC_SEQ // ROW_TILE


def _vmem_limit(nbytes):
    return int(min(VMEM_BYTES_V7X - 8 * 1024 * 1024, max(nbytes, 16 * 1024 * 1024)))


def _sigmoid(x):
    return 1.0 / (1.0 + jnp.exp(-x))


def _rms(x, g):
    ms = jnp.mean(x * x, axis=-1, keepdims=True)
    return x * lax.rsqrt(ms + EPS) * g


def _mod_in(x, g_pre, mod_ref, slot):
    shift = mod_ref[slot * 3:slot * 3 + 1, :]
    scale = mod_ref[slot * 3 + 1:slot * 3 + 2, :]
    return _rms(x, g_pre) * (1.0 + scale) + shift


def _gate(mod_ref, slot):
    return mod_ref[slot * 3 + 2:slot * 3 + 3, :]


def _dot(a, b):
    return jnp.dot(a, b, preferred_element_type=F32)


def _dot_nt(a, b):
    return lax.dot_general(a, b, (((1,), (1,)), ((), ())), preferred_element_type=F32)


def _dot_tn(a, b):
    return lax.dot_general(a, b, (((0,), (0,)), ((), ())), preferred_element_type=F32)


def _tile_group(i):
    return jnp.where(i < PROMPT_TILES, 0, 1 + (i - PROMPT_TILES) // TILES_PER_SAMPLE)


def _row_spec(width):
    return pl.BlockSpec((ROW_TILE, width), lambda i: (i, 0))


def _mod_spec():
    return pl.BlockSpec((None, 9, D_MODEL), lambda i: (_tile_group(i), 0, 0))


def _const_spec(shape):
    nd = len(shape)
    return pl.BlockSpec(shape, lambda *_: (0,) * nd, pipeline_mode=pl.Buffered(1))


def _ada_kernel(cond_ref, w_ref, b_ref, o_ref):
    c = cond_ref[...]
    s = (c * _sigmoid(c)).astype(BF16)
    o_ref[...] = _dot(s, w_ref[...].astype(BF16)) + b_ref[...]


def _ada_mod(cond, w_ada, b_ada):
    n_out = 9 * D_MODEL
    out = pl.pallas_call(
        _ada_kernel,
        grid=(DEPTH, n_out // ADA_TILE),
        in_specs=[
            pl.BlockSpec((SUBLANES, D_MODEL), lambda l, j: (0, 0)),
            pl.BlockSpec((None, D_MODEL, ADA_TILE), lambda l, j: (l, 0, j)),
            pl.BlockSpec((None, 1, ADA_TILE), lambda l, j: (l, 0, j)),
        ],
        out_specs=pl.BlockSpec((None, SUBLANES, ADA_TILE), lambda l, j: (l, 0, j)),
        out_shape=jax.ShapeDtypeStruct((DEPTH, SUBLANES, n_out), F32),
        compiler_params=pltpu.CompilerParams(
            dimension_semantics=("parallel", "parallel"),
            vmem_limit_bytes=_vmem_limit(4 * D_MODEL * ADA_TILE * 4)),
        name="ada_mod",
    )(cond, w_ada, b_ada.reshape(DEPTH, 1, n_out))
    return out[:, :N_COND].reshape(DEPTH, N_COND, 9, D_MODEL)


def _ffn_kernel(x_ref, mod_ref, gpre_ref, gpost_ref, win_ref, wout_ref, o_ref, g_scr, *, slot):
    x = x_ref[...]
    h = _mod_in(x, gpre_ref[...], mod_ref, slot).astype(BF16)
    for c in range(D_FF // FF_CHUNK):
        lo = c * FF_CHUNK
        a = _dot(h, win_ref[:, lo:lo + FF_CHUNK])
        b = _dot(h, win_ref[:, D_FF + lo:D_FF + lo + FF_CHUNK])
        g_scr[:, lo:lo + FF_CHUNK] = (a * _sigmoid(a) * b).astype(BF16)
    y = _dot(g_scr[...], wout_ref[...])
    o_ref[...] = x + (0.5 * _gate(mod_ref, slot)) * _rms(y, gpost_ref[...])


def _ffn(x, mod, g_pre, g_post, w_in, w_out, slot):
    vmem = (2 * D_MODEL * D_FF + D_FF * D_MODEL) * 2 + 4 * ROW_TILE * D_MODEL * 4 \
        + ROW_TILE * D_FF * 2 + 8 * ROW_TILE * FF_CHUNK * 4 + 4 * ROW_TILE * D_MODEL * 4
    return pl.pallas_call(
        functools.partial(_ffn_kernel, slot=slot),
        grid=(N_TOK // ROW_TILE,),
        in_specs=[
            _row_spec(D_MODEL),
            _mod_spec(),
            _const_spec((1, D_MODEL)),
            _const_spec((1, D_MODEL)),
            _const_spec((D_MODEL, 2 * D_FF)),
            _const_spec((D_FF, D_MODEL)),
        ],
        out_specs=_row_spec(D_MODEL),
        out_shape=jax.ShapeDtypeStruct((N_TOK, D_MODEL), F32),
        scratch_shapes=[pltpu.VMEM((ROW_TILE, D_FF), BF16)],
        compiler_params=pltpu.CompilerParams(
            dimension_semantics=("parallel",), vmem_limit_bytes=_vmem_limit(vmem)),
        name="ffn_sublayer",
    )(x, mod, g_pre.reshape(1, D_MODEL), g_post.reshape(1, D_MODEL), w_in, w_out)


def _qkv_kernel(x_ref, mod_ref, gpre_ref, w_ref, cos_ref, sin_ref, q_ref, k_ref, v_ref):
    i = pl.program_id(0)
    scale = HEAD_DIM ** -0.5
    h = _mod_in(x_ref[...], gpre_ref[...], mod_ref, 1).astype(BF16)
    qkv = _dot(h, w_ref[...])
    nq = N_Q_HEADS * HEAD_DIM
    nk = N_KV_HEADS * HEAD_DIM
    v_ref[...] = qkv[:, nq + nk:]

    @pl.when(i < PROMPT_TILES)
    def _():
        q_ref[...] = (qkv[:, :nq] * scale).astype(BF16)
        k_ref[...] = qkv[:, nq:nq + nk]

    @pl.when(i >= PROMPT_TILES)
    def _():
        cos = cos_ref[...]
        sin = sin_ref[...]
        lane = lax.broadcasted_iota(jnp.int32, (ROW_TILE, LANES), 1)
        first = (lane & (HEAD_DIM // 4)) == 0

        def rope(xg):
            up = pltpu.roll(xg, LANES - HEAD_DIM // 4, 1)
            down = pltpu.roll(xg, HEAD_DIM // 4, 1)
            return xg * cos + jnp.where(first, up, down) * sin

        for j in range(nq // LANES):
            q_ref[:, j * LANES:(j + 1) * LANES] = (rope(qkv[:, j * LANES:(j + 1) * LANES]) * scale).astype(BF16)
        for j in range(nk // LANES):
            k_ref[:, j * LANES:(j + 1) * LANES] = rope(qkv[:, nq + j * LANES:nq + (j + 1) * LANES])


def _rope_tables():
    t = jnp.arange(DEC_SEQ)
    row = (t // GRID_W).astype(F32)
    col = (t % GRID_W).astype(F32)
    nf = HEAD_DIM // 4
    inv = ROPE_BASE ** (-jnp.arange(nf, dtype=F32) / nf)
    ar = row[:, None] * inv[None, :]
    ac = col[:, None] * inv[None, :]
    cos = jnp.concatenate([jnp.cos(ar), jnp.cos(ar), jnp.cos(ac), jnp.cos(ac)], axis=-1)
    sin = jnp.concatenate([-jnp.sin(ar), jnp.sin(ar), -jnp.sin(ac), jnp.sin(ac)], axis=-1)
    reps = LANES // HEAD_DIM
    return jnp.tile(cos, (1, reps)), jnp.tile(sin, (1, reps))


def _qkv(x, mod, g_pre, w_qkv):
    cos, sin = _rope_tables()
    tab_spec = pl.BlockSpec(
        (ROW_TILE, LANES), lambda i: (jnp.maximum(i - PROMPT_TILES, 0) % TILES_PER_SAMPLE, 0))
    nk = N_KV_HEADS * HEAD_DIM
    vmem = D_MODEL * QKV_DIM * 2 + 6 * ROW_TILE * D_MODEL * 4 + 4 * ROW_TILE * QKV_DIM * 4
    return pl.pallas_call(
        _qkv_kernel,
        grid=(N_TOK // ROW_TILE,),
        in_specs=[
            _row_spec(D_MODEL),
            _mod_spec(),
            _const_spec((1, D_MODEL)),
            _const_spec((D_MODEL, QKV_DIM)),
            tab_spec,
            tab_spec,
        ],
        out_specs=[_row_spec(D_MODEL), _row_spec(nk), _row_spec(nk)],
        out_shape=[
            jax.ShapeDtypeStruct((N_TOK, D_MODEL), BF16),
            jax.ShapeDtypeStruct((N_TOK, nk), F32),
            jax.ShapeDtypeStruct((N_TOK, nk), F32),
        ],
        compiler_params=pltpu.CompilerParams(
            dimension_semantics=("parallel",), vmem_limit_bytes=_vmem_limit(vmem)),
        name="attn_qkv",
    )(x, mod, g_pre.reshape(1, D_MODEL), w_qkv, cos, sin)


def _lane_halves(ref_or_val, hkv, rows):
    grp = ref_or_val[:, (hkv // 2) * LANES:(hkv // 2 + 1) * LANES]
    lane = lax.broadcasted_iota(jnp.int32, (rows, LANES), 1)
    in_low = lane < HEAD_DIM
    if hkv % 2 == 0:
        lo = jnp.where(in_low, grp, 0.0)
        hi = pltpu.roll(lo, HEAD_DIM, 1)
    else:
        hi = jnp.where(in_low, 0.0, grp)
        lo = pltpu.roll(hi, HEAD_DIM, 1)
    return lo.astype(BF16), hi.astype(BF16)


def _softmax_sink(s, sink):
    m = jnp.maximum(jnp.max(s, axis=-1, keepdims=True), sink)
    e = jnp.exp(s - m)
    denom = jnp.sum(e, axis=-1, keepdims=True) + jnp.exp(sink - m)
    return (e * (1.0 / denom)).astype(BF16)


def _attend(sink_ref, q_ref, o_ref, keys, vals, n_keys, valid):
    for hkv in range(N_KV_HEADS):
        k_lo, k_hi = _lane_halves(keys, hkv, n_keys)
        v_lo, v_hi = _lane_halves(vals, hkv, n_keys)
        for j in (2 * hkv, 2 * hkv + 1):
            q2 = q_ref[:, j * LANES:(j + 1) * LANES]
            s_a = _dot_nt(q2, k_lo)
            s_b = _dot_nt(q2, k_hi)
            if valid is not None:
                s_a = jnp.where(valid, s_a, MASK_VALUE)
                s_b = jnp.where(valid, s_b, MASK_VALUE)
            p_a = _softmax_sink(s_a, sink_ref[2 * j])
            p_b = _softmax_sink(s_b, sink_ref[2 * j + 1])
            o_ref[:, j * LANES:(j + 1) * LANES] = (_dot(p_a, v_lo) + _dot(p_b, v_hi)).astype(BF16)


def _attn_prompt_kernel(sink_ref, q_ref, k_ref, v_ref, o_ref):
    _attend(sink_ref, q_ref, o_ref, k_ref[...], v_ref[...], SEQ, None)


def _attn_prompt(sink, q, k, v):
    nk = N_KV_HEADS * HEAD_DIM
    return pl.pallas_call(
        _attn_prompt_kernel,
        grid=(BATCH,),
        in_specs=[
            pl.BlockSpec(memory_space=pltpu.SMEM),
            pl.BlockSpec((SEQ, D_MODEL), lambda b: (b, 0)),
            pl.BlockSpec((SEQ, nk), lambda b: (b, 0)),
            pl.BlockSpec((SEQ, nk), lambda b: (b, 0)),
        ],
        out_specs=pl.BlockSpec((SEQ, D_MODEL), lambda b: (b, 0)),
        out_shape=jax.ShapeDtypeStruct((N_PROMPT, D_MODEL), BF16),
        compiler_params=pltpu.CompilerParams(dimension_semantics=("parallel",)),
        name="attn_prompt",
    )(sink, q, k, v)


def _attn_sample_kernel(sink_ref, q_ref, k_ref, v_ref, ck_ref, cv_ref, o_ref):
    qb = pl.program_id(1)
    nblk = DEC_SEQ // ATTN_BLOCK
    B = ATTN_BLOCK
    starts = [
        pl.multiple_of(jnp.maximum(qb - 1, 0) * B, B),
        pl.multiple_of(qb * B, B),
        pl.multiple_of(jnp.minimum(qb + 1, nblk - 1) * B, B),
    ]
    keys = jnp.concatenate([k_ref[pl.ds(s, B), :] for s in starts] + [ck_ref[...]], axis=0)
    vals = jnp.concatenate([v_ref[pl.ds(s, B), :] for s in starts] + [cv_ref[...]], axis=0)
    n_keys = 3 * B + PAST_LEN
    r = lax.broadcasted_iota(jnp.int32, (B, n_keys), 0)
    c = lax.broadcasted_iota(jnp.int32, (B, n_keys), 1)
    prev_bad = (c < B) & ((c < r) | (qb == 0))
    next_bad = (c >= 2 * B) & (c < 3 * B) & (((c - 2 * B) > r) | (qb == nblk - 1))
    valid = jnp.logical_not(prev_bad | next_bad)
    _attend(sink_ref, q_ref, o_ref, keys, vals, n_keys, valid)


def _attn_sample(sink, q, k, v, cache_k, cache_v):
    nk = N_KV_HEADS * HEAD_DIM
    nblk = DEC_SEQ // ATTN_BLOCK
    q_off = N_PROMPT // ATTN_BLOCK
    kv_off = N_PROMPT // DEC_SEQ
    return pl.pallas_call(
        _attn_sample_kernel,
        grid=(DEC_BATCH, nblk),
        in_specs=[
            pl.BlockSpec(memory_space=pltpu.SMEM),
            pl.BlockSpec((ATTN_BLOCK, D_MODEL), lambda b, t: (q_off + b * nblk + t, 0)),
            pl.BlockSpec((DEC_SEQ, nk), lambda b, t: (kv_off + b, 0)),
            pl.BlockSpec((DEC_SEQ, nk), lambda b, t: (kv_off + b, 0)),
            pl.BlockSpec((None, PAST_LEN, nk), lambda b, t: (b, 0, 0)),
            pl.BlockSpec((None, PAST_LEN, nk), lambda b, t: (b, 0, 0)),
        ],
        out_specs=pl.BlockSpec((ATTN_BLOCK, D_MODEL), lambda b, t: (b * nblk + t, 0)),
        out_shape=jax.ShapeDtypeStruct((N_SAMPLE, D_MODEL), BF16),
        compiler_params=pltpu.CompilerParams(dimension_semantics=("parallel", "parallel")),
        name="attn_sample",
    )(sink, q, k, v, cache_k, cache_v)


def _mix_out_kernel(o_ref, x_ref, mod_ref, gpost_ref, w_ref, out_ref):
    y = _dot(o_ref[...], w_ref[...])
    out_ref[...] = x_ref[...] + _gate(mod_ref, 1) * _rms(y, gpost_ref[...])


def _mix_out(o, x, mod, g_post, w):
    vmem = D_MODEL * D_MODEL * 2 + 10 * ROW_TILE * D_MODEL * 4
    return pl.pallas_call(
        _mix_out_kernel,
        grid=(N_TOK // ROW_TILE,),
        in_specs=[
            _row_spec(D_MODEL),
            _row_spec(D_MODEL),
            _mod_spec(),
            _const_spec((1, D_MODEL)),
            _const_spec((D_MODEL, D_MODEL)),
        ],
        out_specs=_row_spec(D_MODEL),
        out_shape=jax.ShapeDtypeStruct((N_TOK, D_MODEL), F32),
        compiler_params=pltpu.CompilerParams(
            dimension_semantics=("parallel",), vmem_limit_bytes=_vmem_limit(vmem)),
        name="mixer_out",
    )(o, x, mod, g_post.reshape(1, D_MODEL), w)


def _rec_in_kernel(x_ref, mod_ref, gpre_ref, w_ref, p_ref):
    h = _mod_in(x_ref[...], gpre_ref[...], mod_ref, 1).astype(BF16)
    for part in range(5):
        lo = part * D_MODEL
        y = _dot(h, w_ref[:, lo:lo + D_MODEL])
        if part == 0:
            y = y * _sigmoid(y) * (REC_DK ** -0.5)
        elif part == 4:
            y = y * _sigmoid(y)
        p_ref[:, lo:lo + D_MODEL] = y


def _rec_in(x, mod, g_pre, w):
    vmem = D_MODEL * REC_IN_DIM * 2 + 4 * ROW_TILE * D_MODEL * 4 + 2 * ROW_TILE * REC_IN_DIM * 4 \
        + 4 * ROW_TILE * D_MODEL * 4
    return pl.pallas_call(
        _rec_in_kernel,
        grid=(N_TOK // ROW_TILE,),
        in_specs=[
            _row_spec(D_MODEL),
            _mod_spec(),
            _const_spec((1, D_MODEL)),
            _const_spec((D_MODEL, REC_IN_DIM)),
        ],
        out_specs=_row_spec(REC_IN_DIM),
        out_shape=jax.ShapeDtypeStruct((N_TOK, REC_IN_DIM), F32),
        compiler_params=pltpu.CompilerParams(
            dimension_semantics=("parallel",), vmem_limit_bytes=_vmem_limit(vmem)),
        name="rec_in",
    )(x, mod, g_pre.reshape(1, D_MODEL), w)


def _cumsum_rows(tri, x):
    hi = x.astype(BF16)
    r1 = x - hi.astype(F32)
    mid = r1.astype(BF16)
    lo = (r1 - mid.astype(F32)).astype(BF16)
    return _dot(tri, hi) + _dot(tri, mid) + _dot(tri, lo)


def _rec_kernel(*refs, seq_len, has_init):
    if has_init:
        q_ref, v_ref, zf_ref, zb_ref, sg_ref, lbf_ref, lbb_ref, gn_ref, s0_ref, o_ref, o_scr = refs
        st_ref = None
    else:
        q_ref, v_ref, zf_ref, zb_ref, sg_ref, lbf_ref, lbb_ref, gn_ref, o_ref, st_ref, o_scr = refs
        s0_ref = None
    G = REC_GROUP
    n_groups = seq_len // G
    n_chunks = G // CHUNK
    r = lax.broadcasted_iota(jnp.int32, (G, G), 0)
    c = lax.broadcasted_iota(jnp.int32, (G, G), 1)
    same = (r // CHUNK) == (c // CHUNK)
    masks = (same & (c <= r), same & (c >= r))
    tris = tuple(jnp.where(m, 1.0, 0.0).astype(BF16) for m in masks)

    for hh in range(REC_HEADS_PER_STEP):
        cs = slice(hh * REC_DK, (hh + 1) * REC_DK)
        for d in range(2):
            z = (zf_ref, zb_ref)[d][:, cs]
            lb = (lbf_ref, lbb_ref)[d][hh]
            e = jnp.exp(-jnp.abs(z))
            rcp = 1.0 / (1.0 + e)
            pos = z >= 0
            sig = jnp.where(pos, rcp, e * rcp)
            nsig = jnp.where(pos, e * rcp, rcp)
            logf = jnp.log(lb + (1.0 - lb) * sig)
            key = (1.0 - lb) * nsig
            if has_init:
                state = s0_ref[d, hh].T
            else:
                state = jnp.zeros((REC_DV, REC_DK), F32)
            g_order = range(n_groups) if d == 0 else range(n_groups - 1, -1, -1)
            c_order = range(n_chunks) if d == 0 else range(n_chunks - 1, -1, -1)
            for g in g_order:
                rows = slice(g * G, (g + 1) * G)
                bc = _cumsum_rows(tris[d], logf[rows])
                q_g = q_ref[rows, cs]
                v_g = v_ref[rows, cs].astype(BF16)
                key_g = key[rows]
                q_dec = (q_g * jnp.exp(bc)).astype(BF16)
                k_inv = (key_g * jnp.exp(-bc)).astype(BF16)
                a = jnp.where(masks[d], _dot_nt(q_dec, k_inv), 0.0).astype(BF16)
                o_g = _dot(a, v_g)
                for ch in c_order:
                    rr = slice(ch * CHUNK, (ch + 1) * CHUNK)
                    edge = ch * CHUNK + (CHUNK - 1 if d == 0 else 0)
                    bl = bc[edge:edge + 1, :]
                    o_c = o_g[rr] + _dot_nt(q_dec[rr], state.astype(BF16))
                    k_end = (key_g[rr] * jnp.exp(bl - bc[rr])).astype(BF16)
                    state = state * jnp.exp(bl) + _dot_tn(v_g[rr], k_end)
                    out_rows = slice(g * G + ch * CHUNK, g * G + (ch + 1) * CHUNK)
                    if d == 0:
                        o_scr[out_rows, cs] = o_c
                    else:
                        o_scr[out_rows, cs] += o_c
            if st_ref is not None:
                st_ref[d, hh] = state.T
        o = o_scr[:, cs]
        o_ref[:, cs] = (_rms(o, gn_ref[hh]) * sg_ref[:, cs]).astype(BF16)


def _rec(proj, lb_f, lb_b, g_norm, s0, *, seq_len, n_seq, row_block_off):
    hb = REC_HEADS_PER_STEP
    w = hb * REC_DK
    n_hp = REC_HEADS // hb
    has_init = s0 is not None

    def col_spec(part):
        return pl.BlockSpec((seq_len, w), lambda b, h: (row_block_off + b, part * n_hp + h))

    head_vec_spec = pl.BlockSpec((hb, 1, REC_DK), lambda b, h: (h, 0, 0))
    in_specs = [col_spec(0), col_spec(1), col_spec(2), col_spec(3), col_spec(4),
                head_vec_spec, head_vec_spec, head_vec_spec]
    args = [proj, proj, proj, proj, proj,
            lb_f.reshape(REC_HEADS, 1, REC_DK), lb_b.reshape(REC_HEADS, 1, REC_DK),
            g_norm.reshape(REC_HEADS, 1, REC_DV)]
    state_spec = pl.BlockSpec((None, 2, hb, REC_DK, REC_DV), lambda b, h: (b, 0, h, 0, 0))
    o_spec = pl.BlockSpec((seq_len, w), lambda b, h: (b, h))
    o_shape = jax.ShapeDtypeStruct((n_seq * seq_len, D_MODEL), BF16)
    if has_init:
        in_specs.append(state_spec)
        args.append(s0)
        out_specs, out_shape = o_spec, o_shape
    else:
        out_specs = [o_spec, state_spec]
        out_shape = [o_shape, jax.ShapeDtypeStruct((n_seq, 2, REC_HEADS, REC_DK, REC_DV), F32)]
    return pl.pallas_call(
        functools.partial(_rec_kernel, seq_len=seq_len, has_init=has_init),
        grid=(n_seq, n_hp),
        in_specs=in_specs,
        out_specs=out_specs,
        out_shape=out_shape,
        scratch_shapes=[pltpu.VMEM((seq_len, w), F32)],
        compiler_params=pltpu.CompilerParams(
            dimension_semantics=("parallel", "parallel"),
            vmem_limit_bytes=_vmem_limit(40 * seq_len * w * 4)),
        name="rec_scan_init" if has_init else "rec_scan",
    )(*args)


def kernel(x_prompt, x_sample, c, cache_k, cache_v, state_s, c_ctx, w_ada, b_ada, norm_pre, norm_post,
           w_ffn_in, w_ffn_out, w_qkv, w_attn_out, attn_sink, w_rec_in, rec_lb_logits, rec_norm, w_rec_out):
    x = jnp.concatenate([x_prompt.reshape(N_PROMPT, D_MODEL), x_sample.reshape(N_SAMPLE, D_MODEL)], axis=0)
    cond = jnp.concatenate([c_ctx[None], c, jnp.zeros((SUBLANES - N_COND, D_MODEL), F32)], axis=0)
    mods = _ada_mod(cond, w_ada, b_ada)

    lb_soft = jax.nn.softmax(rec_lb_logits.astype(F32), axis=1)
    lb_all = jnp.cumsum(lb_soft, axis=1) - lb_soft[:, :1]

    w_ffn_in_b = w_ffn_in.astype(BF16)
    w_ffn_out_b = w_ffn_out.astype(BF16)
    nk = N_KV_HEADS * HEAD_DIM
    new_k = new_v = new_s = None
    for i in range(DEPTH):
        mod = mods[i]
        x = _ffn(x, mod, norm_pre[i, 0], norm_post[i, 0], w_ffn_in_b[i, 0], w_ffn_out_b[i, 0], 0)
        j = i // 2
        if i % 2 == 0:
            q, k, v = _qkv(x, mod, norm_pre[i, 1], w_qkv[j].astype(BF16))
            o_p = _attn_prompt(attn_sink[j], q, k, v)
            o_s = _attn_sample(attn_sink[j], q, k, v,
                               cache_k[:, j].reshape(DEC_BATCH, PAST_LEN, nk),
                               cache_v[:, j].reshape(DEC_BATCH, PAST_LEN, nk))
            o = jnp.concatenate([o_p, o_s], axis=0)
            x = _mix_out(o, x, mod, norm_post[i, 1], w_attn_out[j].astype(BF16))
            new_k = k[:N_PROMPT].reshape(BATCH, 1, SEQ, N_KV_HEADS, HEAD_DIM)
            new_v = v[:N_PROMPT].reshape(BATCH, 1, SEQ, N_KV_HEADS, HEAD_DIM)
        else:
            proj = _rec_in(x, mod, norm_pre[i, 1], w_rec_in[j].astype(BF16))
            o_p, s_p = _rec(proj, lb_all[0, i], lb_all[1, i], rec_norm[j], None,
                            seq_len=SEQ, n_seq=BATCH, row_block_off=0)
            o_s = _rec(proj, lb_all[0, i], lb_all[1, i], rec_norm[j], state_s[:, j],
                       seq_len=DEC_SEQ, n_seq=DEC_BATCH, row_block_off=N_PROMPT // DEC_SEQ)
            o = jnp.concatenate([o_p, o_s], axis=0)
            x = _mix_out(o, x, mod, norm_post[i, 1], w_rec_out[j].astype(BF16))
            new_s = s_p.reshape(BATCH, 1, 2, REC_HEADS, REC_DK, REC_DV)
        x = _ffn(x, mod, norm_pre[i, 2], norm_post[i, 2], w_ffn_in_b[i, 1], w_ffn_out_b[i, 1], 2)
    y_prompt = x[:N_PROMPT].reshape(BATCH, SEQ, D_MODEL)
    y_sample = x[N_PROMPT:].reshape(DEC_BATCH, DEC_SEQ, D_MODEL)
    return (y_prompt, y_sample, new_k, new_v, new_s)
```

```python
import functools

import jax
import jax.numpy as jnp
from jax import lax
from jax.experimental import pallas as pl
from jax.experimental.pallas import tpu as pltpu

F32 = jnp.float32
BF16 = jnp.bfloat16

D_MODEL = 1024
BATCH = 32
SEQ = 256
DEPTH = 2
DEC_BATCH = 2
DEC_SEQ = 1024
PAST_LEN = 256
GRID_W = 64
HEAD_DIM = 64
N_Q_HEADS = 16
N_KV_HEADS = 4
QKV_DIM = (N_Q_HEADS + 2 * N_KV_HEADS) * HEAD_DIM
ATTN_BLOCK = 128
ROPE_BASE = 10000.0
REC_HEADS = 8
REC_DK = 128
REC_DV = 128
REC_IN_DIM = 5 * D_MODEL
CHUNK = 64
D_FF = 2816
EPS = 1e-6
MASK_VALUE = -1e30

N_PROMPT = BATCH * SEQ
N_SAMPLE = DEC_BATCH * DEC_SEQ
N_TOK = N_PROMPT + N_SAMPLE
N_COND = 1 + DEC_BATCH

LANES = 128
SUBLANES = 8
VMEM_BYTES_V7X = 64 * 1024 * 1024

ROW_TILE = 512
FF_CHUNK = 256
ADA_TILE = 1024
REC_GROUP = 256
REC_HEADS_PER_STEP = 2

PROMPT_TILES = N_PROMPT // ROW_TILE
TILES_PER_SAMPLE = DEC_SEQ // ROW_TILE


def _vmem_limit(nbytes):
    return int(min(VMEM_BYTES_V7X - 8 * 1024 * 1024, max(nbytes, 16 * 1024 * 1024)))


def _sigmoid(x):
    return 1.0 / (1.0 + jnp.exp(-x))


def _rms(x, g):
    ms = jnp.mean(x * x, axis=-1, keepdims=True)
    return x * lax.rsqrt(ms + EPS) * g


def _mod_in(x, g_pre, mod_ref, slot):
    shift = mod_ref[slot * 3:slot * 3 + 1, :]
    scale = mod_ref[slot * 3 + 1:slot * 3 + 2, :]
    return _rms(x, g_pre) * (1.0 + scale) + shift


def _gate(mod_ref, slot):
    return mod_ref[slot * 3 + 2:slot * 3 + 3, :]


def _dot(a, b):
    return jnp.dot(a, b, preferred_element_type=F32)


def _dot_nt(a, b):
    return lax.dot_general(a, b, (((1,), (1,)), ((), ())), preferred_element_type=F32)


def _dot_tn(a, b):
    return lax.dot_general(a, b, (((0,), (0,)), ((), ())), preferred_element_type=F32)


def _tile_group(i):
    return jnp.where(i < PROMPT_TILES, 0, 1 + (i - PROMPT_TILES) // TILES_PER_SAMPLE)


def _row_spec(width):
    return pl.BlockSpec((ROW_TILE, width), lambda i: (i, 0))


def _prompt_row_spec(width):
    return pl.BlockSpec((ROW_TILE, width), lambda i: (jnp.minimum(i, PROMPT_TILES - 1), 0))


def _sample_row_spec(width):
    return pl.BlockSpec((ROW_TILE, width), lambda i: (jnp.maximum(i - PROMPT_TILES, 0), 0))


def _mod_spec():
    return pl.BlockSpec((None, 9, D_MODEL), lambda i: (_tile_group(i), 0, 0))


def _const_spec(shape):
    nd = len(shape)
    return pl.BlockSpec(shape, lambda *_: (0,) * nd, pipeline_mode=pl.Buffered(1))


def _ada_kernel(cond_ref, w_ref, b_ref, o_ref):
    c = cond_ref[...]
    s = (c * _sigmoid(c)).astype(BF16)
    o_ref[...] = _dot(s, w_ref[...].astype(BF16)) + b_ref[...]


def _ada_mod(cond, w_ada, b_ada):
    n_out = 9 * D_MODEL
    out = pl.pallas_call(
        _ada_kernel,
        grid=(DEPTH, n_out // ADA_TILE),
        in_specs=[
            pl.BlockSpec((SUBLANES, D_MODEL), lambda l, j: (0, 0)),
            pl.BlockSpec((None, D_MODEL, ADA_TILE), lambda l, j: (l, 0, j)),
            pl.BlockSpec((None, 1, ADA_TILE), lambda l, j: (l, 0, j)),
        ],
        out_specs=pl.BlockSpec((None, SUBLANES, ADA_TILE), lambda l, j: (l, 0, j)),
        out_shape=jax.ShapeDtypeStruct((DEPTH, SUBLANES, n_out), F32),
        compiler_params=pltpu.CompilerParams(
            dimension_semantics=("parallel", "parallel"),
            vmem_limit_bytes=_vmem_limit(4 * D_MODEL * ADA_TILE * 4)),
        name="ada_mod",
    )(cond, w_ada, b_ada.reshape(DEPTH, 1, n_out))
    return out[:, :N_COND].reshape(DEPTH, N_COND, 9, D_MODEL)


def _ffn_kernel(*refs, slot, split_in, split_out):
    refs = list(refs)
    x_refs = [refs.pop(0) for _ in range(2 if split_in else 1)]
    mod_ref, gpre_ref, gpost_ref, win_ref, wout_ref = (refs.pop(0) for _ in range(5))
    o_refs = [refs.pop(0) for _ in range(2 if split_out else 1)]
    g_scr = refs.pop(0)
    is_prompt = pl.program_id(0) < PROMPT_TILES
    if split_in:
        x_scr = refs.pop(0)

        @pl.when(is_prompt)
        def _():
            x_scr[...] = x_refs[0][...]

        @pl.when(jnp.logical_not(is_prompt))
        def _():
            x_scr[...] = x_refs[1][...]

        x = x_scr[...]
    else:
        x = x_refs[0][...]
    h = _mod_in(x, gpre_ref[...], mod_ref, slot).astype(BF16)
    for c in range(D_FF // FF_CHUNK):
        lo = c * FF_CHUNK
        a = _dot(h, win_ref[:, lo:lo + FF_CHUNK])
        b = _dot(h, win_ref[:, D_FF + lo:D_FF + lo + FF_CHUNK])
        g_scr[:, lo:lo + FF_CHUNK] = (a * _sigmoid(a) * b).astype(BF16)
    y = _dot(g_scr[...], wout_ref[...])
    out = x + (0.5 * _gate(mod_ref, slot)) * _rms(y, gpost_ref[...])
    if split_out:
        @pl.when(is_prompt)
        def _():
            o_refs[0][...] = out

        @pl.when(jnp.logical_not(is_prompt))
        def _():
            o_refs[1][...] = out
    else:
        o_refs[0][...] = out


def _ffn(x, mod, g_pre, g_post, w_in, w_out, slot, split_out=False):
    split_in = isinstance(x, tuple)
    xs = x if split_in else (x,)
    vmem = (2 * D_MODEL * D_FF + D_FF * D_MODEL) * 2 + 4 * ROW_TILE * D_MODEL * 4 \
        + ROW_TILE * D_FF * 2 + 8 * ROW_TILE * FF_CHUNK * 4 + 6 * ROW_TILE * D_MODEL * 4
    pair_specs = [_prompt_row_spec(D_MODEL), _sample_row_spec(D_MODEL)]
    scratch = [pltpu.VMEM((ROW_TILE, D_FF), BF16)]
    if split_in:
        scratch.append(pltpu.VMEM((ROW_TILE, D_MODEL), F32))
    if split_out:
        out_specs = pair_specs
        out_shape = [jax.ShapeDtypeStruct((N_PROMPT, D_MODEL), F32),
                     jax.ShapeDtypeStruct((N_SAMPLE, D_MODEL), F32)]
    else:
        out_specs = _row_spec(D_MODEL)
        out_shape = jax.ShapeDtypeStruct((N_TOK, D_MODEL), F32)
    return pl.pallas_call(
        functools.partial(_ffn_kernel, slot=slot, split_in=split_in, split_out=split_out),
        grid=(N_TOK // ROW_TILE,),
        in_specs=(pair_specs if split_in else [_row_spec(D_MODEL)]) + [
            _mod_spec(),
            _const_spec((1, D_MODEL)),
            _const_spec((1, D_MODEL)),
            _const_spec((D_MODEL, 2 * D_FF)),
            _const_spec((D_FF, D_MODEL)),
        ],
        out_specs=out_specs,
        out_shape=out_shape,
        scratch_shapes=scratch,
        compiler_params=pltpu.CompilerParams(
            dimension_semantics=("arbitrary",), vmem_limit_bytes=_vmem_limit(vmem)),
        name="ffn_sublayer",
    )(*xs, mod, g_pre.reshape(1, D_MODEL), g_post.reshape(1, D_MODEL), w_in, w_out)


def _qkv_kernel(x_ref, mod_ref, gpre_ref, w_ref, cos_ref, sin_ref, q_ref, kp_ref, vp_ref, k_ref, v_ref):
    i = pl.program_id(0)
    scale = HEAD_DIM ** -0.5
    h = _mod_in(x_ref[...], gpre_ref[...], mod_ref, 1).astype(BF16)
    qkv = _dot(h, w_ref[...])
    nq = N_Q_HEADS * HEAD_DIM
    nk = N_KV_HEADS * HEAD_DIM

    @pl.when(i < PROMPT_TILES)
    def _():
        q_ref[...] = (qkv[:, :nq] * scale).astype(BF16)
        kp_ref[...] = qkv[:, nq:nq + nk]
        vp_ref[...] = qkv[:, nq + nk:]

    @pl.when(i >= PROMPT_TILES)
    def _():
        v_ref[...] = qkv[:, nq + nk:]
        cos = cos_ref[...]
        sin = sin_ref[...]
        lane = lax.broadcasted_iota(jnp.int32, (ROW_TILE, LANES), 1)
        first = (lane & (HEAD_DIM // 4)) == 0

        def rope(xg):
            up = pltpu.roll(xg, LANES - HEAD_DIM // 4, 1)
            down = pltpu.roll(xg, HEAD_DIM // 4, 1)
            return xg * cos + jnp.where(first, up, down) * sin

        for j in range(nq // LANES):
            q_ref[:, j * LANES:(j + 1) * LANES] = (rope(qkv[:, j * LANES:(j + 1) * LANES]) * scale).astype(BF16)
        for j in range(nk // LANES):
            k_ref[:, j * LANES:(j + 1) * LANES] = rope(qkv[:, nq + j * LANES:nq + (j + 1) * LANES])


def _rope_tables():
    t = jnp.arange(DEC_SEQ)
    row = (t // GRID_W).astype(F32)
    col = (t % GRID_W).astype(F32)
    nf = HEAD_DIM // 4
    inv = ROPE_BASE ** (-jnp.arange(nf, dtype=F32) / nf)
    ar = row[:, None] * inv[None, :]
    ac = col[:, None] * inv[None, :]
    cos = jnp.concatenate([jnp.cos(ar), jnp.cos(ar), jnp.cos(ac), jnp.cos(ac)], axis=-1)
    sin = jnp.concatenate([-jnp.sin(ar), jnp.sin(ar), -jnp.sin(ac), jnp.sin(ac)], axis=-1)
    reps = LANES // HEAD_DIM
    return jnp.tile(cos, (1, reps)), jnp.tile(sin, (1, reps))


def _qkv(x, mod, g_pre, w_qkv):
    cos, sin = _rope_tables()
    tab_spec = pl.BlockSpec(
        (ROW_TILE, LANES), lambda i: (jnp.maximum(i - PROMPT_TILES, 0) % TILES_PER_SAMPLE, 0))
    nk = N_KV_HEADS * HEAD_DIM
    vmem = D_MODEL * QKV_DIM * 2 + 6 * ROW_TILE * D_MODEL * 4 + 4 * ROW_TILE * QKV_DIM * 4
    return pl.pallas_call(
        _qkv_kernel,
        grid=(N_TOK // ROW_TILE,),
        in_specs=[
            _row_spec(D_MODEL),
            _mod_spec(),
            _const_spec((1, D_MODEL)),
            _const_spec((D_MODEL, QKV_DIM)),
            tab_spec,
            tab_spec,
        ],
        out_specs=[_row_spec(D_MODEL), _prompt_row_spec(nk), _prompt_row_spec(nk),
                   _sample_row_spec(nk), _sample_row_spec(nk)],
        out_shape=[
            jax.ShapeDtypeStruct((N_TOK, D_MODEL), BF16),
            jax.ShapeDtypeStruct((N_PROMPT, nk), F32),
            jax.ShapeDtypeStruct((N_PROMPT, nk), F32),
            jax.ShapeDtypeStruct((N_SAMPLE, nk), F32),
            jax.ShapeDtypeStruct((N_SAMPLE, nk), F32),
        ],
        compiler_params=pltpu.CompilerParams(
            dimension_semantics=("arbitrary",), vmem_limit_bytes=_vmem_limit(vmem)),
        name="attn_qkv",
    )(x, mod, g_pre.reshape(1, D_MODEL), w_qkv, cos, sin)


def _lane_halves(ref_or_val, hkv, rows):
    grp = ref_or_val[:, (hkv // 2) * LANES:(hkv // 2 + 1) * LANES]
    lane = lax.broadcasted_iota(jnp.int32, (rows, LANES), 1)
    in_low = lane < HEAD_DIM
    if hkv % 2 == 0:
        lo = jnp.where(in_low, grp, 0.0)
        hi = pltpu.roll(lo, HEAD_DIM, 1)
    else:
        hi = jnp.where(in_low, 0.0, grp)
        lo = pltpu.roll(hi, HEAD_DIM, 1)
    return lo.astype(BF16), hi.astype(BF16)


def _softmax_sink(s, sink):
    m = jnp.maximum(jnp.max(s, axis=-1, keepdims=True), sink)
    e = jnp.exp(s - m)
    denom = jnp.sum(e, axis=-1, keepdims=True) + jnp.exp(sink - m)
    return (e * (1.0 / denom)).astype(BF16)


def _attend(sink_ref, q_ref, o_ref, keys, vals, n_keys, valid):
    for hkv in range(N_KV_HEADS):
        k_lo, k_hi = _lane_halves(keys, hkv, n_keys)
        v_lo, v_hi = _lane_halves(vals, hkv, n_keys)
        for j in (2 * hkv, 2 * hkv + 1):
            q2 = q_ref[:, j * LANES:(j + 1) * LANES]
            s_a = _dot_nt(q2, k_lo)
            s_b = _dot_nt(q2, k_hi)
            if valid is not None:
                s_a = jnp.where(valid, s_a, MASK_VALUE)
                s_b = jnp.where(valid, s_b, MASK_VALUE)
            p_a = _softmax_sink(s_a, sink_ref[2 * j])
            p_b = _softmax_sink(s_b, sink_ref[2 * j + 1])
            o_ref[:, j * LANES:(j + 1) * LANES] = (_dot(p_a, v_lo) + _dot(p_b, v_hi)).astype(BF16)


def _attn_prompt_kernel(sink_ref, q_ref, k_ref, v_ref, o_ref):
    _attend(sink_ref, q_ref, o_ref, k_ref[...], v_ref[...], SEQ, None)


def _attn_prompt(sink, q, k, v):
    nk = N_KV_HEADS * HEAD_DIM
    return pl.pallas_call(
        _attn_prompt_kernel,
        grid=(BATCH,),
        in_specs=[
            pl.BlockSpec(memory_space=pltpu.SMEM),
            pl.BlockSpec((SEQ, D_MODEL), lambda b: (b, 0)),
            pl.BlockSpec((SEQ, nk), lambda b: (b, 0)),
            pl.BlockSpec((SEQ, nk), lambda b: (b, 0)),
        ],
        out_specs=pl.BlockSpec((SEQ, D_MODEL), lambda b: (b, 0)),
        out_shape=jax.ShapeDtypeStruct((N_PROMPT, D_MODEL), BF16),
        compiler_params=pltpu.CompilerParams(dimension_semantics=("parallel",)),
        name="attn_prompt",
    )(sink, q, k, v)


def _attn_sample_kernel(sink_ref, q_ref, k_ref, v_ref, ck_ref, cv_ref, o_ref):
    qb = pl.program_id(1)
    nblk = DEC_SEQ // ATTN_BLOCK
    B = ATTN_BLOCK
    starts = [
        pl.multiple_of(jnp.maximum(qb - 1, 0) * B, B),
        pl.multiple_of(qb * B, B),
        pl.multiple_of(jnp.minimum(qb + 1, nblk - 1) * B, B),
    ]
    keys = jnp.concatenate([k_ref[pl.ds(s, B), :] for s in starts] + [ck_ref[...]], axis=0)
    vals = jnp.concatenate([v_ref[pl.ds(s, B), :] for s in starts] + [cv_ref[...]], axis=0)
    n_keys = 3 * B + PAST_LEN
    r = lax.broadcasted_iota(jnp.int32, (B, n_keys), 0)
    c = lax.broadcasted_iota(jnp.int32, (B, n_keys), 1)
    prev_bad = (c < B) & ((c < r) | (qb == 0))
    next_bad = (c >= 2 * B) & (c < 3 * B) & (((c - 2 * B) > r) | (qb == nblk - 1))
    valid = jnp.logical_not(prev_bad | next_bad)
    _attend(sink_ref, q_ref, o_ref, keys, vals, n_keys, valid)


def _attn_sample(sink, q, k, v, cache_k, cache_v):
    nk = N_KV_HEADS * HEAD_DIM
    nblk = DEC_SEQ // ATTN_BLOCK
    q_off = N_PROMPT // ATTN_BLOCK
    return pl.pallas_call(
        _attn_sample_kernel,
        grid=(DEC_BATCH, nblk),
        in_specs=[
            pl.BlockSpec(memory_space=pltpu.SMEM),
            pl.BlockSpec((ATTN_BLOCK, D_MODEL), lambda b, t: (q_off + b * nblk + t, 0)),
            pl.BlockSpec((DEC_SEQ, nk), lambda b, t: (b, 0)),
            pl.BlockSpec((DEC_SEQ, nk), lambda b, t: (b, 0)),
            pl.BlockSpec((None, PAST_LEN, nk), lambda b, t: (b, 0, 0)),
            pl.BlockSpec((None, PAST_LEN, nk), lambda b, t: (b, 0, 0)),
        ],
        out_specs=pl.BlockSpec((ATTN_BLOCK, D_MODEL), lambda b, t: (b * nblk + t, 0)),
        out_shape=jax.ShapeDtypeStruct((N_SAMPLE, D_MODEL), BF16),
        compiler_params=pltpu.CompilerParams(dimension_semantics=("parallel", "parallel")),
        name="attn_sample",
    )(sink, q, k, v, cache_k, cache_v)


def _mix_out_kernel(op_ref, os_ref, x_ref, mod_ref, gpost_ref, w_ref, out_ref):
    is_prompt = pl.program_id(0) < PROMPT_TILES

    def finish(o_ref):
        y = _dot(o_ref[...], w_ref[...])
        out_ref[...] = x_ref[...] + _gate(mod_ref, 1) * _rms(y, gpost_ref[...])

    @pl.when(is_prompt)
    def _():
        finish(op_ref)

    @pl.when(jnp.logical_not(is_prompt))
    def _():
        finish(os_ref)


def _mix_out(o_prompt, o_sample, x, mod, g_post, w):
    vmem = D_MODEL * D_MODEL * 2 + 12 * ROW_TILE * D_MODEL * 4
    return pl.pallas_call(
        _mix_out_kernel,
        grid=(N_TOK // ROW_TILE,),
        in_specs=[
            _prompt_row_spec(D_MODEL),
            _sample_row_spec(D_MODEL),
            _row_spec(D_MODEL),
            _mod_spec(),
            _const_spec((1, D_MODEL)),
            _const_spec((D_MODEL, D_MODEL)),
        ],
        out_specs=_row_spec(D_MODEL),
        out_shape=jax.ShapeDtypeStruct((N_TOK, D_MODEL), F32),
        compiler_params=pltpu.CompilerParams(
            dimension_semantics=("parallel",), vmem_limit_bytes=_vmem_limit(vmem)),
        name="mixer_out",
    )(o_prompt, o_sample, x, mod, g_post.reshape(1, D_MODEL), w)


def _rec_in_kernel(x_ref, mod_ref, gpre_ref, w_ref, p_ref):
    h = _mod_in(x_ref[...], gpre_ref[...], mod_ref, 1).astype(BF16)
    for part in range(5):
        lo = part * D_MODEL
        y = _dot(h, w_ref[:, lo:lo + D_MODEL])
        if part == 0:
            y = y * _sigmoid(y) * (REC_DK ** -0.5)
        elif part == 4:
            y = y * _sigmoid(y)
        p_ref[:, lo:lo + D_MODEL] = y


def _rec_in(x, mod, g_pre, w):
    vmem = D_MODEL * REC_IN_DIM * 2 + 4 * ROW_TILE * D_MODEL * 4 + 2 * ROW_TILE * REC_IN_DIM * 4 \
        + 4 * ROW_TILE * D_MODEL * 4
    return pl.pallas_call(
        _rec_in_kernel,
        grid=(N_TOK // ROW_TILE,),
        in_specs=[
            _row_spec(D_MODEL),
            _mod_spec(),
            _const_spec((1, D_MODEL)),
            _const_spec((D_MODEL, REC_IN_DIM)),
        ],
        out_specs=_row_spec(REC_IN_DIM),
        out_shape=jax.ShapeDtypeStruct((N_TOK, REC_IN_DIM), F32),
        compiler_params=pltpu.CompilerParams(
            dimension_semantics=("parallel",), vmem_limit_bytes=_vmem_limit(vmem)),
        name="rec_in",
    )(x, mod, g_pre.reshape(1, D_MODEL), w)


def _cumsum_rows(tri, x):
    hi = x.astype(BF16)
    r1 = x - hi.astype(F32)
    mid = r1.astype(BF16)
    lo = (r1 - mid.astype(F32)).astype(BF16)
    return _dot(tri, hi) + _dot(tri, mid) + _dot(tri, lo)


def _rec_kernel(*refs, seq_len, has_init):
    if has_init:
        q_ref, v_ref, zf_ref, zb_ref, sg_ref, lbf_ref, lbb_ref, gn_ref, s0_ref, o_ref, o_scr = refs
        st_ref = None
    else:
        q_ref, v_ref, zf_ref, zb_ref, sg_ref, lbf_ref, lbb_ref, gn_ref, o_ref, st_ref, o_scr = refs
        s0_ref = None
    G = REC_GROUP
    n_groups = seq_len // G
    n_chunks = G // CHUNK
    r = lax.broadcasted_iota(jnp.int32, (G, G), 0)
    c = lax.broadcasted_iota(jnp.int32, (G, G), 1)
    same = (r // CHUNK) == (c // CHUNK)
    masks = (same & (c <= r), same & (c >= r))
    tris = tuple(jnp.where(m, 1.0, 0.0).astype(BF16) for m in masks)

    for hh in range(REC_HEADS_PER_STEP):
        cs = slice(hh * REC_DK, (hh + 1) * REC_DK)
        for d in range(2):
            z = (zf_ref, zb_ref)[d][:, cs]
            lb = (lbf_ref, lbb_ref)[d][hh]
            e = jnp.exp(-jnp.abs(z))
            rcp = 1.0 / (1.0 + e)
            pos = z >= 0
            sig = jnp.where(pos, rcp, e * rcp)
            nsig = jnp.where(pos, e * rcp, rcp)
            logf = jnp.log(lb + (1.0 - lb) * sig)
            key = (1.0 - lb) * nsig
            if has_init:
                state = s0_ref[d, hh].T
            else:
                state = jnp.zeros((REC_DV, REC_DK), F32)
            g_order = range(n_groups) if d == 0 else range(n_groups - 1, -1, -1)
            c_order = range(n_chunks) if d == 0 else range(n_chunks - 1, -1, -1)
            for g in g_order:
                rows = slice(g * G, (g + 1) * G)
                bc = _cumsum_rows(tris[d], logf[rows])
                q_g = q_ref[rows, cs]
                v_g = v_ref[rows, cs].astype(BF16)
                key_g = key[rows]
                q_dec = (q_g * jnp.exp(bc)).astype(BF16)
                k_inv = (key_g * jnp.exp(-bc)).astype(BF16)
                a = jnp.where(masks[d], _dot_nt(q_dec, k_inv), 0.0).astype(BF16)
                o_g = _dot(a, v_g)
                for ch in c_order:
                    rr = slice(ch * CHUNK, (ch + 1) * CHUNK)
                    edge = ch * CHUNK + (CHUNK - 1 if d == 0 else 0)
                    bl = bc[edge:edge + 1, :]
                    o_c = o_g[rr] + _dot_nt(q_dec[rr], state.astype(BF16))
                    k_end = (key_g[rr] * jnp.exp(bl - bc[rr])).astype(BF16)
                    state = state * jnp.exp(bl) + _dot_tn(v_g[rr], k_end)
                    out_rows = slice(g * G + ch * CHUNK, g * G + (ch + 1) * CHUNK)
                    if d == 0:
                        o_scr[out_rows, cs] = o_c
                    else:
                        o_scr[out_rows, cs] += o_c
            if st_ref is not None:
                st_ref[d, hh] = state.T
        o = o_scr[:, cs]
        o_ref[:, cs] = (_rms(o, gn_ref[hh]) * sg_ref[:, cs]).astype(BF16)


def _rec(proj, lb_f, lb_b, g_norm, s0, *, seq_len, n_seq, row_block_off):
    hb = REC_HEADS_PER_STEP
    w = hb * REC_DK
    n_hp = REC_HEADS // hb
    has_init = s0 is not None

    def col_spec(part):
        return pl.BlockSpec((seq_len, w), lambda b, h: (row_block_off + b, part * n_hp + h))

    head_vec_spec = pl.BlockSpec((hb, 1, REC_DK), lambda b, h: (h, 0, 0))
    in_specs = [col_spec(0), col_spec(1), col_spec(2), col_spec(3), col_spec(4),
                head_vec_spec, head_vec_spec, head_vec_spec]
    args = [proj, proj, proj, proj, proj,
            lb_f.reshape(REC_HEADS, 1, REC_DK), lb_b.reshape(REC_HEADS, 1, REC_DK),
            g_norm.reshape(REC_HEADS, 1, REC_DV)]
    state_spec = pl.BlockSpec((None, 2, hb, REC_DK, REC_DV), lambda b, h: (b, 0, h, 0, 0))
    o_spec = pl.BlockSpec((seq_len, w), lambda b, h: (b, h))
    o_shape = jax.ShapeDtypeStruct((n_seq * seq_len, D_MODEL), BF16)
    if has_init:
        in_specs.append(state_spec)
        args.append(s0)
        out_specs, out_shape = o_spec, o_shape
    else:
        out_specs = [o_spec, state_spec]
        out_shape = [o_shape, jax.ShapeDtypeStruct((n_seq, 2, REC_HEADS, REC_DK, REC_DV), F32)]
    return pl.pallas_call(
        functools.partial(_rec_kernel, seq_len=seq_len, has_init=has_init),
        grid=(n_seq, n_hp),
        in_specs=in_specs,
        out_specs=out_specs,
        out_shape=out_shape,
        scratch_shapes=[pltpu.VMEM((seq_len, w), F32)],
        compiler_params=pltpu.CompilerParams(
            dimension_semantics=("parallel", "parallel"),
            vmem_limit_bytes=_vmem_limit(40 * seq_len * w * 4)),
        name="rec_scan_init" if has_init else "rec_scan",
    )(*args)


def kernel(x_prompt, x_sample, c, cache_k, cache_v, state_s, c_ctx, w_ada, b_ada, norm_pre, norm_post,
           w_ffn_in, w_ffn_out, w_qkv, w_attn_out, attn_sink, w_rec_in, rec_lb_logits, rec_norm, w_rec_out):
    x = (x_prompt.reshape(N_PROMPT, D_MODEL), x_sample.reshape(N_SAMPLE, D_MODEL))
    cond =jnp.concatenate([c_ctx[None], c, jnp.zeros((SUBLANES - N_COND, D_MODEL), F32)], axis=0)
    mods = _ada_mod(cond, w_ada, b_ada)

    lb_soft = jax.nn.softmax(rec_lb_logits.astype(F32), axis=1)
    lb_all = jnp.cumsum(lb_soft, axis=1) - lb_soft[:, :1]

    nk = N_KV_HEADS * HEAD_DIM
    new_k = new_v = new_s = None
    for i in range(DEPTH):
        mod = mods[i]
        x = _ffn(x, mod, norm_pre[i, 0], norm_post[i, 0],
                 w_ffn_in[i, 0].astype(BF16), w_ffn_out[i, 0].astype(BF16), 0)
        j = i // 2
        if i % 2 == 0:
            q, k_p, v_p, k_s, v_s = _qkv(x, mod, norm_pre[i, 1], w_qkv[j].astype(BF16))
            o_p = _attn_prompt(attn_sink[j], q, k_p, v_p)
            o_s = _attn_sample(attn_sink[j], q, k_s, v_s,
                               cache_k[:, j].reshape(DEC_BATCH, PAST_LEN, nk),
                               cache_v[:, j].reshape(DEC_BATCH, PAST_LEN, nk))
            x = _mix_out(o_p, o_s, x, mod, norm_post[i, 1], w_attn_out[j].astype(BF16))
            new_k = k_p.reshape(BATCH, 1, SEQ, N_KV_HEADS, HEAD_DIM)
            new_v = v_p.reshape(BATCH, 1, SEQ, N_KV_HEADS, HEAD_DIM)
        else:
            proj = _rec_in(x, mod, norm_pre[i, 1], w_rec_in[j].astype(BF16))
            o_p, s_p = _rec(proj, lb_all[0, i], lb_all[1, i], rec_norm[j], None,
                            seq_len=SEQ, n_seq=BATCH, row_block_off=0)
            o_s = _rec(proj, lb_all[0, i], lb_all[1, i], rec_norm[j], state_s[:, j],
                       seq_len=DEC_SEQ, n_seq=DEC_BATCH, row_block_off=N_PROMPT // DEC_SEQ)
            x = _mix_out(o_p, o_s, x, mod, norm_post[i, 1], w_rec_out[j].astype(BF16))
            new_s = s_p.reshape(BATCH, 1, 2, REC_HEADS, REC_DK, REC_DV)
        x = _ffn(x, mod, norm_pre[i, 2], norm_post[i, 2],
                 w_ffn_in[i, 1].astype(BF16), w_ffn_out[i, 1].astype(BF16), 2,
                 split_out=(i == DEPTH - 1))
    y_prompt, y_sample = x
    return (y_prompt.reshape(BATCH, SEQ, D_MODEL), y_sample.reshape(DEC_BATCH, DEC_SEQ, D_MODEL),
            new_k, new_v, new_s)
```

```python
import functools

import jax
import jax.numpy as jnp
from jax import lax
from jax.experimental import pallas as pl
from jax.experimental.pallas import tpu as pltpu

F32 = jnp.float32
BF16 = jnp.bfloat16

D_MODEL = 1024
BATCH = 32
SEQ = 256
DEPTH = 2
DEC_BATCH = 2
DEC_SEQ = 1024
PAST_LEN = 256
GRID_W = 64
HEAD_DIM = 64
N_Q_HEADS = 16
N_KV_HEADS = 4
QKV_DIM = (N_Q_HEADS + 2 * N_KV_HEADS) * HEAD_DIM
ATTN_BLOCK = 128
ROPE_BASE = 10000.0
REC_HEADS = 8
REC_DK = 128
REC_DV = 128
REC_IN_DIM = 5 * D_MODEL
CHUNK = 64
D_FF = 2816
EPS = 1e-6
MASK_VALUE = -1e30

N_PROMPT = BATCH * SEQ
N_SAMPLE = DEC_BATCH * DEC_SEQ
N_TOK = N_PROMPT + N_SAMPLE
N_COND = 1 + DEC_BATCH

LANES = 128
SUBLANES = 8
VMEM_BYTES_V7X = 64 * 1024 * 1024

ROW_TILE = 512
FF_CHUNK = 256
ADA_TILE = 1024
REC_GROUP = 256
REC_HEADS_PER_STEP = 4

PROMPT_TILES = N_PROMPT // ROW_TILE
TILES_PER_SAMPLE = DEC_SEQ // ROW_TILE


def _vmem_limit(nbytes):
    return int(min(VMEM_BYTES_V7X - 8 * 1024 * 1024, max(nbytes, 16 * 1024 * 1024)))


def _sigmoid(x):
    return 1.0 / (1.0 + jnp.exp(-x))


def _rms(x, g):
    ms = jnp.mean(x * x, axis=-1, keepdims=True)
    return x * lax.rsqrt(ms + EPS) * g


def _mod_in(x, g_pre, mod_ref, slot):
    shift = mod_ref[slot * 3:slot * 3 + 1, :]
    scale = mod_ref[slot * 3 + 1:slot * 3 + 2, :]
    return _rms(x, g_pre) * (1.0 + scale) + shift


def _gate(mod_ref, slot):
    return mod_ref[slot * 3 + 2:slot * 3 + 3, :]


def _dot(a, b):
    return jnp.dot(a, b, preferred_element_type=F32)


def _dot_nt(a, b):
    return lax.dot_general(a, b, (((1,), (1,)), ((), ())), preferred_element_type=F32)


def _dot_tn(a, b):
    return lax.dot_general(a, b, (((0,), (0,)), ((), ())), preferred_element_type=F32)


def _tile_group(i):
    return jnp.where(i < PROMPT_TILES, 0, 1 + (i - PROMPT_TILES) // TILES_PER_SAMPLE)


def _row_spec(width):
    return pl.BlockSpec((ROW_TILE, width), lambda i: (i, 0))


def _prompt_row_spec(width):
    return pl.BlockSpec((ROW_TILE, width), lambda i: (jnp.minimum(i, PROMPT_TILES - 1), 0))


def _sample_row_spec(width):
    return pl.BlockSpec((ROW_TILE, width), lambda i: (jnp.maximum(i - PROMPT_TILES, 0), 0))


def _mod_spec():
    return pl.BlockSpec((None, 9, D_MODEL), lambda i: (_tile_group(i), 0, 0))


def _const_spec(shape, lead=()):
    nd = len(shape)
    return pl.BlockSpec((None,) * len(lead) + tuple(shape), lambda *_: tuple(lead) + (0,) * nd,
                        pipeline_mode=pl.Buffered(1))


def _ada_kernel(cond_ref, w_ref, b_ref, o_ref):
    c = cond_ref[...]
    s = (c * _sigmoid(c)).astype(BF16)
    o_ref[...] = _dot(s, w_ref[...].astype(BF16)) + b_ref[...]


def _ada_mod(cond, w_ada, b_ada):
    n_out = 9 * D_MODEL
    out = pl.pallas_call(
        _ada_kernel,
        grid=(DEPTH, n_out // ADA_TILE),
        in_specs=[
            pl.BlockSpec((SUBLANES, D_MODEL), lambda l, j: (0, 0)),
            pl.BlockSpec((None, D_MODEL, ADA_TILE), lambda l, j: (l, 0, j)),
            pl.BlockSpec((None, 1, ADA_TILE), lambda l, j: (l, 0, j)),
        ],
        out_specs=pl.BlockSpec((None, SUBLANES, ADA_TILE), lambda l, j: (l, 0, j)),
        out_shape=jax.ShapeDtypeStruct((DEPTH, SUBLANES, n_out), F32),
        compiler_params=pltpu.CompilerParams(
            dimension_semantics=("parallel", "parallel"),
            vmem_limit_bytes=_vmem_limit(4 * D_MODEL * ADA_TILE * 4)),
        name="ada_mod",
    )(cond, w_ada, b_ada.reshape(DEPTH, 1, n_out))
    return out[:, :N_COND].reshape(DEPTH, N_COND, 9, D_MODEL)


def _ffn_kernel(*refs, slot, split_in, split_out):
    refs = list(refs)
    x_refs = [refs.pop(0) for _ in range(2 if split_in else 1)]
    mod_ref, gpre_ref, gpost_ref, win_ref, wout_ref = (refs.pop(0) for _ in range(5))
    o_refs = [refs.pop(0) for _ in range(2 if split_out else 1)]
    g_scr = refs.pop(0)
    is_prompt = pl.program_id(0) < PROMPT_TILES
    if split_in:
        x_scr = refs.pop(0)

        @pl.when(is_prompt)
        def _():
            x_scr[...] = x_refs[0][...]

        @pl.when(jnp.logical_not(is_prompt))
        def _():
            x_scr[...] = x_refs[1][...]

        x = x_scr[...]
    else:
        x = x_refs[0][...]
    h = _mod_in(x, gpre_ref[...], mod_ref, slot).astype(BF16)
    for c in range(D_FF // FF_CHUNK):
        lo = c * FF_CHUNK
        a = _dot(h, win_ref[:, lo:lo + FF_CHUNK])
        b = _dot(h, win_ref[:, D_FF + lo:D_FF + lo + FF_CHUNK])
        g_scr[:, lo:lo + FF_CHUNK] = (a * _sigmoid(a) * b).astype(BF16)
    y = _dot(g_scr[...], wout_ref[...])
    out = x + (0.5 * _gate(mod_ref, slot)) * _rms(y, gpost_ref[...])
    if split_out:
        @pl.when(is_prompt)
        def _():
            o_refs[0][...] = out

        @pl.when(jnp.logical_not(is_prompt))
        def _():
            o_refs[1][...] = out
    else:
        o_refs[0][...] = out


def _ffn(x, mod, g_pre, g_post, w_in, w_out, lead, slot, split_out=False):
    split_in = isinstance(x, tuple)
    xs = x if split_in else (x,)
    vmem = (2 * D_MODEL * D_FF + D_FF * D_MODEL) * 2 + 4 * ROW_TILE * D_MODEL * 4 \
        + ROW_TILE * D_FF * 2 + 8 * ROW_TILE * FF_CHUNK * 4 + 6 * ROW_TILE * D_MODEL * 4
    pair_specs = [_prompt_row_spec(D_MODEL), _sample_row_spec(D_MODEL)]
    scratch = [pltpu.VMEM((ROW_TILE, D_FF), BF16)]
    if split_in:
        scratch.append(pltpu.VMEM((ROW_TILE, D_MODEL), F32))
    if split_out:
        out_specs = pair_specs
        out_shape = [jax.ShapeDtypeStruct((N_PROMPT, D_MODEL), F32),
                     jax.ShapeDtypeStruct((N_SAMPLE, D_MODEL), F32)]
    else:
        out_specs = _row_spec(D_MODEL)
        out_shape = jax.ShapeDtypeStruct((N_TOK, D_MODEL), F32)
    return pl.pallas_call(
        functools.partial(_ffn_kernel, slot=slot, split_in=split_in, split_out=split_out),
        grid=(N_TOK // ROW_TILE,),
        in_specs=(pair_specs if split_in else [_row_spec(D_MODEL)]) + [
            _mod_spec(),
            _const_spec((1, D_MODEL)),
            _const_spec((1, D_MODEL)),
            _const_spec((D_MODEL, 2 * D_FF), lead),
            _const_spec((D_FF, D_MODEL), lead),
        ],
        out_specs=out_specs,
        out_shape=out_shape,
        scratch_shapes=scratch,
        compiler_params=pltpu.CompilerParams(
            dimension_semantics=("arbitrary",), vmem_limit_bytes=_vmem_limit(vmem)),
        name="ffn_sublayer",
    )(*xs, mod, g_pre.reshape(1, D_MODEL), g_post.reshape(1, D_MODEL), w_in, w_out)


def _qkv_kernel(x_ref, mod_ref, gpre_ref, w_ref, cos_ref, sin_ref, q_ref, kp_ref, vp_ref, k_ref, v_ref):
    i = pl.program_id(0)
    scale = HEAD_DIM ** -0.5
    h = _mod_in(x_ref[...], gpre_ref[...], mod_ref, 1).astype(BF16)
    qkv = _dot(h, w_ref[...])
    nq = N_Q_HEADS * HEAD_DIM
    nk = N_KV_HEADS * HEAD_DIM

    @pl.when(i < PROMPT_TILES)
    def _():
        q_ref[...] = (qkv[:, :nq] * scale).astype(BF16)
        kp_ref[...] = qkv[:, nq:nq + nk]
        vp_ref[...] = qkv[:, nq + nk:]

    @pl.when(i >= PROMPT_TILES)
    def _():
        v_ref[...] = qkv[:, nq + nk:]
        cos = cos_ref[...]
        sin = sin_ref[...]
        lane = lax.broadcasted_iota(jnp.int32, (ROW_TILE, LANES), 1)
        first = (lane & (HEAD_DIM // 4)) == 0

        def rope(xg):
            up = pltpu.roll(xg, LANES - HEAD_DIM // 4, 1)
            down = pltpu.roll(xg, HEAD_DIM // 4, 1)
            return xg * cos + jnp.where(first, up, down) * sin

        for j in range(nq // LANES):
            q_ref[:, j * LANES:(j + 1) * LANES] = (rope(qkv[:, j * LANES:(j + 1) * LANES]) * scale).astype(BF16)
        for j in range(nk // LANES):
            k_ref[:, j * LANES:(j + 1) * LANES] = rope(qkv[:, nq + j * LANES:nq + (j + 1) * LANES])


def _rope_tables():
    t = jnp.arange(DEC_SEQ)
    row = (t // GRID_W).astype(F32)
    col = (t % GRID_W).astype(F32)
    nf = HEAD_DIM // 4
    inv = ROPE_BASE ** (-jnp.arange(nf, dtype=F32) / nf)
    ar = row[:, None] * inv[None, :]
    ac = col[:, None] * inv[None, :]
    cos = jnp.concatenate([jnp.cos(ar), jnp.cos(ar), jnp.cos(ac), jnp.cos(ac)], axis=-1)
    sin = jnp.concatenate([-jnp.sin(ar), jnp.sin(ar), -jnp.sin(ac), jnp.sin(ac)], axis=-1)
    reps = LANES // HEAD_DIM
    return jnp.tile(cos, (1, reps)), jnp.tile(sin, (1, reps))


def _qkv(x, mod, g_pre, w_qkv, lead):
    cos, sin = _rope_tables()
    tab_spec = pl.BlockSpec(
        (ROW_TILE, LANES), lambda i: (jnp.maximum(i - PROMPT_TILES, 0) % TILES_PER_SAMPLE, 0))
    nk = N_KV_HEADS * HEAD_DIM
    vmem = D_MODEL * QKV_DIM * 2 + 6 * ROW_TILE * D_MODEL * 4 + 4 * ROW_TILE * QKV_DIM * 4
    return pl.pallas_call(
        _qkv_kernel,
        grid=(N_TOK // ROW_TILE,),
        in_specs=[
            _row_spec(D_MODEL),
            _mod_spec(),
            _const_spec((1, D_MODEL)),
            _const_spec((D_MODEL, QKV_DIM), lead),
            tab_spec,
            tab_spec,
        ],
        out_specs=[_row_spec(D_MODEL), _prompt_row_spec(nk), _prompt_row_spec(nk),
                   _sample_row_spec(nk), _sample_row_spec(nk)],
        out_shape=[
            jax.ShapeDtypeStruct((N_TOK, D_MODEL), BF16),
            jax.ShapeDtypeStruct((N_PROMPT, nk), F32),
            jax.ShapeDtypeStruct((N_PROMPT, nk), F32),
            jax.ShapeDtypeStruct((N_SAMPLE, nk), F32),
            jax.ShapeDtypeStruct((N_SAMPLE, nk), F32),
        ],
        compiler_params=pltpu.CompilerParams(
            dimension_semantics=("arbitrary",), vmem_limit_bytes=_vmem_limit(vmem)),
        name="attn_qkv",
    )(x, mod, g_pre.reshape(1, D_MODEL), w_qkv, cos, sin)


def _lane_halves(ref_or_val, hkv, rows):
    grp = ref_or_val[:, (hkv // 2) * LANES:(hkv // 2 + 1) * LANES]
    lane = lax.broadcasted_iota(jnp.int32, (rows, LANES), 1)
    in_low = lane < HEAD_DIM
    if hkv % 2 == 0:
        lo = jnp.where(in_low, grp, 0.0)
        hi = pltpu.roll(lo, HEAD_DIM, 1)
    else:
        hi = jnp.where(in_low, 0.0, grp)
        lo = pltpu.roll(hi, HEAD_DIM, 1)
    return lo.astype(BF16), hi.astype(BF16)


def _softmax_sink(s, sink):
    m = jnp.maximum(jnp.max(s, axis=-1, keepdims=True), sink)
    e = jnp.exp(s - m)
    denom = jnp.sum(e, axis=-1, keepdims=True) + jnp.exp(sink - m)
    return (e * (1.0 / denom)).astype(BF16)


def _attend(sink_ref, q_ref, o_ref, keys, vals, n_keys, valid):
    for hkv in range(N_KV_HEADS):
        k_lo, k_hi = _lane_halves(keys, hkv, n_keys)
        v_lo, v_hi = _lane_halves(vals, hkv, n_keys)
        for j in (2 * hkv, 2 * hkv + 1):
            q2 = q_ref[:, j * LANES:(j + 1) * LANES]
            s_a = _dot_nt(q2, k_lo)
            s_b = _dot_nt(q2, k_hi)
            if valid is not None:
                s_a = jnp.where(valid, s_a, MASK_VALUE)
                s_b = jnp.where(valid, s_b, MASK_VALUE)
            p_a = _softmax_sink(s_a, sink_ref[2 * j])
            p_b = _softmax_sink(s_b, sink_ref[2 * j + 1])
            o_ref[:, j * LANES:(j + 1) * LANES] = (_dot(p_a, v_lo) + _dot(p_b, v_hi)).astype(BF16)


def _attn_prompt_kernel(sink_ref, q_ref, k_ref, v_ref, o_ref):
    _attend(sink_ref, q_ref, o_ref, k_ref[...], v_ref[...], SEQ, None)


def _attn_prompt(sink, q, k, v):
    nk = N_KV_HEADS * HEAD_DIM
    return pl.pallas_call(
        _attn_prompt_kernel,
        grid=(BATCH,),
        in_specs=[
            pl.BlockSpec(memory_space=pltpu.SMEM),
            pl.BlockSpec((SEQ, D_MODEL), lambda b: (b, 0)),
            pl.BlockSpec((SEQ, nk), lambda b: (b, 0)),
            pl.BlockSpec((SEQ, nk), lambda b: (b, 0)),
        ],
        out_specs=pl.BlockSpec((SEQ, D_MODEL), lambda b: (b, 0)),
        out_shape=jax.ShapeDtypeStruct((N_PROMPT, D_MODEL), BF16),
        compiler_params=pltpu.CompilerParams(dimension_semantics=("parallel",)),
        name="attn_prompt",
    )(sink, q, k, v)


def _attn_sample_kernel(sink_ref, q_ref, k_ref, v_ref, ck_ref, cv_ref, o_ref):
    qb = pl.program_id(1)
    nblk = DEC_SEQ // ATTN_BLOCK
    B = ATTN_BLOCK
    starts = [
        pl.multiple_of(jnp.maximum(qb - 1, 0) * B, B),
        pl.multiple_of(qb * B, B),
        pl.multiple_of(jnp.minimum(qb + 1, nblk - 1) * B, B),
    ]
    keys = jnp.concatenate([k_ref[pl.ds(s, B), :] for s in starts] + [ck_ref[...]], axis=0)
    vals = jnp.concatenate([v_ref[pl.ds(s, B), :] for s in starts] + [cv_ref[...]], axis=0)
    n_keys = 3 * B + PAST_LEN
    r = lax.broadcasted_iota(jnp.int32, (B, n_keys), 0)
    c = lax.broadcasted_iota(jnp.int32, (B, n_keys), 1)
    prev_bad = (c < B) & ((c < r) | (qb == 0))
    next_bad = (c >= 2 * B) & (c < 3 * B) & (((c - 2 * B) > r) | (qb == nblk - 1))
    valid = jnp.logical_not(prev_bad | next_bad)
    _attend(sink_ref, q_ref, o_ref, keys, vals, n_keys, valid)


def _attn_sample(sink, q, k, v, cache_k, cache_v):
    nk = N_KV_HEADS * HEAD_DIM
    nblk = DEC_SEQ // ATTN_BLOCK
    q_off = N_PROMPT // ATTN_BLOCK
    return pl.pallas_call(
        _attn_sample_kernel,
        grid=(DEC_BATCH, nblk),
        in_specs=[
            pl.BlockSpec(memory_space=pltpu.SMEM),
            pl.BlockSpec((ATTN_BLOCK, D_MODEL), lambda b, t: (q_off + b * nblk + t, 0)),
            pl.BlockSpec((DEC_SEQ, nk), lambda b, t: (b, 0)),
            pl.BlockSpec((DEC_SEQ, nk), lambda b, t: (b, 0)),
            pl.BlockSpec((None, PAST_LEN, nk), lambda b, t: (b, 0, 0)),
            pl.BlockSpec((None, PAST_LEN, nk), lambda b, t: (b, 0, 0)),
        ],
        out_specs=pl.BlockSpec((ATTN_BLOCK, D_MODEL), lambda b, t: (b * nblk + t, 0)),
        out_shape=jax.ShapeDtypeStruct((N_SAMPLE, D_MODEL), BF16),
        compiler_params=pltpu.CompilerParams(dimension_semantics=("parallel", "parallel")),
        name="attn_sample",
    )(sink, q, k, v, cache_k, cache_v)


def _mix_out_kernel(op_ref, os_ref, x_ref, mod_ref, gpost_ref, w_ref, out_ref):
    is_prompt = pl.program_id(0) < PROMPT_TILES

    def finish(o_ref):
        y = _dot(o_ref[...], w_ref[...])
        out_ref[...] = x_ref[...] + _gate(mod_ref, 1) * _rms(y, gpost_ref[...])

    @pl.when(is_prompt)
    def _():
        finish(op_ref)

    @pl.when(jnp.logical_not(is_prompt))
    def _():
        finish(os_ref)


def _mix_out(o_prompt, o_sample, x, mod, g_post, w, lead):
    vmem = D_MODEL * D_MODEL * 2 + 12 * ROW_TILE * D_MODEL * 4
    return pl.pallas_call(
        _mix_out_kernel,
        grid=(N_TOK // ROW_TILE,),
        in_specs=[
            _prompt_row_spec(D_MODEL),
            _sample_row_spec(D_MODEL),
            _row_spec(D_MODEL),
            _mod_spec(),
            _const_spec((1, D_MODEL)),
            _const_spec((D_MODEL, D_MODEL), lead),
        ],
        out_specs=_row_spec(D_MODEL),
        out_shape=jax.ShapeDtypeStruct((N_TOK, D_MODEL), F32),
        compiler_params=pltpu.CompilerParams(
            dimension_semantics=("parallel",), vmem_limit_bytes=_vmem_limit(vmem)),
        name="mixer_out",
    )(o_prompt, o_sample, x, mod, g_post.reshape(1, D_MODEL), w)


def _cumsum_rows(tri, x):
    hi = x.astype(BF16)
    r1 = x - hi.astype(F32)
    mid = r1.astype(BF16)
    lo = (r1 - mid.astype(F32)).astype(BF16)
    return _dot(tri, hi) + _dot(tri, mid) + _dot(tri, lo)


def _chunk_masks():
    r = lax.broadcasted_iota(jnp.int32, (REC_GROUP, REC_GROUP), 0)
    c = lax.broadcasted_iota(jnp.int32, (REC_GROUP, REC_GROUP), 1)
    same = (r // CHUNK) == (c // CHUNK)
    return same & (c <= r), same & (c >= r)


def _rec_in_kernel(x_ref, mod_ref, gpre_ref, w_ref, lbf_ref, lbb_ref,
                   qdf_ref, kif_ref, kef_ref, qdb_ref, kib_ref, keb_ref, v_ref, sg_ref, decf_ref, decb_ref):
    G = REC_GROUP
    h = _mod_in(x_ref[...], gpre_ref[...], mod_ref, 1).astype(BF16)
    y = _dot(h, w_ref[:, 0:D_MODEL])
    qf = y * _sigmoid(y) * (REC_DK ** -0.5)
    v_ref[...] = _dot(h, w_ref[:, D_MODEL:2 * D_MODEL]).astype(BF16)
    y = _dot(h, w_ref[:, 4 * D_MODEL:5 * D_MODEL])
    sg_ref[...] = y * _sigmoid(y)
    tris = tuple(jnp.where(m, 1.0, 0.0).astype(BF16) for m in _chunk_masks())
    outs = ((qdf_ref, kif_ref, kef_ref, decf_ref, lbf_ref), (qdb_ref, kib_ref, keb_ref, decb_ref, lbb_ref))
    for d in range(2):
        qd_ref, ki_ref, ke_ref, dec_ref, lb_ref = outs[d]
        z = _dot(h, w_ref[:, (2 + d) * D_MODEL:(3 + d) * D_MODEL])
        lb = lb_ref[...]
        e = jnp.exp(-jnp.abs(z))
        rcp = 1.0 / (1.0 + e)
        pos = z >= 0
        sig = jnp.where(pos, rcp, e * rcp)
        nsig = jnp.where(pos, e * rcp, rcp)
        logf = jnp.log(lb + (1.0 - lb) * sig)
        key = (1.0 - lb) * nsig
        for g in range(ROW_TILE // G):
            rows = slice(g * G, (g + 1) * G)
            bc = _cumsum_rows(tris[d], logf[rows])
            key_g = key[rows]
            qd_ref[rows, :] = (qf[rows] * jnp.exp(bc)).astype(BF16)
            ki_ref[rows, :] = (key_g * jnp.exp(-bc)).astype(BF16)
            for ch in range(G // CHUNK):
                rr = slice(ch * CHUNK, (ch + 1) * CHUNK)
                edge = ch * CHUNK + (CHUNK - 1 if d == 0 else 0)
                bl = bc[edge:edge + 1, :]
                ke_ref[g * G + ch * CHUNK:g * G + (ch + 1) * CHUNK, :] = \
                    (key_g[rr] * jnp.exp(bl - bc[rr])).astype(BF16)
                dec_ref[g, ch:ch + 1, :] = jnp.exp(bl)


def _rec_in(x, mod, g_pre, w, lead, lb_f, lb_b):
    n_chunks = REC_GROUP // CHUNK
    groups_per_tile = ROW_TILE // REC_GROUP
    vmem = D_MODEL * REC_IN_DIM * 2 + 4 * ROW_TILE * D_MODEL * 4 + 2 * ROW_TILE * D_MODEL * (7 * 2 + 4) \
        + 12 * ROW_TILE * D_MODEL * 4
    act = jax.ShapeDtypeStruct((N_TOK, D_MODEL), BF16)
    dec = jax.ShapeDtypeStruct((N_TOK // REC_GROUP, n_chunks, D_MODEL), F32)
    dec_spec = pl.BlockSpec((groups_per_tile, n_chunks, D_MODEL), lambda i: (i, 0, 0))
    return pl.pallas_call(
        _rec_in_kernel,
        grid=(N_TOK // ROW_TILE,),
        in_specs=[
            _row_spec(D_MODEL),
            _mod_spec(),
            _const_spec((1, D_MODEL)),
            _const_spec((D_MODEL, REC_IN_DIM), lead),
            _const_spec((1, D_MODEL)),
            _const_spec((1, D_MODEL)),
        ],
        out_specs=[_row_spec(D_MODEL) for _ in range(8)] + [dec_spec, dec_spec],
        out_shape=[act] * 7 + [jax.ShapeDtypeStruct((N_TOK, D_MODEL), F32), dec, dec],
        compiler_params=pltpu.CompilerParams(
            dimension_semantics=("parallel",), vmem_limit_bytes=_vmem_limit(vmem)),
        name="rec_in",
    )(x, mod, g_pre.reshape(1, D_MODEL), w, lb_f.reshape(1, D_MODEL), lb_b.reshape(1, D_MODEL))


def _rec_kernel(*refs, seq_len, has_init):
    refs = list(refs)
    dir_refs = (refs[0:3], refs[3:6])
    v_ref, sg_ref, decf_ref, decb_ref, gn_ref = refs[6:11]
    dec_refs = (decf_ref, decb_ref)
    if has_init:
        s0_ref, o_ref, o_scr = refs[11:]
        st_ref = None
    else:
        o_ref, st_ref, o_scr = refs[11:]
        s0_ref = None
    G = REC_GROUP
    n_groups = seq_len // G
    n_chunks = G // CHUNK
    masks = _chunk_masks()

    for hh in range(REC_HEADS_PER_STEP):
        cs = slice(hh * REC_DK, (hh + 1) * REC_DK)
        for d in range(2):
            qd_ref, ki_ref, ke_ref = dir_refs[d]
            if has_init:
                state = s0_ref[d, hh].T
            else:
                state = jnp.zeros((REC_DV, REC_DK), F32)
            g_order = range(n_groups) if d == 0 else range(n_groups - 1, -1, -1)
            c_order = range(n_chunks) if d == 0 else range(n_chunks - 1, -1, -1)
            for g in g_order:
                rows = slice(g * G, (g + 1) * G)
                q_dec = qd_ref[rows, cs]
                v_g = v_ref[rows, cs]
                a = jnp.where(masks[d], _dot_nt(q_dec, ki_ref[rows, cs]), 0.0).astype(BF16)
                o_g = _dot(a, v_g)
                for ch in c_order:
                    rr = slice(ch * CHUNK, (ch + 1) * CHUNK)
                    out_rows = slice(g * G + ch * CHUNK, g * G + (ch + 1) * CHUNK)
                    o_c = o_g[rr] + _dot_nt(q_dec[rr], state.astype(BF16))
                    state = state * dec_refs[d][g, ch:ch + 1, cs] + _dot_tn(v_g[rr], ke_ref[out_rows, cs])
                    if d == 0:
                        o_scr[out_rows, cs] = o_c
                    else:
                        o_scr[out_rows, cs] += o_c
            if st_ref is not None:
                st_ref[d, hh] = state.T
        o = o_scr[:, cs]
        o_ref[:, cs] = (_rms(o, gn_ref[hh]) * sg_ref[:, cs]).astype(BF16)


def _rec(rec_acts, g_norm, s0, *, seq_len, n_seq, row_block_off):
    hb = REC_HEADS_PER_STEP
    w = hb * REC_DK
    n_hp = REC_HEADS // hb
    n_groups = seq_len // REC_GROUP
    has_init = s0 is not None

    def act_spec():
        return pl.BlockSpec((seq_len, w), lambda b, h: (row_block_off + b, h))

    def dec_spec():
        return pl.BlockSpec((n_groups, REC_GROUP // CHUNK, w), lambda b, h: (row_block_off + b, 0, h))

    in_specs = [act_spec() for _ in range(8)] + [dec_spec(), dec_spec(),
                                                 pl.BlockSpec((hb, 1, REC_DV), lambda b, h: (h, 0, 0))]
    args = list(rec_acts) + [g_norm.reshape(REC_HEADS, 1, REC_DV)]
    state_spec = pl.BlockSpec((None, 2, hb, REC_DK, REC_DV), lambda b, h: (b, 0, h, 0, 0))
    o_spec = pl.BlockSpec((seq_len, w), lambda b, h: (b, h))
    o_shape = jax.ShapeDtypeStruct((n_seq * seq_len, D_MODEL), BF16)
    if has_init:
        in_specs.append(state_spec)
        args.append(s0)
        out_specs, out_shape = o_spec, o_shape
    else:
        out_specs = [o_spec, state_spec]
        out_shape = [o_shape, jax.ShapeDtypeStruct((n_seq, 2, REC_HEADS, REC_DK, REC_DV), F32)]
    return pl.pallas_call(
        functools.partial(_rec_kernel, seq_len=seq_len, has_init=has_init),
        grid=(n_seq, n_hp),
        in_specs=in_specs,
        out_specs=out_specs,
        out_shape=out_shape,
        scratch_shapes=[pltpu.VMEM((seq_len, w), F32)],
        compiler_params=pltpu.CompilerParams(
            dimension_semantics=("parallel", "parallel"),
            vmem_limit_bytes=_vmem_limit(40 * seq_len * w * 4)),
        name="rec_scan_init" if has_init else "rec_scan",
    )(*args)


def kernel(x_prompt, x_sample, c, cache_k, cache_v, state_s, c_ctx, w_ada, b_ada, norm_pre, norm_post,
           w_ffn_in, w_ffn_out, w_qkv, w_attn_out, attn_sink, w_rec_in, rec_lb_logits, rec_norm, w_rec_out):
    x = (x_prompt.reshape(N_PROMPT, D_MODEL), x_sample.reshape(N_SAMPLE, D_MODEL))
    cond = jnp.concatenate([c_ctx[None], c, jnp.zeros((SUBLANES - N_COND, D_MODEL), F32)], axis=0)
    mods = _ada_mod(cond, w_ada, b_ada)

    lb_soft = jax.nn.softmax(rec_lb_logits.astype(F32), axis=1)
    lb_all = jnp.cumsum(lb_soft, axis=1) - lb_soft[:, :1]

    nk = N_KV_HEADS * HEAD_DIM
    w_ffn_in_b = w_ffn_in.astype(BF16)
    w_ffn_out_b = w_ffn_out.astype(BF16)
    w_qkv_b = w_qkv.astype(BF16)
    w_attn_out_b = w_attn_out.astype(BF16)
    w_rec_in_b = w_rec_in.astype(BF16)
    w_rec_out_b = w_rec_out.astype(BF16)
    new_k = new_v = new_s = None
    for i in range(DEPTH):
        mod = mods[i]
        x = _ffn(x, mod, norm_pre[i, 0], norm_post[i, 0], w_ffn_in_b, w_ffn_out_b, (i, 0), 0)
        j = i // 2
        if i % 2 == 0:
            q, k_p, v_p, k_s, v_s = _qkv(x, mod, norm_pre[i, 1], w_qkv_b, (j,))
            o_p = _attn_prompt(attn_sink[j], q, k_p, v_p)
            o_s = _attn_sample(attn_sink[j], q, k_s, v_s,
                               cache_k[:, j].reshape(DEC_BATCH, PAST_LEN, nk),
                               cache_v[:, j].reshape(DEC_BATCH, PAST_LEN, nk))
            x = _mix_out(o_p, o_s, x, mod, norm_post[i, 1], w_attn_out_b, (j,))
            new_k = k_p.reshape(BATCH, 1, SEQ, N_KV_HEADS, HEAD_DIM)
            new_v = v_p.reshape(BATCH, 1, SEQ, N_KV_HEADS, HEAD_DIM)
        else:
            acts = _rec_in(x, mod, norm_pre[i, 1], w_rec_in_b, (j,), lb_all[0, i], lb_all[1, i])
            o_p, s_p = _rec(acts, rec_norm[j], None, seq_len=SEQ, n_seq=BATCH, row_block_off=0)
            o_s = _rec(acts, rec_norm[j], state_s[:, j],
                       seq_len=DEC_SEQ, n_seq=DEC_BATCH, row_block_off=N_PROMPT // DEC_SEQ)
            x = _mix_out(o_p, o_s, x, mod, norm_post[i, 1], w_rec_out_b, (j,))
            new_s = s_p.reshape(BATCH, 1, 2, REC_HEADS, REC_DK, REC_DV)
        x = _ffn(x, mod, norm_pre[i, 2], norm_post[i, 2], w_ffn_in_b, w_ffn_out_b, (i, 1), 2,
                 split_out=(i == DEPTH - 1))
    y_prompt, y_sample = x
    return (y_prompt.reshape(BATCH, SEQ, D_MODEL), y_sample.reshape(DEC_BATCH, DEC_SEQ, D_MODEL),
            new_k, new_v, new_s)
```

```python
import functools

import jax
import jax.numpy as jnp
from jax import lax
from jax.experimental import pallas as pl
from jax.experimental.pallas import tpu as pltpu

F32 = jnp.float32
BF16 = jnp.bfloat16

D_MODEL = 1024
BATCH = 32
SEQ = 256
DEPTH = 2
DEC_BATCH = 2
DEC_SEQ = 1024
PAST_LEN = 256
GRID_W = 64
HEAD_DIM = 64
N_Q_HEADS = 16
N_KV_HEADS = 4
QKV_DIM = (N_Q_HEADS + 2 * N_KV_HEADS) * HEAD_DIM
ATTN_BLOCK = 128
ROPE_BASE = 10000.0
REC_HEADS = 8
REC_DK = 128
REC_DV = 128
REC_IN_DIM = 5 * D_MODEL
CHUNK = 64
D_FF = 2816
EPS = 1e-6
MASK_VALUE = -1e30

N_PROMPT = BATCH * SEQ
N_SAMPLE = DEC_BATCH * DEC_SEQ
N_TOK = N_PROMPT + N_SAMPLE
N_COND = 1 + DEC_BATCH

LANES = 128
SUBLANES = 8
VMEM_BYTES_V7X = 64 * 1024 * 1024

ROW_TILE = 512
FF_CHUNK = 256
ADA_TILE = 1024
REC_GROUP = 256
REC_HEADS_PER_STEP = 4

PROMPT_TILES = N_PROMPT // ROW_TILE
TILES_PER_SAMPLE = DEC_SEQ // ROW_TILE


def _vmem_limit(nbytes):
    return int(min(VMEM_BYTES_V7X - 8 * 1024 * 1024, max(nbytes, 16 * 1024 * 1024)))


def _sigmoid(x):
    return 1.0 / (1.0 + jnp.exp(-x))


def _rms(x, g):
    ms = jnp.mean(x * x, axis=-1, keepdims=True)
    return x * lax.rsqrt(ms + EPS) * g


def _mod_in(x, g_pre, mod_ref, slot):
    shift = mod_ref[slot * 3:slot * 3 + 1, :]
    scale = mod_ref[slot * 3 + 1:slot * 3 + 2, :]
    return _rms(x, g_pre) * (1.0 + scale) + shift


def _gate(mod_ref, slot):
    return mod_ref[slot * 3 + 2:slot * 3 + 3, :]


def _dot(a, b):
    return jnp.dot(a, b, preferred_element_type=F32)


def _dot_nt(a, b):
    return lax.dot_general(a, b, (((1,), (1,)), ((), ())), preferred_element_type=F32)


def _dot_tn(a, b):
    return lax.dot_general(a, b, (((0,), (0,)), ((), ())), preferred_element_type=F32)


def _tile_group(i):
    return jnp.where(i < PROMPT_TILES, 0, 1 + (i - PROMPT_TILES) // TILES_PER_SAMPLE)


def _row_spec(width):
    return pl.BlockSpec((ROW_TILE, width), lambda i: (i, 0))


def _prompt_row_spec(width):
    return pl.BlockSpec((ROW_TILE, width), lambda i: (jnp.minimum(i, PROMPT_TILES - 1), 0))


def _sample_row_spec(width):
    return pl.BlockSpec((ROW_TILE, width), lambda i: (jnp.maximum(i - PROMPT_TILES, 0), 0))


def _mod_spec():
    return pl.BlockSpec((None, 9, D_MODEL), lambda i: (_tile_group(i), 0, 0))


def _const_spec(shape, lead=()):
    nd = len(shape)
    return pl.BlockSpec((None,) * len(lead) + tuple(shape), lambda *_: tuple(lead) + (0,) * nd,
                        pipeline_mode=pl.Buffered(1))


def _ada_kernel(cond_ref, w_ref, b_ref, o_ref):
    c = cond_ref[...]
    s = (c * _sigmoid(c)).astype(BF16)
    o_ref[...] = _dot(s, w_ref[...].astype(BF16)) + b_ref[...]


def _ada_mod(cond, w_ada, b_ada):
    n_out = 9 * D_MODEL
    out = pl.pallas_call(
        _ada_kernel,
        grid=(DEPTH, n_out // ADA_TILE),
        in_specs=[
            pl.BlockSpec((SUBLANES, D_MODEL), lambda l, j: (0, 0)),
            pl.BlockSpec((None, D_MODEL, ADA_TILE), lambda l, j: (l, 0, j)),
            pl.BlockSpec((None, 1, ADA_TILE), lambda l, j: (l, 0, j)),
        ],
        out_specs=pl.BlockSpec((None, SUBLANES, ADA_TILE), lambda l, j: (l, 0, j)),
        out_shape=jax.ShapeDtypeStruct((DEPTH, SUBLANES, n_out), F32),
        compiler_params=pltpu.CompilerParams(
            dimension_semantics=("parallel", "parallel"),
            vmem_limit_bytes=_vmem_limit(4 * D_MODEL * ADA_TILE * 4)),
        name="ada_mod",
    )(cond, w_ada, b_ada.reshape(DEPTH, 1, n_out))
    return out[:, :N_COND].reshape(DEPTH, N_COND, 9, D_MODEL)


def _ffn_kernel(*refs, slot, split_in, split_out):
    refs = list(refs)
    x_refs = [refs.pop(0) for _ in range(2 if split_in else 1)]
    mod_ref, gpre_ref, gpost_ref, win_ref, wout_ref = (refs.pop(0) for _ in range(5))
    o_refs = [refs.pop(0) for _ in range(2 if split_out else 1)]
    g_scr = refs.pop(0)
    is_prompt = pl.program_id(0) < PROMPT_TILES
    if split_in:
        x_scr = refs.pop(0)

        @pl.when(is_prompt)
        def _():
            x_scr[...] = x_refs[0][...]

        @pl.when(jnp.logical_not(is_prompt))
        def _():
            x_scr[...] = x_refs[1][...]

        x = x_scr[...]
    else:
        x = x_refs[0][...]
    h = _mod_in(x, gpre_ref[...], mod_ref, slot).astype(BF16)
    for c in range(D_FF // FF_CHUNK):
        lo = c * FF_CHUNK
        a = _dot(h, win_ref[:, lo:lo + FF_CHUNK])
        b = _dot(h, win_ref[:, D_FF + lo:D_FF + lo + FF_CHUNK])
        g_scr[:, lo:lo + FF_CHUNK] = (a * _sigmoid(a) * b).astype(BF16)
    y = _dot(g_scr[...], wout_ref[...])
    out = x + (0.5 * _gate(mod_ref, slot)) * _rms(y, gpost_ref[...])
    if split_out:
        @pl.when(is_prompt)
        def _():
            o_refs[0][...] = out

        @pl.when(jnp.logical_not(is_prompt))
        def _():
            o_refs[1][...] = out
    else:
        o_refs[0][...] = out


def _ffn(x, mod, g_pre, g_post, w_in, w_out, lead, slot, split_out=False):
    split_in = isinstance(x, tuple)
    xs = x if split_in else (x,)
    vmem = (2 * D_MODEL * D_FF + D_FF * D_MODEL) * 2 + 4 * ROW_TILE * D_MODEL * 4 \
        + ROW_TILE * D_FF * 2 + 8 * ROW_TILE * FF_CHUNK * 4 + 6 * ROW_TILE * D_MODEL * 4
    pair_specs = [_prompt_row_spec(D_MODEL), _sample_row_spec(D_MODEL)]
    scratch = [pltpu.VMEM((ROW_TILE, D_FF), BF16)]
    if split_in:
        scratch.append(pltpu.VMEM((ROW_TILE, D_MODEL), F32))
    if split_out:
        out_specs = pair_specs
        out_shape = [jax.ShapeDtypeStruct((N_PROMPT, D_MODEL), F32),
                     jax.ShapeDtypeStruct((N_SAMPLE, D_MODEL), F32)]
    else:
        out_specs = _row_spec(D_MODEL)
        out_shape = jax.ShapeDtypeStruct((N_TOK, D_MODEL), F32)
    return pl.pallas_call(
        functools.partial(_ffn_kernel, slot=slot, split_in=split_in, split_out=split_out),
        grid=(N_TOK // ROW_TILE,),
        in_specs=(pair_specs if split_in else [_row_spec(D_MODEL)]) + [
            _mod_spec(),
            _const_spec((1, D_MODEL)),
            _const_spec((1, D_MODEL)),
            _const_spec((D_MODEL, 2 * D_FF), lead),
            _const_spec((D_FF, D_MODEL), lead),
        ],
        out_specs=out_specs,
        out_shape=out_shape,
        scratch_shapes=scratch,
        compiler_params=pltpu.CompilerParams(
            dimension_semantics=("arbitrary",), vmem_limit_bytes=_vmem_limit(vmem)),
        name="ffn_sublayer",
    )(*xs, mod, g_pre.reshape(1, D_MODEL), g_post.reshape(1, D_MODEL), w_in, w_out)


def _qkv_kernel(x_ref, mod_ref, gpre_ref, w_ref, cos_ref, sin_ref, q_ref, kp_ref, vp_ref, k_ref, v_ref):
    i = pl.program_id(0)
    scale = HEAD_DIM ** -0.5
    h = _mod_in(x_ref[...], gpre_ref[...], mod_ref, 1).astype(BF16)
    qkv = _dot(h, w_ref[...])
    nq = N_Q_HEADS * HEAD_DIM
    nk = N_KV_HEADS * HEAD_DIM

    @pl.when(i < PROMPT_TILES)
    def _():
        q_ref[...] = (qkv[:, :nq] * scale).astype(BF16)
        kp_ref[...] = qkv[:, nq:nq + nk]
        vp_ref[...] = qkv[:, nq + nk:]

    @pl.when(i >= PROMPT_TILES)
    def _():
        v_ref[...] = qkv[:, nq + nk:]
        cos = cos_ref[...]
        sin = sin_ref[...]
        lane = lax.broadcasted_iota(jnp.int32, (ROW_TILE, LANES), 1)
        first = (lane & (HEAD_DIM // 4)) == 0

        def rope(xg):
            up = pltpu.roll(xg, LANES - HEAD_DIM // 4, 1)
            down = pltpu.roll(xg, HEAD_DIM // 4, 1)
            return xg * cos + jnp.where(first, up, down) * sin

        for j in range(nq // LANES):
            q_ref[:, j * LANES:(j + 1) * LANES] = (rope(qkv[:, j * LANES:(j + 1) * LANES]) * scale).astype(BF16)
        for j in range(nk // LANES):
            k_ref[:, j * LANES:(j + 1) * LANES] = rope(qkv[:, nq + j * LANES:nq + (j + 1) * LANES])


def _rope_tables():
    t = jnp.arange(DEC_SEQ)
    row = (t // GRID_W).astype(F32)
    col = (t % GRID_W).astype(F32)
    nf = HEAD_DIM // 4
    inv = ROPE_BASE ** (-jnp.arange(nf, dtype=F32) / nf)
    ar = row[:, None] * inv[None, :]
    ac = col[:, None] * inv[None, :]
    cos = jnp.concatenate([jnp.cos(ar), jnp.cos(ar), jnp.cos(ac), jnp.cos(ac)], axis=-1)
    sin = jnp.concatenate([-jnp.sin(ar), jnp.sin(ar), -jnp.sin(ac), jnp.sin(ac)], axis=-1)
    reps = LANES // HEAD_DIM
    return jnp.tile(cos, (1, reps)), jnp.tile(sin, (1, reps))


def _qkv(x, mod, g_pre, w_qkv, lead):
    cos, sin = _rope_tables()
    tab_spec = pl.BlockSpec(
        (ROW_TILE, LANES), lambda i: (jnp.maximum(i - PROMPT_TILES, 0) % TILES_PER_SAMPLE, 0))
    nk = N_KV_HEADS * HEAD_DIM
    vmem = D_MODEL * QKV_DIM * 2 + 6 * ROW_TILE * D_MODEL * 4 + 4 * ROW_TILE * QKV_DIM * 4
    return pl.pallas_call(
        _qkv_kernel,
        grid=(N_TOK // ROW_TILE,),
        in_specs=[
            _row_spec(D_MODEL),
            _mod_spec(),
            _const_spec((1, D_MODEL)),
            _const_spec((D_MODEL, QKV_DIM), lead),
            tab_spec,
            tab_spec,
        ],
        out_specs=[_row_spec(D_MODEL), _prompt_row_spec(nk), _prompt_row_spec(nk),
                   _sample_row_spec(nk), _sample_row_spec(nk)],
        out_shape=[
            jax.ShapeDtypeStruct((N_TOK, D_MODEL), BF16),
            jax.ShapeDtypeStruct((N_PROMPT, nk), F32),
            jax.ShapeDtypeStruct((N_PROMPT, nk), F32),
            jax.ShapeDtypeStruct((N_SAMPLE, nk), F32),
            jax.ShapeDtypeStruct((N_SAMPLE, nk), F32),
        ],
        compiler_params=pltpu.CompilerParams(
            dimension_semantics=("arbitrary",), vmem_limit_bytes=_vmem_limit(vmem)),
        name="attn_qkv",
    )(x, mod, g_pre.reshape(1, D_MODEL), w_qkv, cos, sin)


def _lane_halves(ref_or_val, hkv, rows):
    grp = ref_or_val[:, (hkv // 2) * LANES:(hkv // 2 + 1) * LANES]
    lane = lax.broadcasted_iota(jnp.int32, (rows, LANES), 1)
    in_low = lane < HEAD_DIM
    if hkv % 2 == 0:
        lo = jnp.where(in_low, grp, 0.0)
        hi = pltpu.roll(lo, HEAD_DIM, 1)
    else:
        hi = jnp.where(in_low, 0.0, grp)
        lo = pltpu.roll(hi, HEAD_DIM, 1)
    return lo.astype(BF16), hi.astype(BF16)


def _softmax_sink(s, sink):
    m = jnp.maximum(jnp.max(s, axis=-1, keepdims=True), sink)
    e = jnp.exp(s - m)
    denom = jnp.sum(e, axis=-1, keepdims=True) + jnp.exp(sink - m)
    return (e * (1.0 / denom)).astype(BF16)


def _attend(sink_ref, q_ref, o_ref, keys, vals, n_keys, valid):
    for hkv in range(N_KV_HEADS):
        k_lo, k_hi = _lane_halves(keys, hkv, n_keys)
        v_lo, v_hi = _lane_halves(vals, hkv, n_keys)
        for j in (2 * hkv, 2 * hkv + 1):
            q2 = q_ref[:, j * LANES:(j + 1) * LANES]
            s_a = _dot_nt(q2, k_lo)
            s_b = _dot_nt(q2, k_hi)
            if valid is not None:
                s_a = jnp.where(valid, s_a, MASK_VALUE)
                s_b = jnp.where(valid, s_b, MASK_VALUE)
            p_a = _softmax_sink(s_a, sink_ref[2 * j])
            p_b = _softmax_sink(s_b, sink_ref[2 * j + 1])
            o_ref[:, j * LANES:(j + 1) * LANES] = (_dot(p_a, v_lo) + _dot(p_b, v_hi)).astype(BF16)


def _attn_prompt_kernel(sink_ref, q_ref, k_ref, v_ref, o_ref):
    _attend(sink_ref, q_ref, o_ref, k_ref[...], v_ref[...], SEQ, None)


def _attn_prompt(sink, q, k, v):
    nk = N_KV_HEADS * HEAD_DIM
    return pl.pallas_call(
        _attn_prompt_kernel,
        grid=(BATCH,),
        in_specs=[
            pl.BlockSpec(memory_space=pltpu.SMEM),
            pl.BlockSpec((SEQ, D_MODEL), lambda b: (b, 0)),
            pl.BlockSpec((SEQ, nk), lambda b: (b, 0)),
            pl.BlockSpec((SEQ, nk), lambda b: (b, 0)),
        ],
        out_specs=pl.BlockSpec((SEQ, D_MODEL), lambda b: (b, 0)),
        out_shape=jax.ShapeDtypeStruct((N_PROMPT, D_MODEL), BF16),
        compiler_params=pltpu.CompilerParams(dimension_semantics=("parallel",)),
        name="attn_prompt",
    )(sink, q, k, v)


def _attn_sample_kernel(sink_ref, q_ref, k_ref, v_ref, ck_ref, cv_ref, o_ref):
    qb = pl.program_id(1)
    nblk = DEC_SEQ // ATTN_BLOCK
    B = ATTN_BLOCK
    starts = [
        pl.multiple_of(jnp.maximum(qb - 1, 0) * B, B),
        pl.multiple_of(qb * B, B),
        pl.multiple_of(jnp.minimum(qb + 1, nblk - 1) * B, B),
    ]
    keys = jnp.concatenate([k_ref[pl.ds(s, B), :] for s in starts] + [ck_ref[...]], axis=0)
    vals = jnp.concatenate([v_ref[pl.ds(s, B), :] for s in starts] + [cv_ref[...]], axis=0)
    n_keys = 3 * B + PAST_LEN
    r = lax.broadcasted_iota(jnp.int32, (B, n_keys), 0)
    c = lax.broadcasted_iota(jnp.int32, (B, n_keys), 1)
    prev_bad = (c < B) & ((c < r) | (qb == 0))
    next_bad = (c >= 2 * B) & (c < 3 * B) & (((c - 2 * B) > r) | (qb == nblk - 1))
    valid = jnp.logical_not(prev_bad | next_bad)
    _attend(sink_ref, q_ref, o_ref, keys, vals, n_keys, valid)


def _attn_sample(sink, q, k, v, cache_k, cache_v):
    nk = N_KV_HEADS * HEAD_DIM
    nblk = DEC_SEQ // ATTN_BLOCK
    q_off = N_PROMPT // ATTN_BLOCK
    return pl.pallas_call(
        _attn_sample_kernel,
        grid=(DEC_BATCH, nblk),
        in_specs=[
            pl.BlockSpec(memory_space=pltpu.SMEM),
            pl.BlockSpec((ATTN_BLOCK, D_MODEL), lambda b, t: (q_off + b * nblk + t, 0)),
            pl.BlockSpec((DEC_SEQ, nk), lambda b, t: (b, 0)),
            pl.BlockSpec((DEC_SEQ, nk), lambda b, t: (b, 0)),
            pl.BlockSpec((None, PAST_LEN, nk), lambda b, t: (b, 0, 0)),
            pl.BlockSpec((None, PAST_LEN, nk), lambda b, t: (b, 0, 0)),
        ],
        out_specs=pl.BlockSpec((ATTN_BLOCK, D_MODEL), lambda b, t: (b * nblk + t, 0)),
        out_shape=jax.ShapeDtypeStruct((N_SAMPLE, D_MODEL), BF16),
        compiler_params=pltpu.CompilerParams(dimension_semantics=("parallel", "parallel")),
        name="attn_sample",
    )(sink, q, k, v, cache_k, cache_v)


def _mix_out_kernel(op_ref, os_ref, x_ref, mod_ref, gpost_ref, w_ref, out_ref):
    is_prompt = pl.program_id(0) < PROMPT_TILES

    def finish(o_ref):
        y = _dot(o_ref[...], w_ref[...])
        out_ref[...] = x_ref[...] + _gate(mod_ref, 1) * _rms(y, gpost_ref[...])

    @pl.when(is_prompt)
    def _():
        finish(op_ref)

    @pl.when(jnp.logical_not(is_prompt))
    def _():
        finish(os_ref)


def _mix_out(o_prompt, o_sample, x, mod, g_post, w, lead):
    vmem = D_MODEL * D_MODEL * 2 + 12 * ROW_TILE * D_MODEL * 4
    return pl.pallas_call(
        _mix_out_kernel,
        grid=(N_TOK // ROW_TILE,),
        in_specs=[
            _prompt_row_spec(D_MODEL),
            _sample_row_spec(D_MODEL),
            _row_spec(D_MODEL),
            _mod_spec(),
            _const_spec((1, D_MODEL)),
            _const_spec((D_MODEL, D_MODEL), lead),
        ],
        out_specs=_row_spec(D_MODEL),
        out_shape=jax.ShapeDtypeStruct((N_TOK, D_MODEL), F32),
        compiler_params=pltpu.CompilerParams(
            dimension_semantics=("parallel",), vmem_limit_bytes=_vmem_limit(vmem)),
        name="mixer_out",
    )(o_prompt, o_sample, x, mod, g_post.reshape(1, D_MODEL), w)


def _cumsum_rows(tri, x):
    hi = x.astype(BF16)
    r1 = x - hi.astype(F32)
    mid = r1.astype(BF16)
    lo = (r1 - mid.astype(F32)).astype(BF16)
    return _dot(tri, hi) + _dot(tri, mid) + _dot(tri, lo)


def _chunk_masks():
    r = lax.broadcasted_iota(jnp.int32, (REC_GROUP, REC_GROUP), 0)
    c = lax.broadcasted_iota(jnp.int32, (REC_GROUP, REC_GROUP), 1)
    same = (r // CHUNK) == (c // CHUNK)
    return same & (c <= r), same & (c >= r)


def _rec_in_kernel(x_ref, mod_ref, gpre_ref, w_ref, lbf_ref, lbb_ref,
                   qdf_ref, kif_ref, kef_ref, qdb_ref, kib_ref, keb_ref, v_ref, sg_ref, decf_ref, decb_ref):
    G = REC_GROUP
    h = _mod_in(x_ref[...], gpre_ref[...], mod_ref, 1).astype(BF16)
    y = _dot(h, w_ref[:, 0:D_MODEL])
    qf = y * _sigmoid(y) * (REC_DK ** -0.5)
    v_ref[...] = _dot(h, w_ref[:, D_MODEL:2 * D_MODEL]).astype(BF16)
    y = _dot(h, w_ref[:, 4 * D_MODEL:5 * D_MODEL])
    sg_ref[...] = y * _sigmoid(y)
    tris = tuple(jnp.where(m, 1.0, 0.0).astype(BF16) for m in _chunk_masks())
    outs = ((qdf_ref, kif_ref, kef_ref, decf_ref, lbf_ref), (qdb_ref, kib_ref, keb_ref, decb_ref, lbb_ref))
    for d in range(2):
        qd_ref, ki_ref, ke_ref, dec_ref, lb_ref = outs[d]
        z = _dot(h, w_ref[:, (2 + d) * D_MODEL:(3 + d) * D_MODEL])
        lb = lb_ref[...]
        e = jnp.exp(-jnp.abs(z))
        rcp = 1.0 / (1.0 + e)
        pos = z >= 0
        sig = jnp.where(pos, rcp, e * rcp)
        nsig = jnp.where(pos, e * rcp, rcp)
        logf = jnp.log(lb + (1.0 - lb) * sig)
        key = (1.0 - lb) * nsig
        for g in range(ROW_TILE // G):
            rows = slice(g * G, (g + 1) * G)
            bc = _cumsum_rows(tris[d], logf[rows])
            key_g = key[rows]
            qd_ref[rows, :] = (qf[rows] * jnp.exp(bc)).astype(BF16)
            ki_ref[rows, :] = (key_g * jnp.exp(-bc)).astype(BF16)
            for ch in range(G // CHUNK):
                rr = slice(ch * CHUNK, (ch + 1) * CHUNK)
                edge = ch * CHUNK + (CHUNK - 1 if d == 0 else 0)
                bl = bc[edge:edge + 1, :]
                ke_ref[g * G + ch * CHUNK:g * G + (ch + 1) * CHUNK, :] = \
                    (key_g[rr] * jnp.exp(bl - bc[rr])).astype(BF16)
                dec_ref[g, ch:ch + 1, :] = jnp.exp(bl)


def _rec_in(x, mod, g_pre, w, lead, lb_f, lb_b):
    n_chunks = REC_GROUP // CHUNK
    groups_per_tile = ROW_TILE // REC_GROUP
    vmem = D_MODEL * REC_IN_DIM * 2 + 4 * ROW_TILE * D_MODEL * 4 + 2 * ROW_TILE * D_MODEL * (7 * 2 + 4) \
        + 12 * ROW_TILE * D_MODEL * 4
    act = jax.ShapeDtypeStruct((N_TOK, D_MODEL), BF16)
    dec = jax.ShapeDtypeStruct((N_TOK // REC_GROUP, n_chunks, D_MODEL), F32)
    dec_spec = pl.BlockSpec((groups_per_tile, n_chunks, D_MODEL), lambda i: (i, 0, 0))
    return pl.pallas_call(
        _rec_in_kernel,
        grid=(N_TOK // ROW_TILE,),
        in_specs=[
            _row_spec(D_MODEL),
            _mod_spec(),
            _const_spec((1, D_MODEL)),
            _const_spec((D_MODEL, REC_IN_DIM), lead),
            _const_spec((1, D_MODEL)),
            _const_spec((1, D_MODEL)),
        ],
        out_specs=[_row_spec(D_MODEL) for _ in range(8)] + [dec_spec, dec_spec],
        out_shape=[act] * 7 + [jax.ShapeDtypeStruct((N_TOK, D_MODEL), F32), dec, dec],
        compiler_params=pltpu.CompilerParams(
            dimension_semantics=("parallel",), vmem_limit_bytes=_vmem_limit(vmem)),
        name="rec_in",
    )(x, mod, g_pre.reshape(1, D_MODEL), w, lb_f.reshape(1, D_MODEL), lb_b.reshape(1, D_MODEL))


def _rec_kernel(*refs, seq_len, has_init):
    qdf_ref, kif_ref, kef_ref, qdb_ref, kib_ref, keb_ref, v_ref, sg_ref, decf_ref, decb_ref, gn_ref = refs[:11]
    if has_init:
        s0_ref, o_ref = refs[11:]
        st_ref = None
    else:
        o_ref, st_ref = refs[11:]
        s0_ref = None
    G = REC_GROUP
    per_group = G // CHUNK
    n_chunks = seq_len // CHUNK
    mask_f, mask_b = _chunk_masks()

    def rows_of(c):
        return slice(c * CHUNK, (c + 1) * CHUNK)

    for hh in range(REC_HEADS_PER_STEP):
        cs = slice(hh * REC_DK, (hh + 1) * REC_DK)
        upd = []
        for c in range(n_chunks):
            k_end = jnp.concatenate([kef_ref[rows_of(c), cs], keb_ref[rows_of(c), cs]], axis=1)
            upd.append(_dot_tn(v_ref[rows_of(c), cs], k_end))
        if has_init:
            s_f, s_b = s0_ref[0, hh].T, s0_ref[1, hh].T
        else:
            s_f = s_b = jnp.zeros((REC_DV, REC_DK), F32)
        enter = [None] * n_chunks
        for c in range(n_chunks):
            enter[c] = s_f.astype(BF16)
            s_f = s_f * decf_ref[c // per_group, c % per_group:c % per_group + 1, cs] + upd[c][:, :REC_DK]
        for c in range(n_chunks - 1, -1, -1):
            enter[c] = jnp.concatenate([enter[c], s_b.astype(BF16)], axis=1)
            s_b = s_b * decb_ref[c // per_group, c % per_group:c % per_group + 1, cs] + upd[c][:, REC_DK:]
        if st_ref is not None:
            st_ref[0, hh] = s_f.T
            st_ref[1, hh] = s_b.T
        for g in range(seq_len // G):
            rows = slice(g * G, (g + 1) * G)
            qf_g = qdf_ref[rows, cs]
            qb_g = qdb_ref[rows, cs]
            a = jnp.where(mask_f, _dot_nt(qf_g, kif_ref[rows, cs]), 0.0) \
                + jnp.where(mask_b, _dot_nt(qb_g, kib_ref[rows, cs]), 0.0)
            o_g = _dot(a.astype(BF16), v_ref[rows, cs])
            for ch in range(per_group):
                c = g * per_group + ch
                q_cat = jnp.concatenate([qf_g[rows_of(ch)], qb_g[rows_of(ch)]], axis=1)
                o_c = o_g[rows_of(ch)] + _dot_nt(q_cat, enter[c])
                o_ref[rows_of(c), cs] = (_rms(o_c, gn_ref[hh]) * sg_ref[rows_of(c), cs]).astype(BF16)


def _rec(rec_acts, g_norm, s0, *, seq_len, n_seq, row_block_off):
    hb = REC_HEADS_PER_STEP
    w = hb * REC_DK
    n_hp = REC_HEADS // hb
    n_groups = seq_len // REC_GROUP
    has_init = s0 is not None

    def act_spec():
        return pl.BlockSpec((seq_len, w), lambda b, h: (row_block_off + b, h))

    def dec_spec():
        return pl.BlockSpec((n_groups, REC_GROUP // CHUNK, w), lambda b, h: (row_block_off + b, 0, h))

    in_specs = [act_spec() for _ in range(8)] + [dec_spec(), dec_spec(),
                                                 pl.BlockSpec((hb, 1, REC_DV), lambda b, h: (h, 0, 0))]
    args = list(rec_acts) + [g_norm.reshape(REC_HEADS, 1, REC_DV)]
    state_spec = pl.BlockSpec((None, 2, hb, REC_DK, REC_DV), lambda b, h: (b, 0, h, 0, 0))
    o_spec = pl.BlockSpec((seq_len, w), lambda b, h: (b, h))
    o_shape = jax.ShapeDtypeStruct((n_seq * seq_len, D_MODEL), BF16)
    if has_init:
        in_specs.append(state_spec)
        args.append(s0)
        out_specs, out_shape = o_spec, o_shape
    else:
        out_specs = [o_spec, state_spec]
        out_shape = [o_shape, jax.ShapeDtypeStruct((n_seq, 2, REC_HEADS, REC_DK, REC_DV), F32)]
    return pl.pallas_call(
        functools.partial(_rec_kernel, seq_len=seq_len, has_init=has_init),
        grid=(n_seq, n_hp),
        in_specs=in_specs,
        out_specs=out_specs,
        out_shape=out_shape,
        compiler_params=pltpu.CompilerParams(
            dimension_semantics=("parallel", "parallel"),
            vmem_limit_bytes=_vmem_limit(40 * seq_len * w * 4)),
        name="rec_scan_init" if has_init else "rec_scan",
    )(*args)


def kernel(x_prompt, x_sample, c, cache_k, cache_v, state_s, c_ctx, w_ada, b_ada, norm_pre, norm_post,
           w_ffn_in, w_ffn_out, w_qkv, w_attn_out, attn_sink, w_rec_in, rec_lb_logits, rec_norm, w_rec_out):
    x = (x_prompt.reshape(N_PROMPT, D_MODEL), x_sample.reshape(N_SAMPLE, D_MODEL))
    cond = jnp.concatenate([c_ctx[None], c, jnp.zeros((SUBLANES - N_COND, D_MODEL), F32)], axis=0)
    mods = _ada_mod(cond, w_ada, b_ada)

    lb_soft = jax.nn.softmax(rec_lb_logits.astype(F32), axis=1)
    lb_all = jnp.cumsum(lb_soft, axis=1) - lb_soft[:, :1]

    nk = N_KV_HEADS * HEAD_DIM
    w_ffn_in_b = w_ffn_in.astype(BF16)
    w_ffn_out_b = w_ffn_out.astype(BF16)
    w_qkv_b = w_qkv.astype(BF16)
    w_attn_out_b = w_attn_out.astype(BF16)
    w_rec_in_b = w_rec_in.astype(BF16)
    w_rec_out_b = w_rec_out.astype(BF16)
    new_k = new_v = new_s = None
    for i in range(DEPTH):
        mod = mods[i]
        x = _ffn(x, mod, norm_pre[i, 0], norm_post[i, 0], w_ffn_in_b, w_ffn_out_b, (i, 0), 0)
        j = i // 2
        if i % 2 == 0:
            q, k_p, v_p, k_s, v_s = _qkv(x, mod, norm_pre[i, 1], w_qkv_b, (j,))
            o_p = _attn_prompt(attn_sink[j], q, k_p, v_p)
            o_s = _attn_sample(attn_sink[j], q, k_s, v_s,
                               cache_k[:, j].reshape(DEC_BATCH, PAST_LEN, nk),
                               cache_v[:, j].reshape(DEC_BATCH, PAST_LEN, nk))
            x = _mix_out(o_p, o_s, x, mod, norm_post[i, 1], w_attn_out_b, (j,))
            new_k = k_p.reshape(BATCH, 1, SEQ, N_KV_HEADS, HEAD_DIM)
            new_v = v_p.reshape(BATCH, 1, SEQ, N_KV_HEADS, HEAD_DIM)
        else:
            acts = _rec_in(x, mod, norm_pre[i, 1], w_rec_in_b, (j,), lb_all[0, i], lb_all[1, i])
            o_p, s_p = _rec(acts, rec_norm[j], None, seq_len=SEQ, n_seq=BATCH, row_block_off=0)
            o_s = _rec(acts, rec_norm[j], state_s[:, j],
                       seq_len=DEC_SEQ, n_seq=DEC_BATCH, row_block_off=N_PROMPT // DEC_SEQ)
            x = _mix_out(o_p, o_s, x, mod, norm_post[i, 1], w_rec_out_b, (j,))
            new_s = s_p.reshape(BATCH, 1, 2, REC_HEADS, REC_DK, REC_DV)
        x = _ffn(x, mod, norm_pre[i, 2], norm_post[i, 2], w_ffn_in_b, w_ffn_out_b, (i, 1), 2,
                 split_out=(i == DEPTH - 1))
    y_prompt, y_sample = x
    return (y_prompt.reshape(BATCH, SEQ, D_MODEL), y_sample.reshape(DEC_BATCH, DEC_SEQ, D_MODEL),
            new_k, new_v, new_s)
```

```python
import functools

import jax
import jax.numpy as jnp
from jax import lax
from jax.experimental import pallas as pl
from jax.experimental.pallas import tpu as pltpu

F32 = jnp.float32
BF16 = jnp.bfloat16

D_MODEL = 1024
BATCH = 32
SEQ = 256
DEPTH = 2
DEC_BATCH = 2
DEC_SEQ = 1024
PAST_LEN = 256
GRID_W = 64
HEAD_DIM = 64
N_Q_HEADS = 16
N_KV_HEADS = 4
QKV_DIM = (N_Q_HEADS + 2 * N_KV_HEADS) * HEAD_DIM
ATTN_BLOCK = 128
ROPE_BASE = 10000.0
REC_HEADS = 8
REC_DK = 128
REC_DV = 128
REC_IN_DIM = 5 * D_MODEL
CHUNK = 64
D_FF = 2816
EPS = 1e-6
MASK_VALUE = -1e30

N_PROMPT = BATCH * SEQ
N_SAMPLE = DEC_BATCH * DEC_SEQ
N_TOK = N_PROMPT + N_SAMPLE
N_COND = 1 + DEC_BATCH

LANES = 128
SUBLANES = 8
VMEM_BYTES_V7X = 64 * 1024 * 1024

ROW_TILE = 512
FF_CHUNK = 256
ADA_TILE = 1024
REC_GROUP = 256
REC_HEADS_PER_STEP = 4
REC_IN_COLS = 256

PROMPT_TILES = N_PROMPT // ROW_TILE
TILES_PER_SAMPLE = DEC_SEQ // ROW_TILE


def _vmem_limit(nbytes):
    return int(min(VMEM_BYTES_V7X - 8 * 1024 * 1024, max(nbytes, 16 * 1024 * 1024)))


def _sigmoid(x):
    return 1.0 / (1.0 + jnp.exp(-x))


def _rms(x, g):
    ms = jnp.mean(x * x, axis=-1, keepdims=True)
    return x * lax.rsqrt(ms + EPS) * g


def _mod_in(x, g_pre, mod_ref, slot):
    shift = mod_ref[slot * 3:slot * 3 + 1, :]
    scale = mod_ref[slot * 3 + 1:slot * 3 + 2, :]
    return _rms(x, g_pre) * (1.0 + scale) + shift


def _gate(mod_ref, slot):
    return mod_ref[slot * 3 + 2:slot * 3 + 3, :]


def _dot(a, b):
    return jnp.dot(a, b, preferred_element_type=F32)


def _dot_nt(a, b):
    return lax.dot_general(a, b, (((1,), (1,)), ((), ())), preferred_element_type=F32)


def _dot_tn(a, b):
    return lax.dot_general(a, b, (((0,), (0,)), ((), ())), preferred_element_type=F32)


def _tile_group(i):
    return jnp.where(i < PROMPT_TILES, 0, 1 + (i - PROMPT_TILES) // TILES_PER_SAMPLE)


def _row_spec(width):
    return pl.BlockSpec((ROW_TILE, width), lambda i: (i, 0))


def _prompt_row_spec(width):
    return pl.BlockSpec((ROW_TILE, width), lambda i: (jnp.minimum(i, PROMPT_TILES - 1), 0))


def _sample_row_spec(width):
    return pl.BlockSpec((ROW_TILE, width), lambda i: (jnp.maximum(i - PROMPT_TILES, 0), 0))


def _mod_spec():
    return pl.BlockSpec((None, 9, D_MODEL), lambda i: (_tile_group(i), 0, 0))


def _const_spec(shape, lead=()):
    nd = len(shape)
    return pl.BlockSpec((None,) * len(lead) + tuple(shape), lambda *_: tuple(lead) + (0,) * nd,
                        pipeline_mode=pl.Buffered(1))


def _ada_kernel(cond_ref, w_ref, b_ref, o_ref):
    c = cond_ref[...]
    s = (c * _sigmoid(c)).astype(BF16)
    o_ref[...] = _dot(s, w_ref[...].astype(BF16)) + b_ref[...]


def _ada_mod(cond, w_ada, b_ada):
    n_out = 9 * D_MODEL
    out = pl.pallas_call(
        _ada_kernel,
        grid=(DEPTH, n_out // ADA_TILE),
        in_specs=[
            pl.BlockSpec((SUBLANES, D_MODEL), lambda l, j: (0, 0)),
            pl.BlockSpec((None, D_MODEL, ADA_TILE), lambda l, j: (l, 0, j)),
            pl.BlockSpec((None, 1, ADA_TILE), lambda l, j: (l, 0, j)),
        ],
        out_specs=pl.BlockSpec((None, SUBLANES, ADA_TILE), lambda l, j: (l, 0, j)),
        out_shape=jax.ShapeDtypeStruct((DEPTH, SUBLANES, n_out), F32),
        compiler_params=pltpu.CompilerParams(
            dimension_semantics=("parallel", "parallel"),
            vmem_limit_bytes=_vmem_limit(4 * D_MODEL * ADA_TILE * 4)),
        name="ada_mod",
    )(cond, w_ada, b_ada.reshape(DEPTH, 1, n_out))
    return out[:, :N_COND].reshape(DEPTH, N_COND, 9, D_MODEL)


def _ffn_kernel(*refs, slot, split_in, split_out):
    refs = list(refs)
    x_refs = [refs.pop(0) for _ in range(2 if split_in else 1)]
    mod_ref, gpre_ref, gpost_ref, win_ref, wout_ref = (refs.pop(0) for _ in range(5))
    o_refs = [refs.pop(0) for _ in range(2 if split_out else 1)]
    g_scr = refs.pop(0)
    is_prompt = pl.program_id(0) < PROMPT_TILES
    if split_in:
        x_scr = refs.pop(0)

        @pl.when(is_prompt)
        def _():
            x_scr[...] = x_refs[0][...]

        @pl.when(jnp.logical_not(is_prompt))
        def _():
            x_scr[...] = x_refs[1][...]

        x = x_scr[...]
    else:
        x = x_refs[0][...]
    h = _mod_in(x, gpre_ref[...], mod_ref, slot).astype(BF16)
    for c in range(D_FF // FF_CHUNK):
        lo = c * FF_CHUNK
        a = _dot(h, win_ref[:, lo:lo + FF_CHUNK])
        b = _dot(h, win_ref[:, D_FF + lo:D_FF + lo + FF_CHUNK])
        g_scr[:, lo:lo + FF_CHUNK] = (a * _sigmoid(a) * b).astype(BF16)
    y = _dot(g_scr[...], wout_ref[...])
    out = x + (0.5 * _gate(mod_ref, slot)) * _rms(y, gpost_ref[...])
    if split_out:
        @pl.when(is_prompt)
        def _():
            o_refs[0][...] = out

        @pl.when(jnp.logical_not(is_prompt))
        def _():
            o_refs[1][...] = out
    else:
        o_refs[0][...] = out


def _ffn(x, mod, g_pre, g_post, w_in, w_out, lead, slot, split_out=False):
    split_in = isinstance(x, tuple)
    xs = x if split_in else (x,)
    vmem = (2 * D_MODEL * D_FF + D_FF * D_MODEL) * 2 + 4 * ROW_TILE * D_MODEL * 4 \
        + ROW_TILE * D_FF * 2 + 8 * ROW_TILE * FF_CHUNK * 4 + 6 * ROW_TILE * D_MODEL * 4
    pair_specs = [_prompt_row_spec(D_MODEL), _sample_row_spec(D_MODEL)]
    scratch = [pltpu.VMEM((ROW_TILE, D_FF), BF16)]
    if split_in:
        scratch.append(pltpu.VMEM((ROW_TILE, D_MODEL), F32))
    if split_out:
        out_specs = pair_specs
        out_shape = [jax.ShapeDtypeStruct((N_PROMPT, D_MODEL), F32),
                     jax.ShapeDtypeStruct((N_SAMPLE, D_MODEL), F32)]
    else:
        out_specs = _row_spec(D_MODEL)
        out_shape = jax.ShapeDtypeStruct((N_TOK, D_MODEL), F32)
    return pl.pallas_call(
        functools.partial(_ffn_kernel, slot=slot, split_in=split_in, split_out=split_out),
        grid=(N_TOK // ROW_TILE,),
        in_specs=(pair_specs if split_in else [_row_spec(D_MODEL)]) + [
            _mod_spec(),
            _const_spec((1, D_MODEL)),
            _const_spec((1, D_MODEL)),
            _const_spec((D_MODEL, 2 * D_FF), lead),
            _const_spec((D_FF, D_MODEL), lead),
        ],
        out_specs=out_specs,
        out_shape=out_shape,
        scratch_shapes=scratch,
        compiler_params=pltpu.CompilerParams(
            dimension_semantics=("arbitrary",), vmem_limit_bytes=_vmem_limit(vmem)),
        name="ffn_sublayer",
    )(*xs, mod, g_pre.reshape(1, D_MODEL), g_post.reshape(1, D_MODEL), w_in, w_out)


def _qkv_kernel(x_ref, mod_ref, gpre_ref, w_ref, cos_ref, sin_ref, q_ref, kp_ref, vp_ref, k_ref, v_ref):
    i = pl.program_id(0)
    scale = HEAD_DIM ** -0.5
    h = _mod_in(x_ref[...], gpre_ref[...], mod_ref, 1).astype(BF16)
    qkv = _dot(h, w_ref[...])
    nq = N_Q_HEADS * HEAD_DIM
    nk = N_KV_HEADS * HEAD_DIM

    @pl.when(i < PROMPT_TILES)
    def _():
        q_ref[...] = (qkv[:, :nq] * scale).astype(BF16)
        kp_ref[...] = qkv[:, nq:nq + nk]
        vp_ref[...] = qkv[:, nq + nk:]

    @pl.when(i >= PROMPT_TILES)
    def _():
        v_ref[...] = qkv[:, nq + nk:]
        cos = cos_ref[...]
        sin = sin_ref[...]
        lane = lax.broadcasted_iota(jnp.int32, (ROW_TILE, LANES), 1)
        first = (lane & (HEAD_DIM // 4)) == 0

        def rope(xg):
            up = pltpu.roll(xg, LANES - HEAD_DIM // 4, 1)
            down = pltpu.roll(xg, HEAD_DIM // 4, 1)
            return xg * cos + jnp.where(first, up, down) * sin

        for j in range(nq // LANES):
            q_ref[:, j * LANES:(j + 1) * LANES] = (rope(qkv[:, j * LANES:(j + 1) * LANES]) * scale).astype(BF16)
        for j in range(nk // LANES):
            k_ref[:, j * LANES:(j + 1) * LANES] = rope(qkv[:, nq + j * LANES:nq + (j + 1) * LANES])


def _rope_tables():
    t = jnp.arange(DEC_SEQ)
    row = (t // GRID_W).astype(F32)
    col = (t % GRID_W).astype(F32)
    nf = HEAD_DIM // 4
    inv = ROPE_BASE ** (-jnp.arange(nf, dtype=F32) / nf)
    ar = row[:, None] * inv[None, :]
    ac = col[:, None] * inv[None, :]
    cos = jnp.concatenate([jnp.cos(ar), jnp.cos(ar), jnp.cos(ac), jnp.cos(ac)], axis=-1)
    sin = jnp.concatenate([-jnp.sin(ar), jnp.sin(ar), -jnp.sin(ac), jnp.sin(ac)], axis=-1)
    reps = LANES // HEAD_DIM
    return jnp.tile(cos, (1, reps)), jnp.tile(sin, (1, reps))


def _qkv(x, mod, g_pre, w_qkv, lead):
    cos, sin = _rope_tables()
    tab_spec = pl.BlockSpec(
        (ROW_TILE, LANES), lambda i: (jnp.maximum(i - PROMPT_TILES, 0) % TILES_PER_SAMPLE, 0))
    nk = N_KV_HEADS * HEAD_DIM
    vmem = D_MODEL * QKV_DIM * 2 + 6 * ROW_TILE * D_MODEL * 4 + 4 * ROW_TILE * QKV_DIM * 4
    return pl.pallas_call(
        _qkv_kernel,
        grid=(N_TOK // ROW_TILE,),
        in_specs=[
            _row_spec(D_MODEL),
            _mod_spec(),
            _const_spec((1, D_MODEL)),
            _const_spec((D_MODEL, QKV_DIM), lead),
            tab_spec,
            tab_spec,
        ],
        out_specs=[_row_spec(D_MODEL), _prompt_row_spec(nk), _prompt_row_spec(nk),
                   _sample_row_spec(nk), _sample_row_spec(nk)],
        out_shape=[
            jax.ShapeDtypeStruct((N_TOK, D_MODEL), BF16),
            jax.ShapeDtypeStruct((N_PROMPT, nk), F32),
            jax.ShapeDtypeStruct((N_PROMPT, nk), F32),
            jax.ShapeDtypeStruct((N_SAMPLE, nk), F32),
            jax.ShapeDtypeStruct((N_SAMPLE, nk), F32),
        ],
        compiler_params=pltpu.CompilerParams(
            dimension_semantics=("arbitrary",), vmem_limit_bytes=_vmem_limit(vmem)),
        name="attn_qkv",
    )(x, mod, g_pre.reshape(1, D_MODEL), w_qkv, cos, sin)


def _lane_halves(ref_or_val, hkv, rows):
    grp = ref_or_val[:, (hkv // 2) * LANES:(hkv // 2 + 1) * LANES]
    lane = lax.broadcasted_iota(jnp.int32, (rows, LANES), 1)
    in_low = lane < HEAD_DIM
    if hkv % 2 == 0:
        lo = jnp.where(in_low, grp, 0.0)
        hi = pltpu.roll(lo, HEAD_DIM, 1)
    else:
        hi = jnp.where(in_low, 0.0, grp)
        lo = pltpu.roll(hi, HEAD_DIM, 1)
    return lo.astype(BF16), hi.astype(BF16)


def _softmax_sink(s, sink):
    m = jnp.maximum(jnp.max(s, axis=-1, keepdims=True), sink)
    e = jnp.exp(s - m)
    denom = jnp.sum(e, axis=-1, keepdims=True) + jnp.exp(sink - m)
    return (e * (1.0 / denom)).astype(BF16)


def _attend(sink_ref, q_ref, o_ref, keys, vals, n_keys, valid):
    for hkv in range(N_KV_HEADS):
        k_lo, k_hi = _lane_halves(keys, hkv, n_keys)
        v_lo, v_hi = _lane_halves(vals, hkv, n_keys)
        for j in (2 * hkv, 2 * hkv + 1):
            q2 = q_ref[:, j * LANES:(j + 1) * LANES]
            s_a = _dot_nt(q2, k_lo)
            s_b = _dot_nt(q2, k_hi)
            if valid is not None:
                s_a = jnp.where(valid, s_a, MASK_VALUE)
                s_b = jnp.where(valid, s_b, MASK_VALUE)
            p_a = _softmax_sink(s_a, sink_ref[2 * j])
            p_b = _softmax_sink(s_b, sink_ref[2 * j + 1])
            o_ref[:, j * LANES:(j + 1) * LANES] = (_dot(p_a, v_lo) + _dot(p_b, v_hi)).astype(BF16)


def _attn_prompt_kernel(sink_ref, q_ref, k_ref, v_ref, o_ref):
    _attend(sink_ref, q_ref, o_ref, k_ref[...], v_ref[...], SEQ, None)


def _attn_prompt(sink, q, k, v):
    nk = N_KV_HEADS * HEAD_DIM
    return pl.pallas_call(
        _attn_prompt_kernel,
        grid=(BATCH,),
        in_specs=[
            pl.BlockSpec(memory_space=pltpu.SMEM),
            pl.BlockSpec((SEQ, D_MODEL), lambda b: (b, 0)),
            pl.BlockSpec((SEQ, nk), lambda b: (b, 0)),
            pl.BlockSpec((SEQ, nk), lambda b: (b, 0)),
        ],
        out_specs=pl.BlockSpec((SEQ, D_MODEL), lambda b: (b, 0)),
        out_shape=jax.ShapeDtypeStruct((N_PROMPT, D_MODEL), BF16),
        compiler_params=pltpu.CompilerParams(dimension_semantics=("parallel",)),
        name="attn_prompt",
    )(sink, q, k, v)


def _attn_sample_kernel(sink_ref, q_ref, k_ref, v_ref, ck_ref, cv_ref, o_ref):
    qb = pl.program_id(1)
    nblk = DEC_SEQ // ATTN_BLOCK
    B = ATTN_BLOCK
    starts = [
        pl.multiple_of(jnp.maximum(qb - 1, 0) * B, B),
        pl.multiple_of(qb * B, B),
        pl.multiple_of(jnp.minimum(qb + 1, nblk - 1) * B, B),
    ]
    keys = jnp.concatenate([k_ref[pl.ds(s, B), :] for s in starts] + [ck_ref[...]], axis=0)
    vals = jnp.concatenate([v_ref[pl.ds(s, B), :] for s in starts] + [cv_ref[...]], axis=0)
    n_keys = 3 * B + PAST_LEN
    r = lax.broadcasted_iota(jnp.int32, (B, n_keys), 0)
    c = lax.broadcasted_iota(jnp.int32, (B, n_keys), 1)
    prev_bad = (c < B) & ((c < r) | (qb == 0))
    next_bad = (c >= 2 * B) & (c < 3 * B) & (((c - 2 * B) > r) | (qb == nblk - 1))
    valid = jnp.logical_not(prev_bad | next_bad)
    _attend(sink_ref, q_ref, o_ref, keys, vals, n_keys, valid)


def _attn_sample(sink, q, k, v, cache_k, cache_v):
    nk = N_KV_HEADS * HEAD_DIM
    nblk = DEC_SEQ // ATTN_BLOCK
    q_off = N_PROMPT // ATTN_BLOCK
    return pl.pallas_call(
        _attn_sample_kernel,
        grid=(DEC_BATCH, nblk),
        in_specs=[
            pl.BlockSpec(memory_space=pltpu.SMEM),
            pl.BlockSpec((ATTN_BLOCK, D_MODEL), lambda b, t: (q_off + b * nblk + t, 0)),
            pl.BlockSpec((DEC_SEQ, nk), lambda b, t: (b, 0)),
            pl.BlockSpec((DEC_SEQ, nk), lambda b, t: (b, 0)),
            pl.BlockSpec((None, PAST_LEN, nk), lambda b, t: (b, 0, 0)),
            pl.BlockSpec((None, PAST_LEN, nk), lambda b, t: (b, 0, 0)),
        ],
        out_specs=pl.BlockSpec((ATTN_BLOCK, D_MODEL), lambda b, t: (b * nblk + t, 0)),
        out_shape=jax.ShapeDtypeStruct((N_SAMPLE, D_MODEL), BF16),
        compiler_params=pltpu.CompilerParams(dimension_semantics=("parallel", "parallel")),
        name="attn_sample",
    )(sink, q, k, v, cache_k, cache_v)


def _mix_out_kernel(op_ref, os_ref, x_ref, mod_ref, gpost_ref, w_ref, out_ref):
    is_prompt = pl.program_id(0) < PROMPT_TILES

    def finish(o_ref):
        y = _dot(o_ref[...], w_ref[...])
        out_ref[...] = x_ref[...] + _gate(mod_ref, 1) * _rms(y, gpost_ref[...])

    @pl.when(is_prompt)
    def _():
        finish(op_ref)

    @pl.when(jnp.logical_not(is_prompt))
    def _():
        finish(os_ref)


def _mix_out(o_prompt, o_sample, x, mod, g_post, w, lead):
    vmem = D_MODEL * D_MODEL * 2 + 12 * ROW_TILE * D_MODEL * 4
    return pl.pallas_call(
        _mix_out_kernel,
        grid=(N_TOK // ROW_TILE,),
        in_specs=[
            _prompt_row_spec(D_MODEL),
            _sample_row_spec(D_MODEL),
            _row_spec(D_MODEL),
            _mod_spec(),
            _const_spec((1, D_MODEL)),
            _const_spec((D_MODEL, D_MODEL), lead),
        ],
        out_specs=_row_spec(D_MODEL),
        out_shape=jax.ShapeDtypeStruct((N_TOK, D_MODEL), F32),
        compiler_params=pltpu.CompilerParams(
            dimension_semantics=("parallel",), vmem_limit_bytes=_vmem_limit(vmem)),
        name="mixer_out",
    )(o_prompt, o_sample, x, mod, g_post.reshape(1, D_MODEL), w)


def _cumsum_rows(tri3, x):
    hi = x.astype(BF16)
    r1 = x - hi.astype(F32)
    mid = r1.astype(BF16)
    lo = (r1 - mid.astype(F32)).astype(BF16)
    return _dot(tri3, jnp.concatenate([hi, mid, lo], axis=0))


def _chunk_masks():
    r = lax.broadcasted_iota(jnp.int32, (REC_GROUP, REC_GROUP), 0)
    c = lax.broadcasted_iota(jnp.int32, (REC_GROUP, REC_GROUP), 1)
    same = (r // CHUNK) == (c // CHUNK)
    return same & (c <= r), same & (c >= r)


def _rec_in_kernel(x_ref, mod_ref, gpre_ref, w_ref, lbf_ref, lbb_ref,
                   qdf_ref, kif_ref, kef_ref, qdb_ref, kib_ref, keb_ref, v_ref, sg_ref, decf_ref, decb_ref):
    G = REC_GROUP
    CB = REC_IN_COLS
    h = _mod_in(x_ref[...], gpre_ref[...], mod_ref, 1).astype(BF16)
    tris = tuple(jnp.concatenate([jnp.where(m, 1.0, 0.0).astype(BF16)] * 3, axis=1) for m in _chunk_masks())
    outs = ((qdf_ref, kif_ref, kef_ref, decf_ref, lbf_ref), (qdb_ref, kib_ref, keb_ref, decb_ref, lbb_ref))

    def project(cb):
        return [_dot(h, w_ref[:, part * D_MODEL + cb * CB:part * D_MODEL + (cb + 1) * CB]) for part in range(5)]

    def finish(cb, proj):
        cols = slice(cb * CB, (cb + 1) * CB)
        yq, yv, yzf, yzb, yg = proj
        qf = yq * _sigmoid(yq) * (REC_DK ** -0.5)
        v_ref[:, cols] = yv.astype(BF16)
        sg_ref[:, cols] = yg * _sigmoid(yg)
        for d, z in enumerate((yzf, yzb)):
            qd_ref, ki_ref, ke_ref, dec_ref, lb_ref = outs[d]
            one_m_lb = 1.0 - lb_ref[:, cols]
            e = jnp.exp(-jnp.abs(z))
            rcp = 1.0 / (1.0 + e)
            key = one_m_lb * jnp.where(z >= 0, e * rcp, rcp)
            logf = jnp.log(1.0 - key)
            for g in range(ROW_TILE // G):
                rows = slice(g * G, (g + 1) * G)
                bc = _cumsum_rows(tris[d], logf[rows])
                decay = jnp.exp(bc)
                k_inv = key[rows] * (1.0 / decay)
                qd_ref[rows, cols] = (qf[rows] * decay).astype(BF16)
                ki_ref[rows, cols] = k_inv.astype(BF16)
                for ch in range(G // CHUNK):
                    rr = slice(ch * CHUNK, (ch + 1) * CHUNK)
                    edge = ch * CHUNK + (CHUNK - 1 if d == 0 else 0)
                    chunk_decay = decay[edge:edge + 1, :]
                    ke_ref[g * G + ch * CHUNK:g * G + (ch + 1) * CHUNK, cols] = \
                        (k_inv[rr] * chunk_decay).astype(BF16)
                    dec_ref[g, ch:ch + 1, cols] = chunk_decay

    n_blocks = D_MODEL // CB
    proj = project(0)
    for cb in range(n_blocks):
        nxt = project(cb + 1) if cb + 1 < n_blocks else None
        finish(cb, proj)
        proj = nxt


def _rec_in(x, mod, g_pre, w, lead, lb_f, lb_b):
    n_chunks = REC_GROUP // CHUNK
    groups_per_tile = ROW_TILE // REC_GROUP
    vmem = D_MODEL * REC_IN_DIM * 2 + 4 * ROW_TILE * D_MODEL * 4 + 2 * ROW_TILE * D_MODEL * (7 * 2 + 4) \
        + 12 * ROW_TILE * D_MODEL * 4
    act = jax.ShapeDtypeStruct((N_TOK, D_MODEL), BF16)
    dec = jax.ShapeDtypeStruct((N_TOK // REC_GROUP, n_chunks, D_MODEL), F32)
    dec_spec = pl.BlockSpec((groups_per_tile, n_chunks, D_MODEL), lambda i: (i, 0, 0))
    return pl.pallas_call(
        _rec_in_kernel,
        grid=(N_TOK // ROW_TILE,),
        in_specs=[
            _row_spec(D_MODEL),
            _mod_spec(),
            _const_spec((1, D_MODEL)),
            _const_spec((D_MODEL, REC_IN_DIM), lead),
            _const_spec((1, D_MODEL)),
            _const_spec((1, D_MODEL)),
        ],
        out_specs=[_row_spec(D_MODEL) for _ in range(8)] + [dec_spec, dec_spec],
        out_shape=[act] * 7 + [jax.ShapeDtypeStruct((N_TOK, D_MODEL), F32), dec, dec],
        compiler_params=pltpu.CompilerParams(
            dimension_semantics=("parallel",), vmem_limit_bytes=_vmem_limit(vmem)),
        name="rec_in",
    )(x, mod, g_pre.reshape(1, D_MODEL), w, lb_f.reshape(1, D_MODEL), lb_b.reshape(1, D_MODEL))


def _rec_kernel(*refs, seq_len, has_init):
    qdf_ref, kif_ref, kef_ref, qdb_ref, kib_ref, keb_ref, v_ref, sg_ref, decf_ref, decb_ref, gn_ref = refs[:11]
    if has_init:
        s0_ref, o_ref = refs[11:]
        st_ref = None
    else:
        o_ref, st_ref = refs[11:]
        s0_ref = None
    G = REC_GROUP
    per_group = G // CHUNK
    n_chunks = seq_len // CHUNK
    mask_f, mask_b = _chunk_masks()

    def rows_of(c):
        return slice(c * CHUNK, (c + 1) * CHUNK)

    for hh in range(REC_HEADS_PER_STEP):
        cs = slice(hh * REC_DK, (hh + 1) * REC_DK)
        upd = []
        for c in range(n_chunks):
            k_end = jnp.concatenate([kef_ref[rows_of(c), cs], keb_ref[rows_of(c), cs]], axis=1)
            upd.append(_dot_tn(v_ref[rows_of(c), cs], k_end))
        if has_init:
            s_f, s_b = s0_ref[0, hh].T, s0_ref[1, hh].T
        else:
            s_f = s_b = jnp.zeros((REC_DV, REC_DK), F32)
        enter = [None] * n_chunks
        for c in range(n_chunks):
            enter[c] = s_f.astype(BF16)
            s_f = s_f * decf_ref[c // per_group, c % per_group:c % per_group + 1, cs] + upd[c][:, :REC_DK]
        for c in range(n_chunks - 1, -1, -1):
            enter[c] = jnp.concatenate([enter[c], s_b.astype(BF16)], axis=1)
            s_b = s_b * decb_ref[c // per_group, c % per_group:c % per_group + 1, cs] + upd[c][:, REC_DK:]
        if st_ref is not None:
            st_ref[0, hh] = s_f.T
            st_ref[1, hh] = s_b.T
        for g in range(seq_len // G):
            rows = slice(g * G, (g + 1) * G)
            qf_g = qdf_ref[rows, cs]
            qb_g = qdb_ref[rows, cs]
            a = jnp.where(mask_f, _dot_nt(qf_g, kif_ref[rows, cs]), 0.0) \
                + jnp.where(mask_b, _dot_nt(qb_g, kib_ref[rows, cs]), 0.0)
            o_g = _dot(a.astype(BF16), v_ref[rows, cs])
            for ch in range(per_group):
                c = g * per_group + ch
                q_cat = jnp.concatenate([qf_g[rows_of(ch)], qb_g[rows_of(ch)]], axis=1)
                o_c = o_g[rows_of(ch)] + _dot_nt(q_cat, enter[c])
                o_ref[rows_of(c), cs] = (_rms(o_c, gn_ref[hh]) * sg_ref[rows_of(c), cs]).astype(BF16)


def _rec(rec_acts, g_norm, s0, *, seq_len, n_seq, row_block_off):
    hb = REC_HEADS_PER_STEP
    w = hb * REC_DK
    n_hp = REC_HEADS // hb
    n_groups = seq_len // REC_GROUP
    has_init = s0 is not None

    def act_spec():
        return pl.BlockSpec((seq_len, w), lambda b, h: (row_block_off + b, h))

    def dec_spec():
        return pl.BlockSpec((n_groups, REC_GROUP // CHUNK, w), lambda b, h: (row_block_off + b, 0, h))

    in_specs = [act_spec() for _ in range(8)] + [dec_spec(), dec_spec(),
                                                 pl.BlockSpec((hb, 1, REC_DV), lambda b, h: (h, 0, 0))]
    args = list(rec_acts) + [g_norm.reshape(REC_HEADS, 1, REC_DV)]
    state_spec = pl.BlockSpec((None, 2, hb, REC_DK, REC_DV), lambda b, h: (b, 0, h, 0, 0))
    o_spec = pl.BlockSpec((seq_len, w), lambda b, h: (b, h))
    o_shape = jax.ShapeDtypeStruct((n_seq * seq_len, D_MODEL), BF16)
    if has_init:
        in_specs.append(state_spec)
        args.append(s0)
        out_specs, out_shape = o_spec, o_shape
    else:
        out_specs = [o_spec, state_spec]
        out_shape = [o_shape, jax.ShapeDtypeStruct((n_seq, 2, REC_HEADS, REC_DK, REC_DV), F32)]
    return pl.pallas_call(
        functools.partial(_rec_kernel, seq_len=seq_len, has_init=has_init),
        grid=(n_seq, n_hp),
        in_specs=in_specs,
        out_specs=out_specs,
        out_shape=out_shape,
        compiler_params=pltpu.CompilerParams(
            dimension_semantics=("parallel", "parallel"),
            vmem_limit_bytes=_vmem_limit(40 * seq_len * w * 4)),
        name="rec_scan_init" if has_init else "rec_scan",
    )(*args)


def kernel(x_prompt, x_sample, c, cache_k, cache_v, state_s, c_ctx, w_ada, b_ada, norm_pre, norm_post,
           w_ffn_in, w_ffn_out, w_qkv, w_attn_out, attn_sink, w_rec_in, rec_lb_logits, rec_norm, w_rec_out):
    x = (x_prompt.reshape(N_PROMPT, D_MODEL), x_sample.reshape(N_SAMPLE, D_MODEL))
    cond = jnp.concatenate([c_ctx[None], c, jnp.zeros((SUBLANES - N_COND, D_MODEL), F32)], axis=0)
    mods = _ada_mod(cond, w_ada, b_ada)

    lb_soft = jax.nn.softmax(rec_lb_logits.astype(F32), axis=1)
    lb_all = jnp.cumsum(lb_soft, axis=1) - lb_soft[:, :1]

    nk = N_KV_HEADS * HEAD_DIM
    w_ffn_in_b = w_ffn_in.astype(BF16)
    w_ffn_out_b = w_ffn_out.astype(BF16)
    w_qkv_b = w_qkv.astype(BF16)
    w_attn_out_b = w_attn_out.astype(BF16)
    w_rec_in_b = w_rec_in.astype(BF16)
    w_rec_out_b = w_rec_out.astype(BF16)
    new_k = new_v = new_s = None
    for i in range(DEPTH):
        mod = mods[i]
        x = _ffn(x, mod, norm_pre[i, 0], norm_post[i, 0], w_ffn_in_b, w_ffn_out_b, (i, 0), 0)
        j = i // 2
        if i % 2 == 0:
            q, k_p, v_p, k_s, v_s = _qkv(x, mod, norm_pre[i, 1], w_qkv_b, (j,))
            o_p = _attn_prompt(attn_sink[j], q, k_p, v_p)
            o_s = _attn_sample(attn_sink[j], q, k_s, v_s,
                               cache_k[:, j].reshape(DEC_BATCH, PAST_LEN, nk),
                               cache_v[:, j].reshape(DEC_BATCH, PAST_LEN, nk))
            x = _mix_out(o_p, o_s, x, mod, norm_post[i, 1], w_attn_out_b, (j,))
            new_k = k_p.reshape(BATCH, 1, SEQ, N_KV_HEADS, HEAD_DIM)
            new_v = v_p.reshape(BATCH, 1, SEQ, N_KV_HEADS, HEAD_DIM)
        else:
            acts = _rec_in(x, mod, norm_pre[i, 1], w_rec_in_b, (j,), lb_all[0, i], lb_all[1, i])
            o_p, s_p = _rec(acts, rec_norm[j], None, seq_len=SEQ, n_seq=BATCH, row_block_off=0)
            o_s = _rec(acts, rec_norm[j], state_s[:, j],
                       seq_len=DEC_SEQ, n_seq=DEC_BATCH, row_block_off=N_PROMPT // DEC_SEQ)
            x = _mix_out(o_p, o_s, x, mod, norm_post[i, 1], w_rec_out_b, (j,))
            new_s = s_p.reshape(BATCH, 1, 2, REC_HEADS, REC_DK, REC_DV)
        x = _ffn(x, mod, norm_pre[i, 2], norm_post[i, 2], w_ffn_in_b, w_ffn_out_b, (i, 1), 2,
                 split_out=(i == DEPTH - 1))
    y_prompt, y_sample = x
    return (y_prompt.reshape(BATCH, SEQ, D_MODEL), y_sample.reshape(DEC_BATCH, DEC_SEQ, D_MODEL),
            new_k, new_v, new_s)
```

```python
import functools

import jax
import jax.numpy as jnp
from jax import lax
from jax.experimental import pallas as pl
from jax.experimental.pallas import tpu as pltpu

F32 = jnp.float32
BF16 = jnp.bfloat16

D_MODEL = 1024
BATCH = 32
SEQ = 256
DEPTH = 2
DEC_BATCH = 2
DEC_SEQ = 1024
PAST_LEN = 256
GRID_W = 64
HEAD_DIM = 64
N_Q_HEADS = 16
N_KV_HEADS = 4
QKV_DIM = (N_Q_HEADS + 2 * N_KV_HEADS) * HEAD_DIM
ATTN_BLOCK = 128
ROPE_BASE = 10000.0
REC_HEADS = 8
REC_DK = 128
REC_DV = 128
REC_IN_DIM = 5 * D_MODEL
CHUNK = 64
D_FF = 2816
EPS = 1e-6
MASK_VALUE = -1e30

N_PROMPT = BATCH * SEQ
N_SAMPLE = DEC_BATCH * DEC_SEQ
N_TOK = N_PROMPT + N_SAMPLE
N_COND = 1 + DEC_BATCH

LANES = 128
SUBLANES = 8
VMEM_BYTES_V7X = 64 * 1024 * 1024

ROW_TILE = 512
FF_CHUNK = 256
ADA_TILE = 1024
REC_GROUP = 256
REC_HEADS_PER_STEP = 4
REC_IN_COLS = 256

PROMPT_TILES = N_PROMPT // ROW_TILE
TILES_PER_SAMPLE = DEC_SEQ // ROW_TILE


def _vmem_limit(nbytes):
    return int(min(VMEM_BYTES_V7X - 8 * 1024 * 1024, max(nbytes, 16 * 1024 * 1024)))


def _sigmoid(x):
    return 1.0 / (1.0 + jnp.exp(-x))


def _rms(x, g):
    ms = jnp.mean(x * x, axis=-1, keepdims=True)
    return x * lax.rsqrt(ms + EPS) * g


def _mod_in(x, g_pre, mod_ref, slot):
    shift = mod_ref[slot * 3:slot * 3 + 1, :]
    scale = mod_ref[slot * 3 + 1:slot * 3 + 2, :]
    return _rms(x, g_pre) * (1.0 + scale) + shift


def _gate(mod_ref, slot):
    return mod_ref[slot * 3 + 2:slot * 3 + 3, :]


def _dot(a, b):
    return jnp.dot(a, b, preferred_element_type=F32)


def _dot_nt(a, b):
    return lax.dot_general(a, b, (((1,), (1,)), ((), ())), preferred_element_type=F32)


def _dot_tn(a, b):
    return lax.dot_general(a, b, (((0,), (0,)), ((), ())), preferred_element_type=F32)


def _tile_group(i):
    return jnp.where(i < PROMPT_TILES, 0, 1 + (i - PROMPT_TILES) // TILES_PER_SAMPLE)


def _row_spec(width):
    return pl.BlockSpec((ROW_TILE, width), lambda i: (i, 0))


def _prompt_row_spec(width):
    return pl.BlockSpec((ROW_TILE, width), lambda i: (jnp.minimum(i, PROMPT_TILES - 1), 0))


def _sample_row_spec(width):
    return pl.BlockSpec((ROW_TILE, width), lambda i: (jnp.maximum(i - PROMPT_TILES, 0), 0))


def _mod_spec():
    return pl.BlockSpec((None, 9, D_MODEL), lambda i: (_tile_group(i), 0, 0))


def _const_spec(shape, lead=()):
    nd = len(shape)
    return pl.BlockSpec((None,) * len(lead) + tuple(shape), lambda *_: tuple(lead) + (0,) * nd,
                        pipeline_mode=pl.Buffered(1))


def _ada_kernel(cond_ref, w_ref, b_ref, o_ref):
    c = cond_ref[...]
    s = (c * _sigmoid(c)).astype(BF16)
    o_ref[...] = _dot(s, w_ref[...].astype(BF16)) + b_ref[...]


def _ada_mod(cond, w_ada, b_ada):
    n_out = 9 * D_MODEL
    out = pl.pallas_call(
        _ada_kernel,
        grid=(DEPTH, n_out // ADA_TILE),
        in_specs=[
            pl.BlockSpec((SUBLANES, D_MODEL), lambda l, j: (0, 0)),
            pl.BlockSpec((None, D_MODEL, ADA_TILE), lambda l, j: (l, 0, j)),
            pl.BlockSpec((None, 1, ADA_TILE), lambda l, j: (l, 0, j)),
        ],
        out_specs=pl.BlockSpec((None, SUBLANES, ADA_TILE), lambda l, j: (l, 0, j)),
        out_shape=jax.ShapeDtypeStruct((DEPTH, SUBLANES, n_out), F32),
        compiler_params=pltpu.CompilerParams(
            dimension_semantics=("parallel", "parallel"),
            vmem_limit_bytes=_vmem_limit(4 * D_MODEL * ADA_TILE * 4)),
        name="ada_mod",
    )(cond, w_ada, b_ada.reshape(DEPTH, 1, n_out))
    return out[:, :N_COND].reshape(DEPTH, N_COND, 9, D_MODEL)


def _ffn_kernel(*refs, slot, split_in, split_out):
    refs = list(refs)
    x_refs = [refs.pop(0) for _ in range(2 if split_in else 1)]
    mod_ref, gpre_ref, gpost_ref, win_ref, wout_ref = (refs.pop(0) for _ in range(5))
    o_refs = [refs.pop(0) for _ in range(2 if split_out else 1)]
    g_scr = refs.pop(0)
    is_prompt = pl.program_id(0) < PROMPT_TILES
    if split_in:
        x_scr = refs.pop(0)

        @pl.when(is_prompt)
        def _():
            x_scr[...] = x_refs[0][...]

        @pl.when(jnp.logical_not(is_prompt))
        def _():
            x_scr[...] = x_refs[1][...]

        x = x_scr[...]
    else:
        x = x_refs[0][...]
    h = _mod_in(x, gpre_ref[...], mod_ref, slot).astype(BF16)
    for c in range(D_FF // FF_CHUNK):
        lo = c * FF_CHUNK
        a = _dot(h, win_ref[:, lo:lo + FF_CHUNK])
        b = _dot(h, win_ref[:, D_FF + lo:D_FF + lo + FF_CHUNK])
        g_scr[:, lo:lo + FF_CHUNK] = (a * _sigmoid(a) * b).astype(BF16)
    y = _dot(g_scr[...], wout_ref[...])
    out = x + (0.5 * _gate(mod_ref, slot)) * _rms(y, gpost_ref[...])
    if split_out:
        @pl.when(is_prompt)
        def _():
            o_refs[0][...] = out

        @pl.when(jnp.logical_not(is_prompt))
        def _():
            o_refs[1][...] = out
    else:
        o_refs[0][...] = out


def _ffn(x, mod, g_pre, g_post, w_in, w_out, lead, slot, split_out=False):
    split_in = isinstance(x, tuple)
    xs = x if split_in else (x,)
    vmem = (2 * D_MODEL * D_FF + D_FF * D_MODEL) * 2 + 4 * ROW_TILE * D_MODEL * 4 \
        + ROW_TILE * D_FF * 2 + 8 * ROW_TILE * FF_CHUNK * 4 + 6 * ROW_TILE * D_MODEL * 4
    pair_specs = [_prompt_row_spec(D_MODEL), _sample_row_spec(D_MODEL)]
    scratch = [pltpu.VMEM((ROW_TILE, D_FF), BF16)]
    if split_in:
        scratch.append(pltpu.VMEM((ROW_TILE, D_MODEL), F32))
    if split_out:
        out_specs = pair_specs
        out_shape = [jax.ShapeDtypeStruct((N_PROMPT, D_MODEL), F32),
                     jax.ShapeDtypeStruct((N_SAMPLE, D_MODEL), F32)]
    else:
        out_specs = _row_spec(D_MODEL)
        out_shape = jax.ShapeDtypeStruct((N_TOK, D_MODEL), F32)
    return pl.pallas_call(
        functools.partial(_ffn_kernel, slot=slot, split_in=split_in, split_out=split_out),
        grid=(N_TOK // ROW_TILE,),
        in_specs=(pair_specs if split_in else [_row_spec(D_MODEL)]) + [
            _mod_spec(),
            _const_spec((1, D_MODEL)),
            _const_spec((1, D_MODEL)),
            _const_spec((D_MODEL, 2 * D_FF), lead),
            _const_spec((D_FF, D_MODEL), lead),
        ],
        out_specs=out_specs,
        out_shape=out_shape,
        scratch_shapes=scratch,
        compiler_params=pltpu.CompilerParams(
            dimension_semantics=("arbitrary",), vmem_limit_bytes=_vmem_limit(vmem)),
        name="ffn_sublayer",
    )(*xs, mod, g_pre.reshape(1, D_MODEL), g_post.reshape(1, D_MODEL), w_in, w_out)


def _qkv_kernel(x_ref, mod_ref, gpre_ref, w_ref, cos_ref, sin_ref, q_ref, kp_ref, vp_ref, k_ref, v_ref):
    i = pl.program_id(0)
    scale = HEAD_DIM ** -0.5
    h = _mod_in(x_ref[...], gpre_ref[...], mod_ref, 1).astype(BF16)
    qkv = _dot(h, w_ref[...])
    nq = N_Q_HEADS * HEAD_DIM
    nk = N_KV_HEADS * HEAD_DIM

    @pl.when(i < PROMPT_TILES)
    def _():
        q_ref[...] = (qkv[:, :nq] * scale).astype(BF16)
        kp_ref[...] = qkv[:, nq:nq + nk]
        vp_ref[...] = qkv[:, nq + nk:]

    @pl.when(i >= PROMPT_TILES)
    def _():
        v_ref[...] = qkv[:, nq + nk:]
        cos = cos_ref[...]
        sin = sin_ref[...]
        lane = lax.broadcasted_iota(jnp.int32, (ROW_TILE, LANES), 1)
        first = (lane & (HEAD_DIM // 4)) == 0

        def rope(xg):
            up = pltpu.roll(xg, LANES - HEAD_DIM // 4, 1)
            down = pltpu.roll(xg, HEAD_DIM // 4, 1)
            return xg * cos + jnp.where(first, up, down) * sin

        for j in range(nq // LANES):
            q_ref[:, j * LANES:(j + 1) * LANES] = (rope(qkv[:, j * LANES:(j + 1) * LANES]) * scale).astype(BF16)
        for j in range(nk // LANES):
            k_ref[:, j * LANES:(j + 1) * LANES] = rope(qkv[:, nq + j * LANES:nq + (j + 1) * LANES])


def _rope_tables():
    t = jnp.arange(DEC_SEQ)
    row = (t // GRID_W).astype(F32)
    col = (t % GRID_W).astype(F32)
    nf = HEAD_DIM // 4
    inv = ROPE_BASE ** (-jnp.arange(nf, dtype=F32) / nf)
    ar = row[:, None] * inv[None, :]
    ac = col[:, None] * inv[None, :]
    cos = jnp.concatenate([jnp.cos(ar), jnp.cos(ar), jnp.cos(ac), jnp.cos(ac)], axis=-1)
    sin = jnp.concatenate([-jnp.sin(ar), jnp.sin(ar), -jnp.sin(ac), jnp.sin(ac)], axis=-1)
    reps = LANES // HEAD_DIM
    return jnp.tile(cos, (1, reps)), jnp.tile(sin, (1, reps))


def _qkv(x, mod, g_pre, w_qkv, lead):
    cos, sin = _rope_tables()
    tab_spec = pl.BlockSpec(
        (ROW_TILE, LANES), lambda i: (jnp.maximum(i - PROMPT_TILES, 0) % TILES_PER_SAMPLE, 0))
    nk = N_KV_HEADS * HEAD_DIM
    vmem = D_MODEL * QKV_DIM * 2 + 6 * ROW_TILE * D_MODEL * 4 + 4 * ROW_TILE * QKV_DIM * 4
    return pl.pallas_call(
        _qkv_kernel,
        grid=(N_TOK // ROW_TILE,),
        in_specs=[
            _row_spec(D_MODEL),
            _mod_spec(),
            _const_spec((1, D_MODEL)),
            _const_spec((D_MODEL, QKV_DIM), lead),
            tab_spec,
            tab_spec,
        ],
        out_specs=[_row_spec(D_MODEL), _prompt_row_spec(nk), _prompt_row_spec(nk),
                   _sample_row_spec(nk), _sample_row_spec(nk)],
        out_shape=[
            jax.ShapeDtypeStruct((N_TOK, D_MODEL), BF16),
            jax.ShapeDtypeStruct((N_PROMPT, nk), F32),
            jax.ShapeDtypeStruct((N_PROMPT, nk), F32),
            jax.ShapeDtypeStruct((N_SAMPLE, nk), F32),
            jax.ShapeDtypeStruct((N_SAMPLE, nk), F32),
        ],
        compiler_params=pltpu.CompilerParams(
            dimension_semantics=("arbitrary",), vmem_limit_bytes=_vmem_limit(vmem)),
        name="attn_qkv",
    )(x, mod, g_pre.reshape(1, D_MODEL), w_qkv, cos, sin)


def _lane_halves(ref_or_val, hkv, rows):
    grp = ref_or_val[:, (hkv // 2) * LANES:(hkv // 2 + 1) * LANES]
    lane = lax.broadcasted_iota(jnp.int32, (rows, LANES), 1)
    in_low = lane < HEAD_DIM
    if hkv % 2 == 0:
        lo = jnp.where(in_low, grp, 0.0)
        hi = pltpu.roll(lo, HEAD_DIM, 1)
    else:
        hi = jnp.where(in_low, 0.0, grp)
        lo = pltpu.roll(hi, HEAD_DIM, 1)
    return lo.astype(BF16), hi.astype(BF16)


def _attend(sink_ref, q_ref, o_ref, keys, vals, n_keys, valid):
    group = N_Q_HEADS // N_KV_HEADS
    k_halves = [_lane_halves(keys, hkv, n_keys) for hkv in range(N_KV_HEADS)]
    v_halves = [_lane_halves(vals, hkv, n_keys) for hkv in range(N_KV_HEADS)]
    scores = []
    for h in range(N_Q_HEADS):
        s = _dot_nt(q_ref[:, (h // 2) * LANES:(h // 2 + 1) * LANES], k_halves[h // group][h % 2])
        scores.append(s if valid is None else jnp.where(valid, s, MASK_VALUE))
    heads = range(N_Q_HEADS)
    maxes = [jnp.maximum(jnp.max(scores[h], axis=-1, keepdims=True), sink_ref[h]) for h in heads]
    exps = [jnp.exp(scores[h] - maxes[h]) for h in heads]
    denoms = [jnp.sum(exps[h], axis=-1, keepdims=True) + jnp.exp(sink_ref[h] - maxes[h]) for h in heads]
    probs = [(exps[h] * (1.0 / denoms[h])).astype(BF16) for h in heads]
    for j in range(N_Q_HEADS // 2):
        v_lo, v_hi = v_halves[(2 * j) // group]
        o_ref[:, j * LANES:(j + 1) * LANES] = \
            (_dot(probs[2 * j], v_lo) + _dot(probs[2 * j + 1], v_hi)).astype(BF16)


def _attn_prompt_kernel(sink_ref, q_ref, k_ref, v_ref, o_ref):
    _attend(sink_ref, q_ref, o_ref, k_ref[...], v_ref[...], SEQ, None)


def _attn_prompt(sink, q, k, v):
    nk = N_KV_HEADS * HEAD_DIM
    return pl.pallas_call(
        _attn_prompt_kernel,
        grid=(BATCH,),
        in_specs=[
            pl.BlockSpec(memory_space=pltpu.SMEM),
            pl.BlockSpec((SEQ, D_MODEL), lambda b: (b, 0)),
            pl.BlockSpec((SEQ, nk), lambda b: (b, 0)),
            pl.BlockSpec((SEQ, nk), lambda b: (b, 0)),
        ],
        out_specs=pl.BlockSpec((SEQ, D_MODEL), lambda b: (b, 0)),
        out_shape=jax.ShapeDtypeStruct((N_PROMPT, D_MODEL), BF16),
        compiler_params=pltpu.CompilerParams(dimension_semantics=("parallel",)),
        name="attn_prompt",
    )(sink, q, k, v)


def _attn_sample_kernel(sink_ref, q_ref, k_ref, v_ref, ck_ref, cv_ref, o_ref):
    qb = pl.program_id(1)
    nblk = DEC_SEQ // ATTN_BLOCK
    B = ATTN_BLOCK
    starts = [
        pl.multiple_of(jnp.maximum(qb - 1, 0) * B, B),
        pl.multiple_of(qb * B, B),
        pl.multiple_of(jnp.minimum(qb + 1, nblk - 1) * B, B),
    ]
    keys = jnp.concatenate([k_ref[pl.ds(s, B), :] for s in starts] + [ck_ref[...]], axis=0)
    vals = jnp.concatenate([v_ref[pl.ds(s, B), :] for s in starts] + [cv_ref[...]], axis=0)
    n_keys = 3 * B + PAST_LEN
    r = lax.broadcasted_iota(jnp.int32, (B, n_keys), 0)
    c = lax.broadcasted_iota(jnp.int32, (B, n_keys), 1)
    prev_bad = (c < B) & ((c < r) | (qb == 0))
    next_bad = (c >= 2 * B) & (c < 3 * B) & (((c - 2 * B) > r) | (qb == nblk - 1))
    valid = jnp.logical_not(prev_bad | next_bad)
    _attend(sink_ref, q_ref, o_ref, keys, vals, n_keys, valid)


def _attn_sample(sink, q, k, v, cache_k, cache_v):
    nk = N_KV_HEADS * HEAD_DIM
    nblk = DEC_SEQ // ATTN_BLOCK
    q_off = N_PROMPT // ATTN_BLOCK
    return pl.pallas_call(
        _attn_sample_kernel,
        grid=(DEC_BATCH, nblk),
        in_specs=[
            pl.BlockSpec(memory_space=pltpu.SMEM),
            pl.BlockSpec((ATTN_BLOCK, D_MODEL), lambda b, t: (q_off + b * nblk + t, 0)),
            pl.BlockSpec((DEC_SEQ, nk), lambda b, t: (b, 0)),
            pl.BlockSpec((DEC_SEQ, nk), lambda b, t: (b, 0)),
            pl.BlockSpec((None, PAST_LEN, nk), lambda b, t: (b, 0, 0)),
            pl.BlockSpec((None, PAST_LEN, nk), lambda b, t: (b, 0, 0)),
        ],
        out_specs=pl.BlockSpec((ATTN_BLOCK, D_MODEL), lambda b, t: (b * nblk + t, 0)),
        out_shape=jax.ShapeDtypeStruct((N_SAMPLE, D_MODEL), BF16),
        compiler_params=pltpu.CompilerParams(dimension_semantics=("parallel", "parallel")),
        name="attn_sample",
    )(sink, q, k, v, cache_k, cache_v)


def _mix_out_kernel(op_ref, os_ref, x_ref, mod_ref, gpost_ref, w_ref, out_ref):
    is_prompt = pl.program_id(0) < PROMPT_TILES

    def finish(o_ref):
        y = _dot(o_ref[...], w_ref[...])
        out_ref[...] = x_ref[...] + _gate(mod_ref, 1) * _rms(y, gpost_ref[...])

    @pl.when(is_prompt)
    def _():
        finish(op_ref)

    @pl.when(jnp.logical_not(is_prompt))
    def _():
        finish(os_ref)


def _mix_out(o_prompt, o_sample, x, mod, g_post, w, lead):
    vmem = D_MODEL * D_MODEL * 2 + 12 * ROW_TILE * D_MODEL * 4
    return pl.pallas_call(
        _mix_out_kernel,
        grid=(N_TOK // ROW_TILE,),
        in_specs=[
            _prompt_row_spec(D_MODEL),
            _sample_row_spec(D_MODEL),
            _row_spec(D_MODEL),
            _mod_spec(),
            _const_spec((1, D_MODEL)),
            _const_spec((D_MODEL, D_MODEL), lead),
        ],
        out_specs=_row_spec(D_MODEL),
        out_shape=jax.ShapeDtypeStruct((N_TOK, D_MODEL), F32),
        compiler_params=pltpu.CompilerParams(
            dimension_semantics=("parallel",), vmem_limit_bytes=_vmem_limit(vmem)),
        name="mixer_out",
    )(o_prompt, o_sample, x, mod, g_post.reshape(1, D_MODEL), w)


def _cumsum_rows(tri3, x):
    hi = x.astype(BF16)
    r1 = x - hi.astype(F32)
    mid = r1.astype(BF16)
    lo = (r1 - mid.astype(F32)).astype(BF16)
    return _dot(tri3, jnp.concatenate([hi, mid, lo], axis=0))


def _chunk_masks():
    r = lax.broadcasted_iota(jnp.int32, (REC_GROUP, REC_GROUP), 0)
    c = lax.broadcasted_iota(jnp.int32, (REC_GROUP, REC_GROUP), 1)
    same = (r // CHUNK) == (c // CHUNK)
    return same & (c <= r), same & (c >= r)


def _rec_in_kernel(x_ref, mod_ref, gpre_ref, w_ref, lbf_ref, lbb_ref,
                   qdf_ref, kif_ref, kef_ref, qdb_ref, kib_ref, keb_ref, v_ref, sg_ref, decf_ref, decb_ref):
    G = REC_GROUP
    CB = REC_IN_COLS
    h = _mod_in(x_ref[...], gpre_ref[...], mod_ref, 1).astype(BF16)
    tris = tuple(jnp.concatenate([jnp.where(m, 1.0, 0.0).astype(BF16)] * 3, axis=1) for m in _chunk_masks())
    outs = ((qdf_ref, kif_ref, kef_ref, decf_ref, lbf_ref), (qdb_ref, kib_ref, keb_ref, decb_ref, lbb_ref))

    def project(cb):
        return [_dot(h, w_ref[:, part * D_MODEL + cb * CB:part * D_MODEL + (cb + 1) * CB]) for part in range(5)]

    def finish(cb, proj):
        cols = slice(cb * CB, (cb + 1) * CB)
        yq, yv, yzf, yzb, yg = proj
        qf = yq * _sigmoid(yq) * (REC_DK ** -0.5)
        v_ref[:, cols] = yv.astype(BF16)
        sg_ref[:, cols] = yg * _sigmoid(yg)
        for d, z in enumerate((yzf, yzb)):
            qd_ref, ki_ref, ke_ref, dec_ref, lb_ref = outs[d]
            one_m_lb = 1.0 - lb_ref[:, cols]
            e = jnp.exp(-jnp.abs(z))
            rcp = 1.0 / (1.0 + e)
            key = one_m_lb * jnp.where(z >= 0, e * rcp, rcp)
            logf = jnp.log(1.0 - key)
            for g in range(ROW_TILE // G):
                rows = slice(g * G, (g + 1) * G)
                bc = _cumsum_rows(tris[d], logf[rows])
                decay = jnp.exp(bc)
                k_inv = key[rows] * (1.0 / decay)
                qd_ref[rows, cols] = (qf[rows] * decay).astype(BF16)
                ki_ref[rows, cols] = k_inv.astype(BF16)
                for ch in range(G // CHUNK):
                    rr = slice(ch * CHUNK, (ch + 1) * CHUNK)
                    edge = ch * CHUNK + (CHUNK - 1 if d == 0 else 0)
                    chunk_decay = decay[edge:edge + 1, :]
                    ke_ref[g * G + ch * CHUNK:g * G + (ch + 1) * CHUNK, cols] = \
                        (k_inv[rr] * chunk_decay).astype(BF16)
                    dec_ref[g, ch:ch + 1, cols] = chunk_decay

    n_blocks = D_MODEL // CB
    proj = project(0)
    for cb in range(n_blocks):
        nxt = project(cb + 1) if cb + 1 < n_blocks else None
        finish(cb, proj)
        proj = nxt


def _rec_in(x, mod, g_pre, w, lead, lb_f, lb_b):
    n_chunks = REC_GROUP // CHUNK
    groups_per_tile = ROW_TILE // REC_GROUP
    vmem = D_MODEL * REC_IN_DIM * 2 + 4 * ROW_TILE * D_MODEL * 4 + 2 * ROW_TILE * D_MODEL * (7 * 2 + 4) \
        + 12 * ROW_TILE * D_MODEL * 4
    act = jax.ShapeDtypeStruct((N_TOK, D_MODEL), BF16)
    dec = jax.ShapeDtypeStruct((N_TOK // REC_GROUP, n_chunks, D_MODEL), F32)
    dec_spec = pl.BlockSpec((groups_per_tile, n_chunks, D_MODEL), lambda i: (i, 0, 0))
    return pl.pallas_call(
        _rec_in_kernel,
        grid=(N_TOK // ROW_TILE,),
        in_specs=[
            _row_spec(D_MODEL),
            _mod_spec(),
            _const_spec((1, D_MODEL)),
            _const_spec((D_MODEL, REC_IN_DIM), lead),
            _const_spec((1, D_MODEL)),
            _const_spec((1, D_MODEL)),
        ],
        out_specs=[_row_spec(D_MODEL) for _ in range(8)] + [dec_spec, dec_spec],
        out_shape=[act] * 7 + [jax.ShapeDtypeStruct((N_TOK, D_MODEL), F32), dec, dec],
        compiler_params=pltpu.CompilerParams(
            dimension_semantics=("parallel",), vmem_limit_bytes=_vmem_limit(vmem)),
        name="rec_in",
    )(x, mod, g_pre.reshape(1, D_MODEL), w, lb_f.reshape(1, D_MODEL), lb_b.reshape(1, D_MODEL))


def _rec_kernel(*refs, seq_len, has_init):
    qdf_ref, kif_ref, kef_ref, qdb_ref, kib_ref, keb_ref, v_ref, sg_ref, decf_ref, decb_ref, gn_ref = refs[:11]
    if has_init:
        s0_ref, o_ref = refs[11:]
        st_ref = None
    else:
        o_ref, st_ref = refs[11:]
        s0_ref = None
    G = REC_GROUP
    per_group = G // CHUNK
    n_chunks = seq_len // CHUNK
    mask_f, mask_b = _chunk_masks()

    def rows_of(c):
        return slice(c * CHUNK, (c + 1) * CHUNK)

    for hh in range(REC_HEADS_PER_STEP):
        cs = slice(hh * REC_DK, (hh + 1) * REC_DK)
        upd = []
        for c in range(n_chunks):
            k_end = jnp.concatenate([kef_ref[rows_of(c), cs], keb_ref[rows_of(c), cs]], axis=1)
            upd.append(_dot_tn(v_ref[rows_of(c), cs], k_end))
        if has_init:
            s_f, s_b = s0_ref[0, hh].T, s0_ref[1, hh].T
        else:
            s_f = s_b = jnp.zeros((REC_DV, REC_DK), F32)
        enter = [None] * n_chunks
        for c in range(n_chunks):
            enter[c] = s_f.astype(BF16)
            s_f = s_f * decf_ref[c // per_group, c % per_group:c % per_group + 1, cs] + upd[c][:, :REC_DK]
        for c in range(n_chunks - 1, -1, -1):
            enter[c] = jnp.concatenate([enter[c], s_b.astype(BF16)], axis=1)
            s_b = s_b * decb_ref[c // per_group, c % per_group:c % per_group + 1, cs] + upd[c][:, REC_DK:]
        if st_ref is not None:
            st_ref[0, hh] = s_f.T
            st_ref[1, hh] = s_b.T
        for g in range(seq_len // G):
            rows = slice(g * G, (g + 1) * G)
            qf_g = qdf_ref[rows, cs]
            qb_g = qdb_ref[rows, cs]
            a = jnp.where(mask_f, _dot_nt(qf_g, kif_ref[rows, cs]), 0.0) \
                + jnp.where(mask_b, _dot_nt(qb_g, kib_ref[rows, cs]), 0.0)
            o_g = _dot(a.astype(BF16), v_ref[rows, cs])
            for ch in range(per_group):
                c = g * per_group + ch
                q_cat = jnp.concatenate([qf_g[rows_of(ch)], qb_g[rows_of(ch)]], axis=1)
                o_c = o_g[rows_of(ch)] + _dot_nt(q_cat, enter[c])
                o_ref[rows_of(c), cs] = (_rms(o_c, gn_ref[hh]) * sg_ref[rows_of(c), cs]).astype(BF16)


def _rec(rec_acts, g_norm, s0, *, seq_len, n_seq, row_block_off):
    hb = REC_HEADS_PER_STEP
    w = hb * REC_DK
    n_hp = REC_HEADS // hb
    n_groups = seq_len // REC_GROUP
    has_init = s0 is not None

    def act_spec():
        return pl.BlockSpec((seq_len, w), lambda b, h: (row_block_off + b, h))

    def dec_spec():
        return pl.BlockSpec((n_groups, REC_GROUP // CHUNK, w), lambda b, h: (row_block_off + b, 0, h))

    in_specs = [act_spec() for _ in range(8)] + [dec_spec(), dec_spec(),
                                                 pl.BlockSpec((hb, 1, REC_DV), lambda b, h: (h, 0, 0))]
    args = list(rec_acts) + [g_norm.reshape(REC_HEADS, 1, REC_DV)]
    state_spec = pl.BlockSpec((None, 2, hb, REC_DK, REC_DV), lambda b, h: (b, 0, h, 0, 0))
    o_spec = pl.BlockSpec((seq_len, w), lambda b, h: (b, h))
    o_shape = jax.ShapeDtypeStruct((n_seq * seq_len, D_MODEL), BF16)
    if has_init:
        in_specs.append(state_spec)
        args.append(s0)
        out_specs, out_shape = o_spec, o_shape
    else:
        out_specs = [o_spec, state_spec]
        out_shape = [o_shape, jax.ShapeDtypeStruct((n_seq, 2, REC_HEADS, REC_DK, REC_DV), F32)]
    return pl.pallas_call(
        functools.partial(_rec_kernel, seq_len=seq_len, has_init=has_init),
        grid=(n_seq, n_hp),
        in_specs=in_specs,
        out_specs=out_specs,
        out_shape=out_shape,
        compiler_params=pltpu.CompilerParams(
            dimension_semantics=("parallel", "parallel"),
            vmem_limit_bytes=_vmem_limit(40 * seq_len * w * 4)),
        name="rec_scan_init" if has_init else "rec_scan",
    )(*args)


def kernel(x_prompt, x_sample, c, cache_k, cache_v, state_s, c_ctx, w_ada, b_ada, norm_pre, norm_post,
           w_ffn_in, w_ffn_out, w_qkv, w_attn_out, attn_sink, w_rec_in, rec_lb_logits, rec_norm, w_rec_out):
    x = (x_prompt.reshape(N_PROMPT, D_MODEL), x_sample.reshape(N_SAMPLE, D_MODEL))
    cond = jnp.concatenate([c_ctx[None], c, jnp.zeros((SUBLANES - N_COND, D_MODEL), F32)], axis=0)
    mods = _ada_mod(cond, w_ada, b_ada)

    lb_soft = jax.nn.softmax(rec_lb_logits.astype(F32), axis=1)
    lb_all = jnp.cumsum(lb_soft, axis=1) - lb_soft[:, :1]

    nk = N_KV_HEADS * HEAD_DIM
    w_ffn_in_b = w_ffn_in.astype(BF16)
    w_ffn_out_b = w_ffn_out.astype(BF16)
    w_qkv_b = w_qkv.astype(BF16)
    w_attn_out_b = w_attn_out.astype(BF16)
    w_rec_in_b = w_rec_in.astype(BF16)
    w_rec_out_b = w_rec_out.astype(BF16)
    new_k = new_v = new_s = None
    for i in range(DEPTH):
        mod = mods[i]
        x = _ffn(x, mod, norm_pre[i, 0], norm_post[i, 0], w_ffn_in_b, w_ffn_out_b, (i, 0), 0)
        j = i // 2
        if i % 2 == 0:
            q, k_p, v_p, k_s, v_s = _qkv(x, mod, norm_pre[i, 1], w_qkv_b, (j,))
            o_p = _attn_prompt(attn_sink[j], q, k_p, v_p)
            o_s = _attn_sample(attn_sink[j], q, k_s, v_s,
                               cache_k[:, j].reshape(DEC_BATCH, PAST_LEN, nk),
                               cache_v[:, j].reshape(DEC_BATCH, PAST_LEN, nk))
            x = _mix_out(o_p, o_s, x, mod, norm_post[i, 1], w_attn_out_b, (j,))
            new_k = k_p.reshape(BATCH, 1, SEQ, N_KV_HEADS, HEAD_DIM)
            new_v = v_p.reshape(BATCH, 1, SEQ, N_KV_HEADS, HEAD_DIM)
        else:
            acts = _rec_in(x, mod, norm_pre[i, 1], w_rec_in_b, (j,), lb_all[0, i], lb_all[1, i])
            o_p, s_p = _rec(acts, rec_norm[j], None, seq_len=SEQ, n_seq=BATCH, row_block_off=0)
            o_s = _rec(acts, rec_norm[j], state_s[:, j],
                       seq_len=DEC_SEQ, n_seq=DEC_BATCH, row_block_off=N_PROMPT // DEC_SEQ)
            x = _mix_out(o_p, o_s, x, mod, norm_post[i, 1], w_rec_out_b, (j,))
            new_s = s_p.reshape(BATCH, 1, 2, REC_HEADS, REC_DK, REC_DV)
        x = _ffn(x, mod, norm_pre[i, 2], norm_post[i, 2], w_ffn_in_b, w_ffn_out_b, (i, 1), 2,
                 split_out=(i == DEPTH - 1))
    y_prompt, y_sample = x
    return (y_prompt.reshape(BATCH, SEQ, D_MODEL), y_sample.reshape(DEC_BATCH, DEC_SEQ, D_MODEL),
            new_k, new_v, new_s)
```

```python
import functools

import jax
import jax.numpy as jnp
from jax import lax
from jax.experimental import pallas as pl
from jax.experimental.pallas import tpu as pltpu

F32 = jnp.float32
BF16 = jnp.bfloat16

D_MODEL = 1024
BATCH = 32
SEQ = 256
DEPTH = 2
DEC_BATCH = 2
DEC_SEQ = 1024
PAST_LEN = 256
GRID_W = 64
HEAD_DIM = 64
N_Q_HEADS = 16
N_KV_HEADS = 4
QKV_DIM = (N_Q_HEADS + 2 * N_KV_HEADS) * HEAD_DIM
ATTN_BLOCK = 128
ROPE_BASE = 10000.0
REC_HEADS = 8
REC_DK = 128
REC_DV = 128
REC_IN_DIM = 5 * D_MODEL
CHUNK = 64
D_FF = 2816
EPS = 1e-6
MASK_VALUE = -1e30

N_PROMPT = BATCH * SEQ
N_SAMPLE = DEC_BATCH * DEC_SEQ
N_TOK = N_PROMPT + N_SAMPLE
N_COND = 1 + DEC_BATCH

LANES = 128
SUBLANES = 8
VMEM_BYTES_V7X = 64 * 1024 * 1024

ROW_TILE = 512
FF_CHUNK = 256
ADA_TILE = 1024
REC_GROUP = 256
REC_STEP_ROWS_X_HEADS = 4096
REC_IN_COLS = 256
FFN_SUBTILES = 2

PROMPT_TILES = N_PROMPT // ROW_TILE
TILES_PER_SAMPLE = DEC_SEQ // ROW_TILE


def _vmem_limit(nbytes):
    return int(min(VMEM_BYTES_V7X - 8 * 1024 * 1024, max(nbytes, 16 * 1024 * 1024)))


def _sigmoid(x):
    return 1.0 / (1.0 + jnp.exp(-x))


def _rms(x, g):
    ms = jnp.mean(x * x, axis=-1, keepdims=True)
    return x * lax.rsqrt(ms + EPS) * g


def _mod_in(x, g_pre, mod_ref, slot):
    shift = mod_ref[slot * 3:slot * 3 + 1, :]
    scale = mod_ref[slot * 3 + 1:slot * 3 + 2, :]
    return _rms(x, g_pre * (1.0 + scale)) + shift


def _mod_out(x, y, g_post, mod_ref, slot, weight):
    gate = mod_ref[slot * 3 + 2:slot * 3 + 3, :]
    return x + _rms(y, (weight * gate) * g_post)


def _dot(a, b):
    return jnp.dot(a, b, preferred_element_type=F32)


def _dot_nt(a, b):
    return lax.dot_general(a, b, (((1,), (1,)), ((), ())), preferred_element_type=F32)


def _dot_tn(a, b):
    return lax.dot_general(a, b, (((0,), (0,)), ((), ())), preferred_element_type=F32)


def _tile_group(i):
    return jnp.where(i < PROMPT_TILES, 0, 1 + (i - PROMPT_TILES) // TILES_PER_SAMPLE)


def _row_spec(width):
    return pl.BlockSpec((ROW_TILE, width), lambda i: (i, 0))


def _prompt_row_spec(width):
    return pl.BlockSpec((ROW_TILE, width), lambda i: (jnp.minimum(i, PROMPT_TILES - 1), 0))


def _sample_row_spec(width):
    return pl.BlockSpec((ROW_TILE, width), lambda i: (jnp.maximum(i - PROMPT_TILES, 0), 0))


def _mod_spec():
    return pl.BlockSpec((None, 9, D_MODEL), lambda i: (_tile_group(i), 0, 0))


def _const_spec(shape, lead=()):
    nd = len(shape)
    return pl.BlockSpec((None,) * len(lead) + tuple(shape), lambda *_: tuple(lead) + (0,) * nd,
                        pipeline_mode=pl.Buffered(1))


def _ada_kernel(cond_ref, w_ref, b_ref, o_ref):
    c = cond_ref[...]
    s = (c * _sigmoid(c)).astype(BF16)
    o_ref[...] = _dot(s, w_ref[...].astype(BF16)) + b_ref[...]


def _ada_mod(cond, w_ada, b_ada):
    n_out = 9 * D_MODEL
    out = pl.pallas_call(
        _ada_kernel,
        grid=(DEPTH, n_out // ADA_TILE),
        in_specs=[
            pl.BlockSpec((SUBLANES, D_MODEL), lambda l, j: (0, 0)),
            pl.BlockSpec((None, D_MODEL, ADA_TILE), lambda l, j: (l, 0, j)),
            pl.BlockSpec((None, 1, ADA_TILE), lambda l, j: (l, 0, j)),
        ],
        out_specs=pl.BlockSpec((None, SUBLANES, ADA_TILE), lambda l, j: (l, 0, j)),
        out_shape=jax.ShapeDtypeStruct((DEPTH, SUBLANES, n_out), F32),
        compiler_params=pltpu.CompilerParams(
            dimension_semantics=("parallel", "parallel"),
            vmem_limit_bytes=_vmem_limit(4 * D_MODEL * ADA_TILE * 4)),
        name="ada_mod",
    )(cond, w_ada, b_ada.reshape(DEPTH, 1, n_out))
    return out[:, :N_COND].reshape(DEPTH, N_COND, 9, D_MODEL)


def _ffn_kernel(*refs, slot, split_in, split_out):
    refs = list(refs)
    x_refs = [refs.pop(0) for _ in range(2 if split_in else 1)]
    mod_ref, gpre_ref, gpost_ref, win_ref, wout_ref = (refs.pop(0) for _ in range(5))
    o_refs = [refs.pop(0) for _ in range(2 if split_out else 1)]
    g_scr = refs.pop(0)
    is_prompt = pl.program_id(0) < PROMPT_TILES
    if split_in:
        x_scr = refs.pop(0)

        @pl.when(is_prompt)
        def _():
            x_scr[...] = x_refs[0][...]

        @pl.when(jnp.logical_not(is_prompt))
        def _():
            x_scr[...] = x_refs[1][...]

        x = x_scr[...]
    else:
        x = x_refs[0][...]
    parts = []
    sub = ROW_TILE // FFN_SUBTILES
    for s in range(FFN_SUBTILES):
        rows = slice(s * sub, (s + 1) * sub)
        xs = x[rows]
        h = _mod_in(xs, gpre_ref[...], mod_ref, slot).astype(BF16)
        for c in range(D_FF // FF_CHUNK):
            lo = c * FF_CHUNK
            a = _dot(h, win_ref[:, lo:lo + FF_CHUNK])
            b = _dot(h, win_ref[:, D_FF + lo:D_FF + lo + FF_CHUNK])
            g_scr[rows, lo:lo + FF_CHUNK] = (a * _sigmoid(a) * b).astype(BF16)
        y = _dot(g_scr[rows, :], wout_ref[...])
        parts.append(_mod_out(xs, y, gpost_ref[...], mod_ref, slot, 0.5))
    out = jnp.concatenate(parts, axis=0)
    if split_out:
        @pl.when(is_prompt)
        def _():
            o_refs[0][...] = out

        @pl.when(jnp.logical_not(is_prompt))
        def _():
            o_refs[1][...] = out
    else:
        o_refs[0][...] = out


def _ffn(x, mod, g_pre, g_post, w_in, w_out, lead, slot, split_out=False):
    split_in = isinstance(x, tuple)
    xs = x if split_in else (x,)
    vmem = (2 * D_MODEL * D_FF + D_FF * D_MODEL) * 2 + 4 * ROW_TILE * D_MODEL * 4 \
        + ROW_TILE * D_FF * 2 + 8 * ROW_TILE * FF_CHUNK * 4 + 6 * ROW_TILE * D_MODEL * 4
    pair_specs = [_prompt_row_spec(D_MODEL), _sample_row_spec(D_MODEL)]
    scratch = [pltpu.VMEM((ROW_TILE, D_FF), BF16)]
    if split_in:
        scratch.append(pltpu.VMEM((ROW_TILE, D_MODEL), F32))
    if split_out:
        out_specs = pair_specs
        out_shape = [jax.ShapeDtypeStruct((N_PROMPT, D_MODEL), F32),
                     jax.ShapeDtypeStruct((N_SAMPLE, D_MODEL), F32)]
    else:
        out_specs = _row_spec(D_MODEL)
        out_shape = jax.ShapeDtypeStruct((N_TOK, D_MODEL), F32)
    return pl.pallas_call(
        functools.partial(_ffn_kernel, slot=slot, split_in=split_in, split_out=split_out),
        grid=(N_TOK // ROW_TILE,),
        in_specs=(pair_specs if split_in else [_row_spec(D_MODEL)]) + [
            _mod_spec(),
            _const_spec((1, D_MODEL)),
            _const_spec((1, D_MODEL)),
            _const_spec((D_MODEL, 2 * D_FF), lead),
            _const_spec((D_FF, D_MODEL), lead),
        ],
        out_specs=out_specs,
        out_shape=out_shape,
        scratch_shapes=scratch,
        compiler_params=pltpu.CompilerParams(
            dimension_semantics=("arbitrary",), vmem_limit_bytes=_vmem_limit(vmem)),
        name="ffn_sublayer",
    )(*xs, mod, g_pre.reshape(1, D_MODEL), g_post.reshape(1, D_MODEL), w_in, w_out)


def _qkv_kernel(x_ref, mod_ref, gpre_ref, w_ref, cos_ref, sin_ref, q_ref, kp_ref, vp_ref, k_ref, v_ref):
    i = pl.program_id(0)
    scale = HEAD_DIM ** -0.5
    h = _mod_in(x_ref[...], gpre_ref[...], mod_ref, 1).astype(BF16)
    qkv = _dot(h, w_ref[...])
    nq = N_Q_HEADS * HEAD_DIM
    nk = N_KV_HEADS * HEAD_DIM

    @pl.when(i < PROMPT_TILES)
    def _():
        q_ref[...] = (qkv[:, :nq] * scale).astype(BF16)
        kp_ref[...] = qkv[:, nq:nq + nk]
        vp_ref[...] = qkv[:, nq + nk:]

    @pl.when(i >= PROMPT_TILES)
    def _():
        v_ref[...] = qkv[:, nq + nk:]
        cos = cos_ref[...]
        sin = sin_ref[...]
        lane = lax.broadcasted_iota(jnp.int32, (ROW_TILE, LANES), 1)
        first = (lane & (HEAD_DIM // 4)) == 0

        def rope(xg):
            up = pltpu.roll(xg, LANES - HEAD_DIM // 4, 1)
            down = pltpu.roll(xg, HEAD_DIM // 4, 1)
            return xg * cos + jnp.where(first, up, down) * sin

        for j in range(nq // LANES):
            q_ref[:, j * LANES:(j + 1) * LANES] = (rope(qkv[:, j * LANES:(j + 1) * LANES]) * scale).astype(BF16)
        for j in range(nk // LANES):
            k_ref[:, j * LANES:(j + 1) * LANES] = rope(qkv[:, nq + j * LANES:nq + (j + 1) * LANES])


def _rope_tables():
    t = jnp.arange(DEC_SEQ)
    row = (t // GRID_W).astype(F32)
    col = (t % GRID_W).astype(F32)
    nf = HEAD_DIM // 4
    inv = ROPE_BASE ** (-jnp.arange(nf, dtype=F32) / nf)
    ar = row[:, None] * inv[None, :]
    ac = col[:, None] * inv[None, :]
    cos = jnp.concatenate([jnp.cos(ar), jnp.cos(ar), jnp.cos(ac), jnp.cos(ac)], axis=-1)
    sin = jnp.concatenate([-jnp.sin(ar), jnp.sin(ar), -jnp.sin(ac), jnp.sin(ac)], axis=-1)
    reps = LANES // HEAD_DIM
    return jnp.tile(cos, (1, reps)), jnp.tile(sin, (1, reps))


def _qkv(x, mod, g_pre, w_qkv, lead):
    cos, sin = _rope_tables()
    tab_spec = pl.BlockSpec(
        (ROW_TILE, LANES), lambda i: (jnp.maximum(i - PROMPT_TILES, 0) % TILES_PER_SAMPLE, 0))
    nk = N_KV_HEADS * HEAD_DIM
    vmem = D_MODEL * QKV_DIM * 2 + 6 * ROW_TILE * D_MODEL * 4 + 4 * ROW_TILE * QKV_DIM * 4
    return pl.pallas_call(
        _qkv_kernel,
        grid=(N_TOK // ROW_TILE,),
        in_specs=[
            _row_spec(D_MODEL),
            _mod_spec(),
            _const_spec((1, D_MODEL)),
            _const_spec((D_MODEL, QKV_DIM), lead),
            tab_spec,
            tab_spec,
        ],
        out_specs=[_row_spec(D_MODEL), _prompt_row_spec(nk), _prompt_row_spec(nk),
                   _sample_row_spec(nk), _sample_row_spec(nk)],
        out_shape=[
            jax.ShapeDtypeStruct((N_TOK, D_MODEL), BF16),
            jax.ShapeDtypeStruct((N_PROMPT, nk), F32),
            jax.ShapeDtypeStruct((N_PROMPT, nk), F32),
            jax.ShapeDtypeStruct((N_SAMPLE, nk), F32),
            jax.ShapeDtypeStruct((N_SAMPLE, nk), F32),
        ],
        compiler_params=pltpu.CompilerParams(
            dimension_semantics=("arbitrary",), vmem_limit_bytes=_vmem_limit(vmem)),
        name="attn_qkv",
    )(x, mod, g_pre.reshape(1, D_MODEL), w_qkv, cos, sin)


def _lane_halves(ref_or_val, hkv, rows):
    grp = ref_or_val[:, (hkv // 2) * LANES:(hkv // 2 + 1) * LANES]
    lane = lax.broadcasted_iota(jnp.int32, (rows, LANES), 1)
    in_low = lane < HEAD_DIM
    if hkv % 2 == 0:
        lo = jnp.where(in_low, grp, 0.0)
        hi = pltpu.roll(lo, HEAD_DIM, 1)
    else:
        hi = jnp.where(in_low, 0.0, grp)
        lo = pltpu.roll(hi, HEAD_DIM, 1)
    return lo.astype(BF16), hi.astype(BF16)


def _attend(sink_ref, q_ref, o_ref, keys, vals, n_keys, valid):
    group = N_Q_HEADS // N_KV_HEADS
    k_halves = [_lane_halves(keys, hkv, n_keys) for hkv in range(N_KV_HEADS)]
    v_halves = [_lane_halves(vals, hkv, n_keys) for hkv in range(N_KV_HEADS)]
    scores = []
    for h in range(N_Q_HEADS):
        s = _dot_nt(q_ref[:, (h // 2) * LANES:(h // 2 + 1) * LANES], k_halves[h // group][h % 2])
        scores.append(s if valid is None else jnp.where(valid, s, MASK_VALUE))
    heads = range(N_Q_HEADS)
    maxes = [jnp.maximum(jnp.max(scores[h], axis=-1, keepdims=True), sink_ref[h]) for h in heads]
    exps = [jnp.exp(scores[h] - maxes[h]) for h in heads]
    denoms = [jnp.sum(exps[h], axis=-1, keepdims=True) + jnp.exp(sink_ref[h] - maxes[h]) for h in heads]
    probs = [(exps[h] * (1.0 / denoms[h])).astype(BF16) for h in heads]
    for j in range(N_Q_HEADS // 2):
        v_lo, v_hi = v_halves[(2 * j) // group]
        o_ref[:, j * LANES:(j + 1) * LANES] = \
            (_dot(probs[2 * j], v_lo) + _dot(probs[2 * j + 1], v_hi)).astype(BF16)


def _attn_prompt_kernel(sink_ref, q_ref, k_ref, v_ref, o_ref):
    _attend(sink_ref, q_ref, o_ref, k_ref[...], v_ref[...], SEQ, None)


def _attn_prompt(sink, q, k, v):
    nk = N_KV_HEADS * HEAD_DIM
    return pl.pallas_call(
        _attn_prompt_kernel,
        grid=(BATCH,),
        in_specs=[
            pl.BlockSpec(memory_space=pltpu.SMEM),
            pl.BlockSpec((SEQ, D_MODEL), lambda b: (b, 0)),
            pl.BlockSpec((SEQ, nk), lambda b: (b, 0)),
            pl.BlockSpec((SEQ, nk), lambda b: (b, 0)),
        ],
        out_specs=pl.BlockSpec((SEQ, D_MODEL), lambda b: (b, 0)),
        out_shape=jax.ShapeDtypeStruct((N_PROMPT, D_MODEL), BF16),
        compiler_params=pltpu.CompilerParams(dimension_semantics=("parallel",)),
        name="attn_prompt",
    )(sink, q, k, v)


def _attn_sample_kernel(sink_ref, q_ref, k_ref, v_ref, ck_ref, cv_ref, o_ref):
    qb = pl.program_id(1)
    nblk = DEC_SEQ // ATTN_BLOCK
    B = ATTN_BLOCK
    starts = [
        pl.multiple_of(jnp.maximum(qb - 1, 0) * B, B),
        pl.multiple_of(qb * B, B),
        pl.multiple_of(jnp.minimum(qb + 1, nblk - 1) * B, B),
    ]
    keys = jnp.concatenate([k_ref[pl.ds(s, B), :] for s in starts] + [ck_ref[...]], axis=0)
    vals = jnp.concatenate([v_ref[pl.ds(s, B), :] for s in starts] + [cv_ref[...]], axis=0)
    n_keys = 3 * B + PAST_LEN
    r = lax.broadcasted_iota(jnp.int32, (B, n_keys), 0)
    c = lax.broadcasted_iota(jnp.int32, (B, n_keys), 1)
    prev_bad = (c < B) & ((c < r) | (qb == 0))
    next_bad = (c >= 2 * B) & (c < 3 * B) & (((c - 2 * B) > r) | (qb == nblk - 1))
    valid = jnp.logical_not(prev_bad | next_bad)
    _attend(sink_ref, q_ref, o_ref, keys, vals, n_keys, valid)


def _attn_sample(sink, q, k, v, cache_k, cache_v):
    nk = N_KV_HEADS * HEAD_DIM
    nblk = DEC_SEQ // ATTN_BLOCK
    q_off = N_PROMPT // ATTN_BLOCK
    return pl.pallas_call(
        _attn_sample_kernel,
        grid=(DEC_BATCH, nblk),
        in_specs=[
            pl.BlockSpec(memory_space=pltpu.SMEM),
            pl.BlockSpec((ATTN_BLOCK, D_MODEL), lambda b, t: (q_off + b * nblk + t, 0)),
            pl.BlockSpec((DEC_SEQ, nk), lambda b, t: (b, 0)),
            pl.BlockSpec((DEC_SEQ, nk), lambda b, t: (b, 0)),
            pl.BlockSpec((None, PAST_LEN, nk), lambda b, t: (b, 0, 0)),
            pl.BlockSpec((None, PAST_LEN, nk), lambda b, t: (b, 0, 0)),
        ],
        out_specs=pl.BlockSpec((ATTN_BLOCK, D_MODEL), lambda b, t: (b * nblk + t, 0)),
        out_shape=jax.ShapeDtypeStruct((N_SAMPLE, D_MODEL), BF16),
        compiler_params=pltpu.CompilerParams(dimension_semantics=("parallel", "parallel")),
        name="attn_sample",
    )(sink, q, k, v, cache_k, cache_v)


def _mix_out_kernel(op_ref, os_ref, x_ref, mod_ref, gpost_ref, w_ref, out_ref):
    is_prompt = pl.program_id(0) < PROMPT_TILES

    def finish(o_ref):
        y = _dot(o_ref[...], w_ref[...])
        out_ref[...] = _mod_out(x_ref[...], y, gpost_ref[...], mod_ref, 1, 1.0)

    @pl.when(is_prompt)
    def _():
        finish(op_ref)

    @pl.when(jnp.logical_not(is_prompt))
    def _():
        finish(os_ref)


def _mix_out(o_prompt, o_sample, x, mod, g_post, w, lead):
    vmem = D_MODEL * D_MODEL * 2 + 12 * ROW_TILE * D_MODEL * 4
    return pl.pallas_call(
        _mix_out_kernel,
        grid=(N_TOK // ROW_TILE,),
        in_specs=[
            _prompt_row_spec(D_MODEL),
            _sample_row_spec(D_MODEL),
            _row_spec(D_MODEL),
            _mod_spec(),
            _const_spec((1, D_MODEL)),
            _const_spec((D_MODEL, D_MODEL), lead),
        ],
        out_specs=_row_spec(D_MODEL),
        out_shape=jax.ShapeDtypeStruct((N_TOK, D_MODEL), F32),
        compiler_params=pltpu.CompilerParams(
            dimension_semantics=("parallel",), vmem_limit_bytes=_vmem_limit(vmem)),
        name="mixer_out",
    )(o_prompt, o_sample, x, mod, g_post.reshape(1, D_MODEL), w)


def _cumsum_rows(tri3, x):
    hi = x.astype(BF16)
    r1 = x - hi.astype(F32)
    mid = r1.astype(BF16)
    lo = (r1 - mid.astype(F32)).astype(BF16)
    return _dot(tri3, jnp.concatenate([hi, mid, lo], axis=0))


def _chunk_masks():
    r = lax.broadcasted_iota(jnp.int32, (REC_GROUP, REC_GROUP), 0)
    c = lax.broadcasted_iota(jnp.int32, (REC_GROUP, REC_GROUP), 1)
    same = (r // CHUNK) == (c // CHUNK)
    return same & (c <= r), same & (c >= r)


def _rec_in_kernel(x_ref, mod_ref, gpre_ref, w_ref, lbf_ref, lbb_ref,
                   qdf_ref, kif_ref, kef_ref, qdb_ref, kib_ref, keb_ref, v_ref, sg_ref, decf_ref, decb_ref):
    G = REC_GROUP
    CB = REC_IN_COLS
    h = _mod_in(x_ref[...], gpre_ref[...], mod_ref, 1).astype(BF16)
    tris = tuple(jnp.concatenate([jnp.where(m, 1.0, 0.0).astype(BF16)] * 3, axis=1) for m in _chunk_masks())
    outs = ((qdf_ref, kif_ref, kef_ref, decf_ref, lbf_ref), (qdb_ref, kib_ref, keb_ref, decb_ref, lbb_ref))

    def project(cb):
        return [_dot(h, w_ref[:, part * D_MODEL + cb * CB:part * D_MODEL + (cb + 1) * CB]) for part in range(5)]

    def finish(cb, proj):
        cols = slice(cb * CB, (cb + 1) * CB)
        yq, yv, yzf, yzb, yg = proj
        qf = yq * _sigmoid(yq) * (REC_DK ** -0.5)
        v_ref[:, cols] = yv.astype(BF16)
        sg_ref[:, cols] = yg * _sigmoid(yg)
        for d, z in enumerate((yzf, yzb)):
            qd_ref, ki_ref, ke_ref, dec_ref, lb_ref = outs[d]
            one_m_lb = 1.0 - lb_ref[:, cols]
            e = jnp.exp(-jnp.abs(z))
            rcp = 1.0 / (1.0 + e)
            key = one_m_lb * jnp.where(z >= 0, e * rcp, rcp)
            logf = jnp.log(1.0 - key)
            for g in range(ROW_TILE // G):
                rows = slice(g * G, (g + 1) * G)
                bc = _cumsum_rows(tris[d], logf[rows])
                decay = jnp.exp(bc)
                k_inv = key[rows] * (1.0 / decay)
                qd_ref[rows, cols] = (qf[rows] * decay).astype(BF16)
                ki_ref[rows, cols] = k_inv.astype(BF16)
                for ch in range(G // CHUNK):
                    rr = slice(ch * CHUNK, (ch + 1) * CHUNK)
                    edge = ch * CHUNK + (CHUNK - 1 if d == 0 else 0)
                    chunk_decay = decay[edge:edge + 1, :]
                    ke_ref[g * G + ch * CHUNK:g * G + (ch + 1) * CHUNK, cols] = \
                        (k_inv[rr] * chunk_decay).astype(BF16)
                    dec_ref[g, ch:ch + 1, cols] = chunk_decay

    n_blocks = D_MODEL // CB
    proj = project(0)
    for cb in range(n_blocks):
        nxt = project(cb + 1) if cb + 1 < n_blocks else None
        finish(cb, proj)
        proj = nxt


def _rec_in(x, mod, g_pre, w, lead, lb_f, lb_b):
    n_chunks = REC_GROUP // CHUNK
    groups_per_tile = ROW_TILE // REC_GROUP
    vmem = D_MODEL * REC_IN_DIM * 2 + 4 * ROW_TILE * D_MODEL * 4 + 2 * ROW_TILE * D_MODEL * (7 * 2 + 4) \
        + 12 * ROW_TILE * D_MODEL * 4
    act = jax.ShapeDtypeStruct((N_TOK, D_MODEL), BF16)
    dec = jax.ShapeDtypeStruct((N_TOK // REC_GROUP, n_chunks, D_MODEL), F32)
    dec_spec = pl.BlockSpec((groups_per_tile, n_chunks, D_MODEL), lambda i: (i, 0, 0))
    return pl.pallas_call(
        _rec_in_kernel,
        grid=(N_TOK // ROW_TILE,),
        in_specs=[
            _row_spec(D_MODEL),
            _mod_spec(),
            _const_spec((1, D_MODEL)),
            _const_spec((D_MODEL, REC_IN_DIM), lead),
            _const_spec((1, D_MODEL)),
            _const_spec((1, D_MODEL)),
        ],
        out_specs=[_row_spec(D_MODEL) for _ in range(8)] + [dec_spec, dec_spec],
        out_shape=[act] * 7 + [jax.ShapeDtypeStruct((N_TOK, D_MODEL), F32), dec, dec],
        compiler_params=pltpu.CompilerParams(
            dimension_semantics=("parallel",), vmem_limit_bytes=_vmem_limit(vmem)),
        name="rec_in",
    )(x, mod, g_pre.reshape(1, D_MODEL), w, lb_f.reshape(1, D_MODEL), lb_b.reshape(1, D_MODEL))


def _rec_kernel(*refs, seq_len, n_heads, has_init):
    qdf_ref, kif_ref, kef_ref, qdb_ref, kib_ref, keb_ref, v_ref, sg_ref, decf_ref, decb_ref, gn_ref = refs[:11]
    if has_init:
        s0_ref, o_ref = refs[11:]
        st_ref = None
    else:
        o_ref, st_ref = refs[11:]
        s0_ref = None
    G = REC_GROUP
    per_group = G // CHUNK
    n_chunks = seq_len // CHUNK
    mask_f, mask_b = _chunk_masks()

    def rows_of(c):
        return slice(c * CHUNK, (c + 1) * CHUNK)

    heads = range(n_heads)
    cols = [slice(hh * REC_DK, (hh + 1) * REC_DK) for hh in heads]
    groups = range(seq_len // G)

    def group_rows(g):
        return slice(g * G, (g + 1) * G)

    upd = [[_dot_tn(v_ref[rows_of(c), cs],
                    jnp.concatenate([kef_ref[rows_of(c), cs], keb_ref[rows_of(c), cs]], axis=1))
            for c in range(n_chunks)] for cs in cols]
    a = [[(jnp.where(mask_f, _dot_nt(qdf_ref[group_rows(g), cs], kif_ref[group_rows(g), cs]), 0.0)
           + jnp.where(mask_b, _dot_nt(qdb_ref[group_rows(g), cs], kib_ref[group_rows(g), cs]), 0.0)
           ).astype(BF16) for g in groups] for cs in cols]
    enter = []
    for hh in heads:
        cs = cols[hh]
        if has_init:
            s_f, s_b = s0_ref[0, hh].T, s0_ref[1, hh].T
        else:
            s_f = s_b = jnp.zeros((REC_DV, REC_DK), F32)
        ent = [None] * n_chunks
        for c in range(n_chunks):
            ent[c] = s_f.astype(BF16)
            s_f = s_f * decf_ref[c // per_group, c % per_group:c % per_group + 1, cs] + upd[hh][c][:, :REC_DK]
        for c in range(n_chunks - 1, -1, -1):
            ent[c] = jnp.concatenate([ent[c], s_b.astype(BF16)], axis=1)
            s_b = s_b * decb_ref[c // per_group, c % per_group:c % per_group + 1, cs] + upd[hh][c][:, REC_DK:]
        enter.append(ent)
        if st_ref is not None:
            st_ref[0, hh] = s_f.T
            st_ref[1, hh] = s_b.T
    o_intra = [[_dot(a[hh][g], v_ref[group_rows(g), cols[hh]]) for g in groups] for hh in heads]
    for hh in heads:
        cs = cols[hh]
        for c in range(n_chunks):
            q_cat = jnp.concatenate([qdf_ref[rows_of(c), cs], qdb_ref[rows_of(c), cs]], axis=1)
            o_c = o_intra[hh][c // per_group][rows_of(c % per_group)] + _dot_nt(q_cat, enter[hh][c])
            o_ref[rows_of(c), cs] = (_rms(o_c, gn_ref[hh]) * sg_ref[rows_of(c), cs]).astype(BF16)


def _rec(rec_acts, g_norm, s0, *, seq_len, n_seq, row_block_off):
    hb = min(REC_HEADS, REC_STEP_ROWS_X_HEADS // seq_len)
    w = hb * REC_DK
    n_hp = REC_HEADS // hb
    n_groups = seq_len // REC_GROUP
    has_init = s0 is not None

    def act_spec():
        return pl.BlockSpec((seq_len, w), lambda b, h: (row_block_off + b, h))

    def dec_spec():
        return pl.BlockSpec((n_groups, REC_GROUP // CHUNK, w), lambda b, h: (row_block_off + b, 0, h))

    in_specs = [act_spec() for _ in range(8)] + [dec_spec(), dec_spec(),
                                                 pl.BlockSpec((hb, 1, REC_DV), lambda b, h: (h, 0, 0))]
    args = list(rec_acts) + [g_norm.reshape(REC_HEADS, 1, REC_DV)]
    state_spec = pl.BlockSpec((None, 2, hb, REC_DK, REC_DV), lambda b, h: (b, 0, h, 0, 0))
    o_spec = pl.BlockSpec((seq_len, w), lambda b, h: (b, h))
    o_shape = jax.ShapeDtypeStruct((n_seq * seq_len, D_MODEL), BF16)
    if has_init:
        in_specs.append(state_spec)
        args.append(s0)
        out_specs, out_shape = o_spec, o_shape
    else:
        out_specs = [o_spec, state_spec]
        out_shape = [o_shape, jax.ShapeDtypeStruct((n_seq, 2, REC_HEADS, REC_DK, REC_DV), F32)]
    return pl.pallas_call(
        functools.partial(_rec_kernel, seq_len=seq_len, n_heads=hb, has_init=has_init),
        grid=(n_seq, n_hp),
        in_specs=in_specs,
        out_specs=out_specs,
        out_shape=out_shape,
        compiler_params=pltpu.CompilerParams(
            dimension_semantics=("parallel", "parallel"),
            vmem_limit_bytes=_vmem_limit(40 * seq_len * w * 4)),
        name="rec_scan_init" if has_init else "rec_scan",
    )(*args)


def kernel(x_prompt, x_sample, c, cache_k, cache_v, state_s, c_ctx, w_ada, b_ada, norm_pre, norm_post,
           w_ffn_in, w_ffn_out, w_qkv, w_attn_out, attn_sink, w_rec_in, rec_lb_logits, rec_norm, w_rec_out):
    x = (x_prompt.reshape(N_PROMPT, D_MODEL), x_sample.reshape(N_SAMPLE, D_MODEL))
    cond = jnp.concatenate([c_ctx[None], c, jnp.zeros((SUBLANES - N_COND, D_MODEL), F32)], axis=0)
    mods = _ada_mod(cond, w_ada, b_ada)

    lb_soft = jax.nn.softmax(rec_lb_logits.astype(F32), axis=1)
    lb_all = jnp.cumsum(lb_soft, axis=1) - lb_soft[:, :1]

    nk = N_KV_HEADS * HEAD_DIM
    w_ffn_in_b = w_ffn_in.astype(BF16)
    w_ffn_out_b = w_ffn_out.astype(BF16)
    w_qkv_b = w_qkv.astype(BF16)
    w_attn_out_b = w_attn_out.astype(BF16)
    w_rec_in_b = w_rec_in.astype(BF16)
    w_rec_out_b = w_rec_out.astype(BF16)
    new_k = new_v = new_s = None
    for i in range(DEPTH):
        mod = mods[i]
        x = _ffn(x, mod, norm_pre[i, 0], norm_post[i, 0], w_ffn_in_b, w_ffn_out_b, (i, 0), 0)
        j = i // 2
        if i % 2 == 0:
            q, k_p, v_p, k_s, v_s = _qkv(x, mod, norm_pre[i, 1], w_qkv_b, (j,))
            o_p = _attn_prompt(attn_sink[j], q, k_p, v_p)
            o_s = _attn_sample(attn_sink[j], q, k_s, v_s,
                               cache_k[:, j].reshape(DEC_BATCH, PAST_LEN, nk),
                               cache_v[:, j].reshape(DEC_BATCH, PAST_LEN, nk))
            x = _mix_out(o_p, o_s, x, mod, norm_post[i, 1], w_attn_out_b, (j,))
            new_k = k_p.reshape(BATCH, 1, SEQ, N_KV_HEADS, HEAD_DIM)
            new_v = v_p.reshape(BATCH, 1, SEQ, N_KV_HEADS, HEAD_DIM)
        else:
            acts = _rec_in(x, mod, norm_pre[i, 1], w_rec_in_b, (j,), lb_all[0, i], lb_all[1, i])
            o_p, s_p = _rec(acts, rec_norm[j], None, seq_len=SEQ, n_seq=BATCH, row_block_off=0)
            o_s = _rec(acts, rec_norm[j], state_s[:, j],
                       seq_len=DEC_SEQ, n_seq=DEC_BATCH, row_block_off=N_PROMPT // DEC_SEQ)
            x = _mix_out(o_p, o_s, x, mod, norm_post[i, 1], w_rec_out_b, (j,))
            new_s = s_p.reshape(BATCH, 1, 2, REC_HEADS, REC_DK, REC_DV)
        x = _ffn(x, mod, norm_pre[i, 2], norm_post[i, 2], w_ffn_in_b, w_ffn_out_b, (i, 1), 2,
                 split_out=(i == DEPTH - 1))
    y_prompt, y_sample = x
    return (y_prompt.reshape(BATCH, SEQ, D_MODEL), y_sample.reshape(DEC_BATCH, DEC_SEQ, D_MODEL),
            new_k, new_v, new_s)
```

```python
import functools

import jax
import jax.numpy as jnp
from jax import lax
from jax.experimental import pallas as pl
from jax.experimental.pallas import tpu as pltpu

F32 = jnp.float32
BF16 = jnp.bfloat16

D_MODEL = 1024
BATCH = 32
SEQ = 256
DEPTH = 2
DEC_BATCH = 2
DEC_SEQ = 1024
PAST_LEN = 256
GRID_W = 64
HEAD_DIM = 64
N_Q_HEADS = 16
N_KV_HEADS = 4
QKV_DIM = (N_Q_HEADS + 2 * N_KV_HEADS) * HEAD_DIM
ATTN_BLOCK = 128
ROPE_BASE = 10000.0
REC_HEADS = 8
REC_DK = 128
REC_DV = 128
REC_IN_DIM = 5 * D_MODEL
CHUNK = 64
D_FF = 2816
EPS = 1e-6
MASK_VALUE = -1e30

N_PROMPT = BATCH * SEQ
N_SAMPLE = DEC_BATCH * DEC_SEQ
N_TOK = N_PROMPT + N_SAMPLE
N_COND = 1 + DEC_BATCH

LANES = 128
SUBLANES = 8
VMEM_BYTES_V7X = 64 * 1024 * 1024

ROW_TILE = 512
FF_CHUNK = 256
ADA_TILE = 1024
REC_GROUP = 256
REC_STEP_ROWS_X_HEADS = 4096
REC_IN_COLS = 256
FFN_SUBTILES = 2
CAST_STEPS = 16

PROMPT_TILES = N_PROMPT // ROW_TILE
TILES_PER_SAMPLE = DEC_SEQ // ROW_TILE


def _vmem_limit(nbytes):
    return int(min(VMEM_BYTES_V7X - 8 * 1024 * 1024, max(nbytes, 16 * 1024 * 1024)))


def _sigmoid(x):
    return 1.0 / (1.0 + jnp.exp(-x))


def _rms(x, g):
    ms = jnp.mean(x * x, axis=-1, keepdims=True)
    return x * lax.rsqrt(ms + EPS) * g


def _mod_in(x, g_pre, mod_ref, slot):
    shift = mod_ref[slot * 3:slot * 3 + 1, :]
    scale = mod_ref[slot * 3 + 1:slot * 3 + 2, :]
    return _rms(x, g_pre * (1.0 + scale)) + shift


def _mod_out(x, y, g_post, mod_ref, slot, weight):
    gate = mod_ref[slot * 3 + 2:slot * 3 + 3, :]
    return x + _rms(y, (weight * gate) * g_post)


def _dot(a, b):
    return jnp.dot(a, b, preferred_element_type=F32)


def _dot_nt(a, b):
    return lax.dot_general(a, b, (((1,), (1,)), ((), ())), preferred_element_type=F32)


def _dot_tn(a, b):
    return lax.dot_general(a, b, (((0,), (0,)), ((), ())), preferred_element_type=F32)


def _tile_group(i):
    return jnp.where(i < PROMPT_TILES, 0, 1 + (i - PROMPT_TILES) // TILES_PER_SAMPLE)


def _row_spec(width):
    return pl.BlockSpec((ROW_TILE, width), lambda i: (i, 0))


def _prompt_row_spec(width):
    return pl.BlockSpec((ROW_TILE, width), lambda i: (jnp.minimum(i, PROMPT_TILES - 1), 0))


def _sample_row_spec(width):
    return pl.BlockSpec((ROW_TILE, width), lambda i: (jnp.maximum(i - PROMPT_TILES, 0), 0))


def _mod_spec():
    return pl.BlockSpec((None, 9, D_MODEL), lambda i: (_tile_group(i), 0, 0))


def _const_spec(shape, lead=()):
    nd = len(shape)
    return pl.BlockSpec((None,) * len(lead) + tuple(shape), lambda *_: tuple(lead) + (0,) * nd,
                        pipeline_mode=pl.Buffered(1))


def _ada_kernel(cond_ref, w_ref, b_ref, o_ref):
    c = cond_ref[...]
    s = (c * _sigmoid(c)).astype(BF16)
    o_ref[...] = _dot(s, w_ref[...].astype(BF16)) + b_ref[...]


def _ada_mod(cond, w_ada, b_ada):
    n_out = 9 * D_MODEL
    out = pl.pallas_call(
        _ada_kernel,
        grid=(DEPTH, n_out // ADA_TILE),
        in_specs=[
            pl.BlockSpec((SUBLANES, D_MODEL), lambda l, j: (0, 0)),
            pl.BlockSpec((None, D_MODEL, ADA_TILE), lambda l, j: (l, 0, j)),
            pl.BlockSpec((None, 1, ADA_TILE), lambda l, j: (l, 0, j)),
        ],
        out_specs=pl.BlockSpec((None, SUBLANES, ADA_TILE), lambda l, j: (l, 0, j)),
        out_shape=jax.ShapeDtypeStruct((DEPTH, SUBLANES, n_out), F32),
        compiler_params=pltpu.CompilerParams(
            dimension_semantics=("parallel", "parallel"),
            vmem_limit_bytes=_vmem_limit(4 * D_MODEL * ADA_TILE * 4)),
        name="ada_mod",
    )(cond, w_ada, b_ada.reshape(DEPTH, 1, n_out))
    return out[:, :N_COND].reshape(DEPTH, N_COND, 9, D_MODEL)


def _ffn_kernel(*refs, slot, split_in, split_out, n_casts):
    refs = list(refs)
    x_refs = [refs.pop(0) for _ in range(2 if split_in else 1)]
    mod_ref, gpre_ref, gpost_ref, win_ref, wout_ref = (refs.pop(0) for _ in range(5))
    cast_in = [refs.pop(0) for _ in range(n_casts)]
    o_refs = [refs.pop(0) for _ in range(2 if split_out else 1)]
    cast_out = [refs.pop(0) for _ in range(n_casts)]
    g_scr = refs.pop(0)
    is_prompt = pl.program_id(0) < PROMPT_TILES
    if split_in:
        x_scr = refs.pop(0)

        @pl.when(is_prompt)
        def _():
            x_scr[...] = x_refs[0][...]

        @pl.when(jnp.logical_not(is_prompt))
        def _():
            x_scr[...] = x_refs[1][...]

        x = x_scr[...]
    else:
        x = x_refs[0][...]
    parts = []
    sub = ROW_TILE // FFN_SUBTILES
    for s in range(FFN_SUBTILES):
        rows = slice(s * sub, (s + 1) * sub)
        xs = x[rows]
        h = _mod_in(xs, gpre_ref[...], mod_ref, slot).astype(BF16)
        for c in range(D_FF // FF_CHUNK):
            lo = c * FF_CHUNK
            a = _dot(h, win_ref[:, lo:lo + FF_CHUNK])
            b = _dot(h, win_ref[:, D_FF + lo:D_FF + lo + FF_CHUNK])
            g_scr[rows, lo:lo + FF_CHUNK] = (a * _sigmoid(a) * b).astype(BF16)
            if s == 0 and c < len(cast_in):
                cast_out[c][...] = cast_in[c][...].astype(BF16)
        y = _dot(g_scr[rows, :], wout_ref[...])
        parts.append(_mod_out(xs, y, gpost_ref[...], mod_ref, slot, 0.5))
    out = jnp.concatenate(parts, axis=0)
    if split_out:
        @pl.when(is_prompt)
        def _():
            o_refs[0][...] = out

        @pl.when(jnp.logical_not(is_prompt))
        def _():
            o_refs[1][...] = out
    else:
        o_refs[0][...] = out


def _ffn(x, mod, g_pre, g_post, w_in, w_out, slot, split_out=False, casts=()):
    split_in = isinstance(x, tuple)
    xs = x if split_in else (x,)
    vmem = (2 * D_MODEL * D_FF + D_FF * D_MODEL) * 2 + 4 * ROW_TILE * D_MODEL * 4 \
        + ROW_TILE * D_FF * 2 + 8 * ROW_TILE * FF_CHUNK * 4 + 6 * ROW_TILE * D_MODEL * 4
    pair_specs = [_prompt_row_spec(D_MODEL), _sample_row_spec(D_MODEL)]
    scratch = [pltpu.VMEM((ROW_TILE, D_FF), BF16)]
    if split_in:
        scratch.append(pltpu.VMEM((ROW_TILE, D_MODEL), F32))
    if split_out:
        out_specs = [_prompt_row_spec(D_MODEL), _sample_row_spec(D_MODEL)]
        out_shape = [jax.ShapeDtypeStruct((N_PROMPT, D_MODEL), F32),
                     jax.ShapeDtypeStruct((N_SAMPLE, D_MODEL), F32)]
    else:
        out_specs = [_row_spec(D_MODEL)]
        out_shape = [jax.ShapeDtypeStruct((N_TOK, D_MODEL), F32)]
    cast_in_specs = []
    for arr, lead in casts:
        rows, cols = arr.shape[-2:]
        blk = rows // CAST_STEPS
        assert blk * CAST_STEPS == rows and blk % 16 == 0 and CAST_STEPS <= N_TOK // ROW_TILE
        cast_in_specs.append(pl.BlockSpec(
            (None,) * len(lead) + (blk, cols),
            lambda i, lead=lead: tuple(lead) + (jnp.minimum(i, CAST_STEPS - 1), 0)))
        out_specs.append(pl.BlockSpec((blk, cols), lambda i: (jnp.minimum(i, CAST_STEPS - 1), 0)))
        out_shape.append(jax.ShapeDtypeStruct((rows, cols), BF16))
        vmem += 2 * blk * cols * (4 + 2)
    res = pl.pallas_call(
        functools.partial(_ffn_kernel, slot=slot, split_in=split_in, split_out=split_out, n_casts=len(casts)),
        grid=(N_TOK // ROW_TILE,),
        in_specs=(pair_specs if split_in else [_row_spec(D_MODEL)]) + [
            _mod_spec(),
            _const_spec((1, D_MODEL)),
            _const_spec((1, D_MODEL)),
            _const_spec((D_MODEL, 2 * D_FF)),
            _const_spec((D_FF, D_MODEL)),
        ] + cast_in_specs,
        out_specs=out_specs,
        out_shape=out_shape,
        scratch_shapes=scratch,
        compiler_params=pltpu.CompilerParams(
            dimension_semantics=("arbitrary",), vmem_limit_bytes=_vmem_limit(vmem)),
        name="ffn_sublayer",
    )(*xs, mod, g_pre.reshape(1, D_MODEL), g_post.reshape(1, D_MODEL), w_in, w_out, *[a for a, _ in casts])
    n_stream = 2 if split_out else 1
    stream = tuple(res[:n_stream]) if split_out else res[0]
    return stream, list(res[n_stream:])


def _qkv_kernel(x_ref, mod_ref, gpre_ref, w_ref, cos_ref, sin_ref, q_ref, kp_ref, vp_ref, k_ref, v_ref):
    i = pl.program_id(0)
    scale = HEAD_DIM ** -0.5
    h = _mod_in(x_ref[...], gpre_ref[...], mod_ref, 1).astype(BF16)
    qkv = _dot(h, w_ref[...])
    nq = N_Q_HEADS * HEAD_DIM
    nk = N_KV_HEADS * HEAD_DIM

    @pl.when(i < PROMPT_TILES)
    def _():
        q_ref[...] = (qkv[:, :nq] * scale).astype(BF16)
        kp_ref[...] = qkv[:, nq:nq + nk]
        vp_ref[...] = qkv[:, nq + nk:]

    @pl.when(i >= PROMPT_TILES)
    def _():
        v_ref[...] = qkv[:, nq + nk:]
        cos = cos_ref[...]
        sin = sin_ref[...]
        lane = lax.broadcasted_iota(jnp.int32, (ROW_TILE, LANES), 1)
        first = (lane & (HEAD_DIM // 4)) == 0

        def rope(xg):
            up = pltpu.roll(xg, LANES - HEAD_DIM // 4, 1)
            down = pltpu.roll(xg, HEAD_DIM // 4, 1)
            return xg * cos + jnp.where(first, up, down) * sin

        for j in range(nq // LANES):
            q_ref[:, j * LANES:(j + 1) * LANES] = (rope(qkv[:, j * LANES:(j + 1) * LANES]) * scale).astype(BF16)
        for j in range(nk // LANES):
            k_ref[:, j * LANES:(j + 1) * LANES] = rope(qkv[:, nq + j * LANES:nq + (j + 1) * LANES])


def _rope_tables():
    t = jnp.arange(DEC_SEQ)
    row = (t // GRID_W).astype(F32)
    col = (t % GRID_W).astype(F32)
    nf = HEAD_DIM // 4
    inv = ROPE_BASE ** (-jnp.arange(nf, dtype=F32) / nf)
    ar = row[:, None] * inv[None, :]
    ac = col[:, None] * inv[None, :]
    cos = jnp.concatenate([jnp.cos(ar), jnp.cos(ar), jnp.cos(ac), jnp.cos(ac)], axis=-1)
    sin = jnp.concatenate([-jnp.sin(ar), jnp.sin(ar), -jnp.sin(ac), jnp.sin(ac)], axis=-1)
    reps = LANES // HEAD_DIM
    return jnp.tile(cos, (1, reps)), jnp.tile(sin, (1, reps))


def _qkv(x, mod, g_pre, w_qkv, lead):
    cos, sin = _rope_tables()
    tab_spec = pl.BlockSpec(
        (ROW_TILE, LANES), lambda i: (jnp.maximum(i - PROMPT_TILES, 0) % TILES_PER_SAMPLE, 0))
    nk = N_KV_HEADS * HEAD_DIM
    vmem = D_MODEL * QKV_DIM * 2 + 6 * ROW_TILE * D_MODEL * 4 + 4 * ROW_TILE * QKV_DIM * 4
    return pl.pallas_call(
        _qkv_kernel,
        grid=(N_TOK // ROW_TILE,),
        in_specs=[
            _row_spec(D_MODEL),
            _mod_spec(),
            _const_spec((1, D_MODEL)),
            _const_spec((D_MODEL, QKV_DIM), lead),
            tab_spec,
            tab_spec,
        ],
        out_specs=[_row_spec(D_MODEL), _prompt_row_spec(nk), _prompt_row_spec(nk),
                   _sample_row_spec(nk), _sample_row_spec(nk)],
        out_shape=[
            jax.ShapeDtypeStruct((N_TOK, D_MODEL), BF16),
            jax.ShapeDtypeStruct((N_PROMPT, nk), F32),
            jax.ShapeDtypeStruct((N_PROMPT, nk), F32),
            jax.ShapeDtypeStruct((N_SAMPLE, nk), F32),
            jax.ShapeDtypeStruct((N_SAMPLE, nk), F32),
        ],
        compiler_params=pltpu.CompilerParams(
            dimension_semantics=("arbitrary",), vmem_limit_bytes=_vmem_limit(vmem)),
        name="attn_qkv",
    )(x, mod, g_pre.reshape(1, D_MODEL), w_qkv, cos, sin)


def _lane_halves(ref_or_val, hkv, rows):
    grp = ref_or_val[:, (hkv // 2) * LANES:(hkv // 2 + 1) * LANES]
    lane = lax.broadcasted_iota(jnp.int32, (rows, LANES), 1)
    in_low = lane < HEAD_DIM
    if hkv % 2 == 0:
        lo = jnp.where(in_low, grp, 0.0)
        hi = pltpu.roll(lo, HEAD_DIM, 1)
    else:
        hi = jnp.where(in_low, 0.0, grp)
        lo = pltpu.roll(hi, HEAD_DIM, 1)
    return lo.astype(BF16), hi.astype(BF16)


def _attend(sink_ref, q_ref, o_ref, keys, vals, n_keys, valid):
    group = N_Q_HEADS // N_KV_HEADS
    k_halves = [_lane_halves(keys, hkv, n_keys) for hkv in range(N_KV_HEADS)]
    v_halves = [_lane_halves(vals, hkv, n_keys) for hkv in range(N_KV_HEADS)]
    scores = []
    for h in range(N_Q_HEADS):
        s = _dot_nt(q_ref[:, (h // 2) * LANES:(h // 2 + 1) * LANES], k_halves[h // group][h % 2])
        scores.append(s if valid is None else jnp.where(valid, s, MASK_VALUE))
    heads = range(N_Q_HEADS)
    maxes = [jnp.maximum(jnp.max(scores[h], axis=-1, keepdims=True), sink_ref[h]) for h in heads]
    exps = [jnp.exp(scores[h] - maxes[h]) for h in heads]
    denoms = [jnp.sum(exps[h], axis=-1, keepdims=True) + jnp.exp(sink_ref[h] - maxes[h]) for h in heads]
    probs = [(exps[h] * (1.0 / denoms[h])).astype(BF16) for h in heads]
    for j in range(N_Q_HEADS // 2):
        v_lo, v_hi = v_halves[(2 * j) // group]
        o_ref[:, j * LANES:(j + 1) * LANES] = \
            (_dot(probs[2 * j], v_lo) + _dot(probs[2 * j + 1], v_hi)).astype(BF16)


def _attn_prompt_kernel(sink_ref, q_ref, k_ref, v_ref, o_ref):
    _attend(sink_ref, q_ref, o_ref, k_ref[...], v_ref[...], SEQ, None)


def _attn_prompt(sink, q, k, v):
    nk = N_KV_HEADS * HEAD_DIM
    return pl.pallas_call(
        _attn_prompt_kernel,
        grid=(BATCH,),
        in_specs=[
            pl.BlockSpec(memory_space=pltpu.SMEM),
            pl.BlockSpec((SEQ, D_MODEL), lambda b: (b, 0)),
            pl.BlockSpec((SEQ, nk), lambda b: (b, 0)),
            pl.BlockSpec((SEQ, nk), lambda b: (b, 0)),
        ],
        out_specs=pl.BlockSpec((SEQ, D_MODEL), lambda b: (b, 0)),
        out_shape=jax.ShapeDtypeStruct((N_PROMPT, D_MODEL), BF16),
        compiler_params=pltpu.CompilerParams(dimension_semantics=("parallel",)),
        name="attn_prompt",
    )(sink, q, k, v)


def _attn_sample_kernel(sink_ref, q_ref, k_ref, v_ref, ck_ref, cv_ref, o_ref):
    qb = pl.program_id(1)
    nblk = DEC_SEQ // ATTN_BLOCK
    B = ATTN_BLOCK
    starts = [
        pl.multiple_of(jnp.maximum(qb - 1, 0) * B, B),
        pl.multiple_of(qb * B, B),
        pl.multiple_of(jnp.minimum(qb + 1, nblk - 1) * B, B),
    ]
    keys = jnp.concatenate([k_ref[pl.ds(s, B), :] for s in starts] + [ck_ref[...]], axis=0)
    vals = jnp.concatenate([v_ref[pl.ds(s, B), :] for s in starts] + [cv_ref[...]], axis=0)
    n_keys = 3 * B + PAST_LEN
    r = lax.broadcasted_iota(jnp.int32, (B, n_keys), 0)
    c = lax.broadcasted_iota(jnp.int32, (B, n_keys), 1)
    prev_bad = (c < B) & ((c < r) | (qb == 0))
    next_bad = (c >= 2 * B) & (c < 3 * B) & (((c - 2 * B) > r) | (qb == nblk - 1))
    valid = jnp.logical_not(prev_bad | next_bad)
    _attend(sink_ref, q_ref, o_ref, keys, vals, n_keys, valid)


def _attn_sample(sink, q, k, v, cache_k, cache_v):
    nk = N_KV_HEADS * HEAD_DIM
    nblk = DEC_SEQ // ATTN_BLOCK
    q_off = N_PROMPT // ATTN_BLOCK
    return pl.pallas_call(
        _attn_sample_kernel,
        grid=(DEC_BATCH, nblk),
        in_specs=[
            pl.BlockSpec(memory_space=pltpu.SMEM),
            pl.BlockSpec((ATTN_BLOCK, D_MODEL), lambda b, t: (q_off + b * nblk + t, 0)),
            pl.BlockSpec((DEC_SEQ, nk), lambda b, t: (b, 0)),
            pl.BlockSpec((DEC_SEQ, nk), lambda b, t: (b, 0)),
            pl.BlockSpec((None, PAST_LEN, nk), lambda b, t: (b, 0, 0)),
            pl.BlockSpec((None, PAST_LEN, nk), lambda b, t: (b, 0, 0)),
        ],
        out_specs=pl.BlockSpec((ATTN_BLOCK, D_MODEL), lambda b, t: (b * nblk + t, 0)),
        out_shape=jax.ShapeDtypeStruct((N_SAMPLE, D_MODEL), BF16),
        compiler_params=pltpu.CompilerParams(dimension_semantics=("parallel", "parallel")),
        name="attn_sample",
    )(sink, q, k, v, cache_k, cache_v)


def _mix_out_kernel(op_ref, os_ref, x_ref, mod_ref, gpost_ref, w_ref, out_ref):
    is_prompt = pl.program_id(0) < PROMPT_TILES

    def finish(o_ref):
        y = _dot(o_ref[...], w_ref[...])
        out_ref[...] = _mod_out(x_ref[...], y, gpost_ref[...], mod_ref, 1, 1.0)

    @pl.when(is_prompt)
    def _():
        finish(op_ref)

    @pl.when(jnp.logical_not(is_prompt))
    def _():
        finish(os_ref)


def _mix_out(o_prompt, o_sample, x, mod, g_post, w, lead):
    vmem = D_MODEL * D_MODEL * 2 + 12 * ROW_TILE * D_MODEL * 4
    return pl.pallas_call(
        _mix_out_kernel,
        grid=(N_TOK // ROW_TILE,),
        in_specs=[
            _prompt_row_spec(D_MODEL),
            _sample_row_spec(D_MODEL),
            _row_spec(D_MODEL),
            _mod_spec(),
            _const_spec((1, D_MODEL)),
            _const_spec((D_MODEL, D_MODEL), lead),
        ],
        out_specs=_row_spec(D_MODEL),
        out_shape=jax.ShapeDtypeStruct((N_TOK, D_MODEL), F32),
        compiler_params=pltpu.CompilerParams(
            dimension_semantics=("parallel",), vmem_limit_bytes=_vmem_limit(vmem)),
        name="mixer_out",
    )(o_prompt, o_sample, x, mod, g_post.reshape(1, D_MODEL), w)


def _cumsum_rows(tri3, x):
    hi = x.astype(BF16)
    r1 = x - hi.astype(F32)
    mid = r1.astype(BF16)
    lo = (r1 - mid.astype(F32)).astype(BF16)
    return _dot(tri3, jnp.concatenate([hi, mid, lo], axis=0))


def _chunk_masks():
    r = lax.broadcasted_iota(jnp.int32, (REC_GROUP, REC_GROUP), 0)
    c = lax.broadcasted_iota(jnp.int32, (REC_GROUP, REC_GROUP), 1)
    same = (r // CHUNK) == (c // CHUNK)
    return same & (c <= r), same & (c >= r)


def _rec_in_kernel(x_ref, mod_ref, gpre_ref, w_ref, lbf_ref, lbb_ref,
                   qdf_ref, kif_ref, kef_ref, qdb_ref, kib_ref, keb_ref, v_ref, sg_ref, decf_ref, decb_ref):
    G = REC_GROUP
    CB = REC_IN_COLS
    h = _mod_in(x_ref[...], gpre_ref[...], mod_ref, 1).astype(BF16)
    tris = tuple(jnp.concatenate([jnp.where(m, 1.0, 0.0).astype(BF16)] * 3, axis=1) for m in _chunk_masks())
    outs = ((qdf_ref, kif_ref, kef_ref, decf_ref, lbf_ref), (qdb_ref, kib_ref, keb_ref, decb_ref, lbb_ref))

    def project(cb):
        return [_dot(h, w_ref[:, part * D_MODEL + cb * CB:part * D_MODEL + (cb + 1) * CB]) for part in range(5)]

    def finish(cb, proj):
        cols = slice(cb * CB, (cb + 1) * CB)
        yq, yv, yzf, yzb, yg = proj
        qf = yq * _sigmoid(yq) * (REC_DK ** -0.5)
        v_ref[:, cols] = yv.astype(BF16)
        sg_ref[:, cols] = yg * _sigmoid(yg)
        for d, z in enumerate((yzf, yzb)):
            qd_ref, ki_ref, ke_ref, dec_ref, lb_ref = outs[d]
            one_m_lb = 1.0 - lb_ref[:, cols]
            e = jnp.exp(-jnp.abs(z))
            rcp = 1.0 / (1.0 + e)
            key = one_m_lb * jnp.where(z >= 0, e * rcp, rcp)
            logf = jnp.log(1.0 - key)
            for g in range(ROW_TILE // G):
                rows = slice(g * G, (g + 1) * G)
                bc = _cumsum_rows(tris[d], logf[rows])
                decay = jnp.exp(bc)
                k_inv = key[rows] * (1.0 / decay)
                qd_ref[rows, cols] = (qf[rows] * decay).astype(BF16)
                ki_ref[rows, cols] = k_inv.astype(BF16)
                for ch in range(G // CHUNK):
                    rr = slice(ch * CHUNK, (ch + 1) * CHUNK)
                    edge = ch * CHUNK + (CHUNK - 1 if d == 0 else 0)
                    chunk_decay = decay[edge:edge + 1, :]
                    ke_ref[g * G + ch * CHUNK:g * G + (ch + 1) * CHUNK, cols] = \
                        (k_inv[rr] * chunk_decay).astype(BF16)
                    dec_ref[g, ch:ch + 1, cols] = chunk_decay

    n_blocks = D_MODEL // CB
    proj = project(0)
    for cb in range(n_blocks):
        nxt = project(cb + 1) if cb + 1 < n_blocks else None
        finish(cb, proj)
        proj = nxt


def _rec_in(x, mod, g_pre, w, lead, lb_f, lb_b):
    n_chunks = REC_GROUP // CHUNK
    groups_per_tile = ROW_TILE // REC_GROUP
    vmem = D_MODEL * REC_IN_DIM * 2 + 4 * ROW_TILE * D_MODEL * 4 + 2 * ROW_TILE * D_MODEL * (7 * 2 + 4) \
        + 12 * ROW_TILE * D_MODEL * 4
    act = jax.ShapeDtypeStruct((N_TOK, D_MODEL), BF16)
    dec = jax.ShapeDtypeStruct((N_TOK // REC_GROUP, n_chunks, D_MODEL), F32)
    dec_spec = pl.BlockSpec((groups_per_tile, n_chunks, D_MODEL), lambda i: (i, 0, 0))
    return pl.pallas_call(
        _rec_in_kernel,
        grid=(N_TOK // ROW_TILE,),
        in_specs=[
            _row_spec(D_MODEL),
            _mod_spec(),
            _const_spec((1, D_MODEL)),
            _const_spec((D_MODEL, REC_IN_DIM), lead),
            _const_spec((1, D_MODEL)),
            _const_spec((1, D_MODEL)),
        ],
        out_specs=[_row_spec(D_MODEL) for _ in range(8)] + [dec_spec, dec_spec],
        out_shape=[act] * 7 + [jax.ShapeDtypeStruct((N_TOK, D_MODEL), F32), dec, dec],
        compiler_params=pltpu.CompilerParams(
            dimension_semantics=("parallel",), vmem_limit_bytes=_vmem_limit(vmem)),
        name="rec_in",
    )(x, mod, g_pre.reshape(1, D_MODEL), w, lb_f.reshape(1, D_MODEL), lb_b.reshape(1, D_MODEL))


def _rec_kernel(*refs, seq_len, n_heads, has_init):
    qdf_ref, kif_ref, kef_ref, qdb_ref, kib_ref, keb_ref, v_ref, sg_ref, decf_ref, decb_ref, gn_ref = refs[:11]
    if has_init:
        s0_ref, o_ref = refs[11:]
        st_ref = None
    else:
        o_ref, st_ref = refs[11:]
        s0_ref = None
    G = REC_GROUP
    per_group = G // CHUNK
    n_chunks = seq_len // CHUNK
    mask_f, mask_b = _chunk_masks()

    def rows_of(c):
        return slice(c * CHUNK, (c + 1) * CHUNK)

    heads = range(n_heads)
    cols = [slice(hh * REC_DK, (hh + 1) * REC_DK) for hh in heads]
    groups = range(seq_len // G)

    def group_rows(g):
        return slice(g * G, (g + 1) * G)

    upd = [[_dot_tn(v_ref[rows_of(c), cs],
                    jnp.concatenate([kef_ref[rows_of(c), cs], keb_ref[rows_of(c), cs]], axis=1))
            for c in range(n_chunks)] for cs in cols]
    a = [[(jnp.where(mask_f, _dot_nt(qdf_ref[group_rows(g), cs], kif_ref[group_rows(g), cs]), 0.0)
           + jnp.where(mask_b, _dot_nt(qdb_ref[group_rows(g), cs], kib_ref[group_rows(g), cs]), 0.0)
           ).astype(BF16) for g in groups] for cs in cols]
    enter = []
    for hh in heads:
        cs = cols[hh]
        if has_init:
            s_f, s_b = s0_ref[0, hh].T, s0_ref[1, hh].T
        else:
            s_f = s_b = jnp.zeros((REC_DV, REC_DK), F32)
        ent = [None] * n_chunks
        for c in range(n_chunks):
            ent[c] = s_f.astype(BF16)
            s_f = s_f * decf_ref[c // per_group, c % per_group:c % per_group + 1, cs] + upd[hh][c][:, :REC_DK]
        for c in range(n_chunks - 1, -1, -1):
            ent[c] = jnp.concatenate([ent[c], s_b.astype(BF16)], axis=1)
            s_b = s_b * decb_ref[c // per_group, c % per_group:c % per_group + 1, cs] + upd[hh][c][:, REC_DK:]
        enter.append(ent)
        if st_ref is not None:
            st_ref[0, hh] = s_f.T
            st_ref[1, hh] = s_b.T
    o_intra = [[_dot(a[hh][g], v_ref[group_rows(g), cols[hh]]) for g in groups] for hh in heads]
    for hh in heads:
        cs = cols[hh]
        for c in range(n_chunks):
            q_cat = jnp.concatenate([qdf_ref[rows_of(c), cs], qdb_ref[rows_of(c), cs]], axis=1)
            o_c = o_intra[hh][c // per_group][rows_of(c % per_group)] + _dot_nt(q_cat, enter[hh][c])
            o_ref[rows_of(c), cs] = (_rms(o_c, gn_ref[hh]) * sg_ref[rows_of(c), cs]).astype(BF16)


def _rec(rec_acts, g_norm, s0, *, seq_len, n_seq, row_block_off):
    hb = min(REC_HEADS, REC_STEP_ROWS_X_HEADS // seq_len)
    w = hb * REC_DK
    n_hp = REC_HEADS // hb
    n_groups = seq_len // REC_GROUP
    has_init = s0 is not None

    def act_spec():
        return pl.BlockSpec((seq_len, w), lambda b, h: (row_block_off + b, h))

    def dec_spec():
        return pl.BlockSpec((n_groups, REC_GROUP // CHUNK, w), lambda b, h: (row_block_off + b, 0, h))

    in_specs = [act_spec() for _ in range(8)] + [dec_spec(), dec_spec(),
                                                 pl.BlockSpec((hb, 1, REC_DV), lambda b, h: (h, 0, 0))]
    args = list(rec_acts) + [g_norm.reshape(REC_HEADS, 1, REC_DV)]
    state_spec = pl.BlockSpec((None, 2, hb, REC_DK, REC_DV), lambda b, h: (b, 0, h, 0, 0))
    o_spec = pl.BlockSpec((seq_len, w), lambda b, h: (b, h))
    o_shape = jax.ShapeDtypeStruct((n_seq * seq_len, D_MODEL), BF16)
    if has_init:
        in_specs.append(state_spec)
        args.append(s0)
        out_specs, out_shape = o_spec, o_shape
    else:
        out_specs = [o_spec, state_spec]
        out_shape = [o_shape, jax.ShapeDtypeStruct((n_seq, 2, REC_HEADS, REC_DK, REC_DV), F32)]
    return pl.pallas_call(
        functools.partial(_rec_kernel, seq_len=seq_len, n_heads=hb, has_init=has_init),
        grid=(n_seq, n_hp),
        in_specs=in_specs,
        out_specs=out_specs,
        out_shape=out_shape,
        compiler_params=pltpu.CompilerParams(
            dimension_semantics=("parallel", "parallel"),
            vmem_limit_bytes=_vmem_limit(40 * seq_len * w * 4)),
        name="rec_scan_init" if has_init else "rec_scan",
    )(*args)


def kernel(x_prompt, x_sample, c, cache_k, cache_v, state_s, c_ctx, w_ada, b_ada, norm_pre, norm_post,
           w_ffn_in, w_ffn_out, w_qkv, w_attn_out, attn_sink, w_rec_in, rec_lb_logits, rec_norm, w_rec_out):
    x = (x_prompt.reshape(N_PROMPT, D_MODEL), x_sample.reshape(N_SAMPLE, D_MODEL))
    cond = jnp.concatenate([c_ctx[None], c, jnp.zeros((SUBLANES - N_COND, D_MODEL), F32)], axis=0)
    mods = _ada_mod(cond, w_ada, b_ada)

    lb_soft = jax.nn.softmax(rec_lb_logits.astype(F32), axis=1)
    lb_all = jnp.cumsum(lb_soft, axis=1) - lb_soft[:, :1]

    nk = N_KV_HEADS * HEAD_DIM
    ffn_w = (w_ffn_in[0, 0].astype(BF16), w_ffn_out[0, 0].astype(BF16))
    new_k = new_v = new_s = None
    for i in range(DEPTH):
        mod = mods[i]
        j = i // 2
        mixer_params = ((w_qkv, (j,)), (w_attn_out, (j,))) if i % 2 == 0 else ((w_rec_in, (j,)), (w_rec_out, (j,)))
        x, cast = _ffn(x, mod, norm_pre[i, 0], norm_post[i, 0], *ffn_w, 0,
                       casts=mixer_params + ((w_ffn_in, (i, 1)), (w_ffn_out, (i, 1))))
        w_mix_in, w_mix_out, ffn_w = cast[0], cast[1], tuple(cast[2:])
        if i % 2 == 0:
            q, k_p, v_p, k_s, v_s = _qkv(x, mod, norm_pre[i, 1], w_mix_in, ())
            o_p = _attn_prompt(attn_sink[j], q, k_p, v_p)
            o_s = _attn_sample(attn_sink[j], q, k_s, v_s,
                               cache_k[:, j].reshape(DEC_BATCH, PAST_LEN, nk),
                               cache_v[:, j].reshape(DEC_BATCH, PAST_LEN, nk))
            x = _mix_out(o_p, o_s, x, mod, norm_post[i, 1], w_mix_out, ())
            new_k = k_p.reshape(BATCH, 1, SEQ, N_KV_HEADS, HEAD_DIM)
            new_v = v_p.reshape(BATCH, 1, SEQ, N_KV_HEADS, HEAD_DIM)
        else:
            acts = _rec_in(x, mod, norm_pre[i, 1], w_mix_in, (), lb_all[0, i], lb_all[1, i])
            o_p, s_p = _rec(acts, rec_norm[j], None, seq_len=SEQ, n_seq=BATCH, row_block_off=0)
            o_s = _rec(acts, rec_norm[j], state_s[:, j],
                       seq_len=DEC_SEQ, n_seq=DEC_BATCH, row_block_off=N_PROMPT // DEC_SEQ)
            x = _mix_out(o_p, o_s, x, mod, norm_post[i, 1], w_mix_out, ())
            new_s = s_p.reshape(BATCH, 1, 2, REC_HEADS, REC_DK, REC_DV)
        last = i == DEPTH - 1
        x, cast = _ffn(x, mod, norm_pre[i, 2], norm_post[i, 2], *ffn_w, 2, split_out=last,
                       casts=() if last else ((w_ffn_in, (i + 1, 0)), (w_ffn_out, (i + 1, 0))))
        ffn_w = tuple(cast)
    y_prompt, y_sample = x
    return (y_prompt.reshape(BATCH, SEQ, D_MODEL), y_sample.reshape(DEC_BATCH, DEC_SEQ, D_MODEL),
            new_k, new_v, new_s)
```

```python
import functools

import jax
import jax.numpy as jnp
from jax import lax
from jax.experimental import pallas as pl
from jax.experimental.pallas import tpu as pltpu

F32 = jnp.float32
BF16 = jnp.bfloat16

D_MODEL = 1024
BATCH = 32
SEQ = 256
DEPTH = 2
DEC_BATCH = 2
DEC_SEQ = 1024
PAST_LEN = 256
GRID_W = 64
HEAD_DIM = 64
N_Q_HEADS = 16
N_KV_HEADS = 4
QKV_DIM = (N_Q_HEADS + 2 * N_KV_HEADS) * HEAD_DIM
ATTN_BLOCK = 128
ROPE_BASE = 10000.0
REC_HEADS = 8
REC_DK = 128
REC_DV = 128
REC_IN_DIM = 5 * D_MODEL
CHUNK = 64
D_FF = 2816
EPS = 1e-6
MASK_VALUE = -1e30

N_PROMPT = BATCH * SEQ
N_SAMPLE = DEC_BATCH * DEC_SEQ
N_TOK = N_PROMPT + N_SAMPLE
N_COND = 1 + DEC_BATCH

LANES = 128
SUBLANES = 8
VMEM_BYTES_V7X = 64 * 1024 * 1024

ROW_TILE = 512
FF_CHUNK = 256
ADA_TILE = 1024
REC_GROUP = 256
REC_STEP_ROWS_X_HEADS = 4096
REC_IN_COLS = 256
FFN_SUBTILES = 2
CAST_STEPS = 16

PROMPT_TILES = N_PROMPT // ROW_TILE
TILES_PER_SAMPLE = DEC_SEQ // ROW_TILE


def _vmem_limit(nbytes):
    return int(min(VMEM_BYTES_V7X - 8 * 1024 * 1024, max(nbytes, 16 * 1024 * 1024)))


def _sigmoid(x):
    return 1.0 / (1.0 + jnp.exp(-x))


def _rms(x, g):
    ms = jnp.mean(x * x, axis=-1, keepdims=True)
    return x * lax.rsqrt(ms + EPS) * g


def _mod_in(x, g_pre, mod_ref, slot):
    shift = mod_ref[slot * 3:slot * 3 + 1, :]
    scale = mod_ref[slot * 3 + 1:slot * 3 + 2, :]
    return _rms(x, g_pre * (1.0 + scale)) + shift


def _mod_out(x, y, g_post, mod_ref, slot, weight):
    gate = mod_ref[slot * 3 + 2:slot * 3 + 3, :]
    return x + _rms(y, (weight * gate) * g_post)


def _dot(a, b):
    return jnp.dot(a, b, preferred_element_type=F32)


def _dot_nt(a, b):
    return lax.dot_general(a, b, (((1,), (1,)), ((), ())), preferred_element_type=F32)


def _dot_tn(a, b):
    return lax.dot_general(a, b, (((0,), (0,)), ((), ())), preferred_element_type=F32)


def _tile_group(i):
    return jnp.where(i < PROMPT_TILES, 0, 1 + (i - PROMPT_TILES) // TILES_PER_SAMPLE)


def _row_spec(width):
    return pl.BlockSpec((ROW_TILE, width), lambda i: (i, 0))


def _prompt_row_spec(width):
    return pl.BlockSpec((ROW_TILE, width), lambda i: (jnp.minimum(i, PROMPT_TILES - 1), 0))


def _sample_row_spec(width):
    return pl.BlockSpec((ROW_TILE, width), lambda i: (jnp.maximum(i - PROMPT_TILES, 0), 0))


def _mod_spec():
    return pl.BlockSpec((None, 9, D_MODEL), lambda i: (_tile_group(i), 0, 0))


def _const_spec(shape, lead=()):
    nd = len(shape)
    return pl.BlockSpec((None,) * len(lead) + tuple(shape), lambda *_: tuple(lead) + (0,) * nd,
                        pipeline_mode=pl.Buffered(1))


def _ada_kernel(cond_ref, w_ref, b_ref, o_ref):
    c = cond_ref[...]
    s = (c * _sigmoid(c)).astype(BF16)
    o_ref[...] = _dot(s, w_ref[...].astype(BF16)) + b_ref[...]


def _ada_mod(cond, w_ada, b_ada):
    n_out = 9 * D_MODEL
    out = pl.pallas_call(
        _ada_kernel,
        grid=(DEPTH, n_out // ADA_TILE),
        in_specs=[
            pl.BlockSpec((SUBLANES, D_MODEL), lambda l, j: (0, 0)),
            pl.BlockSpec((None, D_MODEL, ADA_TILE), lambda l, j: (l, 0, j)),
            pl.BlockSpec((None, 1, ADA_TILE), lambda l, j: (l, 0, j)),
        ],
        out_specs=pl.BlockSpec((None, SUBLANES, ADA_TILE), lambda l, j: (l, 0, j)),
        out_shape=jax.ShapeDtypeStruct((DEPTH, SUBLANES, n_out), F32),
        compiler_params=pltpu.CompilerParams(
            dimension_semantics=("parallel", "parallel"),
            vmem_limit_bytes=_vmem_limit(4 * D_MODEL * ADA_TILE * 4)),
        name="ada_mod",
    )(cond, w_ada, b_ada.reshape(DEPTH, 1, n_out))
    return out[:, :N_COND].reshape(DEPTH, N_COND, 9, D_MODEL)


def _ffn_kernel(*refs, slot, split_in, split_out, n_casts):
    refs = list(refs)
    x_refs = [refs.pop(0) for _ in range(2 if split_in else 1)]
    mod_ref, gpre_ref, gpost_ref, win_ref, wout_ref = (refs.pop(0) for _ in range(5))
    cast_in = [refs.pop(0) for _ in range(n_casts)]
    o_refs = [refs.pop(0) for _ in range(2 if split_out else 1)]
    cast_out = [refs.pop(0) for _ in range(n_casts)]
    g_scr = refs.pop(0)
    is_prompt = pl.program_id(0) < PROMPT_TILES
    if split_in:
        x_scr = refs.pop(0)

        @pl.when(is_prompt)
        def _():
            x_scr[...] = x_refs[0][...]

        @pl.when(jnp.logical_not(is_prompt))
        def _():
            x_scr[...] = x_refs[1][...]

        x = x_scr[...]
    else:
        x = x_refs[0][...]
    parts = []
    sub = ROW_TILE // FFN_SUBTILES
    for s in range(FFN_SUBTILES):
        rows = slice(s * sub, (s + 1) * sub)
        xs = x[rows]
        h = _mod_in(xs, gpre_ref[...], mod_ref, slot).astype(BF16)
        for c in range(D_FF // FF_CHUNK):
            lo = c * FF_CHUNK
            a = _dot(h, win_ref[:, lo:lo + FF_CHUNK])
            b = _dot(h, win_ref[:, D_FF + lo:D_FF + lo + FF_CHUNK])
            g_scr[rows, lo:lo + FF_CHUNK] = (a * _sigmoid(a) * b).astype(BF16)
            if s == 0 and c < len(cast_in):
                cast_out[c][...] = cast_in[c][...].astype(BF16)
        y = _dot(g_scr[rows, :], wout_ref[...])
        parts.append(_mod_out(xs, y, gpost_ref[...], mod_ref, slot, 0.5))
    out = jnp.concatenate(parts, axis=0)
    if split_out:
        @pl.when(is_prompt)
        def _():
            o_refs[0][...] = out

        @pl.when(jnp.logical_not(is_prompt))
        def _():
            o_refs[1][...] = out
    else:
        o_refs[0][...] = out


def _ffn(x, mod, g_pre, g_post, w_in, w_out, slot, split_out=False, casts=()):
    split_in = isinstance(x, tuple)
    xs = x if split_in else (x,)
    vmem = (2 * D_MODEL * D_FF + D_FF * D_MODEL) * 2 + 4 * ROW_TILE * D_MODEL * 4 \
        + ROW_TILE * D_FF * 2 + 8 * ROW_TILE * FF_CHUNK * 4 + 6 * ROW_TILE * D_MODEL * 4
    pair_specs = [_prompt_row_spec(D_MODEL), _sample_row_spec(D_MODEL)]
    scratch = [pltpu.VMEM((ROW_TILE, D_FF), BF16)]
    if split_in:
        scratch.append(pltpu.VMEM((ROW_TILE, D_MODEL), F32))
    if split_out:
        out_specs = [_prompt_row_spec(D_MODEL), _sample_row_spec(D_MODEL)]
        out_shape = [jax.ShapeDtypeStruct((N_PROMPT, D_MODEL), F32),
                     jax.ShapeDtypeStruct((N_SAMPLE, D_MODEL), F32)]
    else:
        out_specs = [_row_spec(D_MODEL)]
        out_shape = [jax.ShapeDtypeStruct((N_TOK, D_MODEL), F32)]
    cast_in_specs = []
    for arr, lead in casts:
        rows, cols = arr.shape[-2:]
        blk = rows // CAST_STEPS
        assert blk * CAST_STEPS == rows and blk % 16 == 0 and CAST_STEPS <= N_TOK // ROW_TILE
        cast_in_specs.append(pl.BlockSpec(
            (None,) * len(lead) + (blk, cols),
            lambda i, lead=lead: tuple(lead) + (jnp.minimum(i, CAST_STEPS - 1), 0)))
        out_specs.append(pl.BlockSpec((blk, cols), lambda i: (jnp.minimum(i, CAST_STEPS - 1), 0)))
        out_shape.append(jax.ShapeDtypeStruct((rows, cols), BF16))
        vmem += 2 * blk * cols * (4 + 2)
    res = pl.pallas_call(
        functools.partial(_ffn_kernel, slot=slot, split_in=split_in, split_out=split_out, n_casts=len(casts)),
        grid=(N_TOK // ROW_TILE,),
        in_specs=(pair_specs if split_in else [_row_spec(D_MODEL)]) + [
            _mod_spec(),
            _const_spec((1, D_MODEL)),
            _const_spec((1, D_MODEL)),
            _const_spec((D_MODEL, 2 * D_FF)),
            _const_spec((D_FF, D_MODEL)),
        ] + cast_in_specs,
        out_specs=out_specs,
        out_shape=out_shape,
        scratch_shapes=scratch,
        compiler_params=pltpu.CompilerParams(
            dimension_semantics=("arbitrary",), vmem_limit_bytes=_vmem_limit(vmem)),
        name="ffn_sublayer",
    )(*xs, mod, g_pre.reshape(1, D_MODEL), g_post.reshape(1, D_MODEL), w_in, w_out, *[a for a, _ in casts])
    n_stream = 2 if split_out else 1
    stream = tuple(res[:n_stream]) if split_out else res[0]
    return stream, list(res[n_stream:])


def _qkv_kernel(x_ref, mod_ref, gpre_ref, w_ref, cos_ref, sin_ref, q_ref, kp_ref, vp_ref, k_ref, v_ref):
    i = pl.program_id(0)
    scale = HEAD_DIM ** -0.5
    h = _mod_in(x_ref[...], gpre_ref[...], mod_ref, 1).astype(BF16)
    qkv = _dot(h, w_ref[...])
    nq = N_Q_HEADS * HEAD_DIM
    nk = N_KV_HEADS * HEAD_DIM

    @pl.when(i < PROMPT_TILES)
    def _():
        q_ref[...] = (qkv[:, :nq] * scale).astype(BF16)
        kp_ref[...] = qkv[:, nq:nq + nk]
        vp_ref[...] = qkv[:, nq + nk:]

    @pl.when(i >= PROMPT_TILES)
    def _():
        v_ref[...] = qkv[:, nq + nk:]
        cos = cos_ref[...]
        sin = sin_ref[...]
        lane = lax.broadcasted_iota(jnp.int32, (ROW_TILE, LANES), 1)
        first = (lane & (HEAD_DIM // 4)) == 0

        def rope(xg):
            up = pltpu.roll(xg, LANES - HEAD_DIM // 4, 1)
            down = pltpu.roll(xg, HEAD_DIM // 4, 1)
            return xg * cos + jnp.where(first, up, down) * sin

        for j in range(nq // LANES):
            q_ref[:, j * LANES:(j + 1) * LANES] = (rope(qkv[:, j * LANES:(j + 1) * LANES]) * scale).astype(BF16)
        for j in range(nk // LANES):
            k_ref[:, j * LANES:(j + 1) * LANES] = rope(qkv[:, nq + j * LANES:nq + (j + 1) * LANES])


def _rope_tables():
    t = jnp.arange(DEC_SEQ)
    row = (t // GRID_W).astype(F32)
    col = (t % GRID_W).astype(F32)
    nf = HEAD_DIM // 4
    inv = ROPE_BASE ** (-jnp.arange(nf, dtype=F32) / nf)
    ar = row[:, None] * inv[None, :]
    ac = col[:, None] * inv[None, :]
    cos = jnp.concatenate([jnp.cos(ar), jnp.cos(ar), jnp.cos(ac), jnp.cos(ac)], axis=-1)
    sin = jnp.concatenate([-jnp.sin(ar), jnp.sin(ar), -jnp.sin(ac), jnp.sin(ac)], axis=-1)
    reps = LANES // HEAD_DIM
    return jnp.tile(cos, (1, reps)), jnp.tile(sin, (1, reps))


def _qkv(x, mod, g_pre, w_qkv, lead):
    cos, sin = _rope_tables()
    tab_spec = pl.BlockSpec(
        (ROW_TILE, LANES), lambda i: (jnp.maximum(i - PROMPT_TILES, 0) % TILES_PER_SAMPLE, 0))
    nk = N_KV_HEADS * HEAD_DIM
    vmem = D_MODEL * QKV_DIM * 2 + 6 * ROW_TILE * D_MODEL * 4 + 4 * ROW_TILE * QKV_DIM * 4
    return pl.pallas_call(
        _qkv_kernel,
        grid=(N_TOK // ROW_TILE,),
        in_specs=[
            _row_spec(D_MODEL),
            _mod_spec(),
            _const_spec((1, D_MODEL)),
            _const_spec((D_MODEL, QKV_DIM), lead),
            tab_spec,
            tab_spec,
        ],
        out_specs=[_row_spec(D_MODEL), _prompt_row_spec(nk), _prompt_row_spec(nk),
                   _sample_row_spec(nk), _sample_row_spec(nk)],
        out_shape=[
            jax.ShapeDtypeStruct((N_TOK, D_MODEL), BF16),
            jax.ShapeDtypeStruct((N_PROMPT, nk), F32),
            jax.ShapeDtypeStruct((N_PROMPT, nk), F32),
            jax.ShapeDtypeStruct((N_SAMPLE, nk), F32),
            jax.ShapeDtypeStruct((N_SAMPLE, nk), F32),
        ],
        compiler_params=pltpu.CompilerParams(
            dimension_semantics=("arbitrary",), vmem_limit_bytes=_vmem_limit(vmem)),
        name="attn_qkv",
    )(x, mod, g_pre.reshape(1, D_MODEL), w_qkv, cos, sin)


def _lane_halves(ref_or_val, hkv, rows):
    grp = ref_or_val[:, (hkv // 2) * LANES:(hkv // 2 + 1) * LANES]
    lane = lax.broadcasted_iota(jnp.int32, (rows, LANES), 1)
    in_low = lane < HEAD_DIM
    if hkv % 2 == 0:
        lo = jnp.where(in_low, grp, 0.0)
        hi = pltpu.roll(lo, HEAD_DIM, 1)
    else:
        hi = jnp.where(in_low, 0.0, grp)
        lo = pltpu.roll(hi, HEAD_DIM, 1)
    return lo.astype(BF16), hi.astype(BF16)


def _attend(sink_ref, q_ref, o_ref, keys, vals, n_keys, valid):
    group = N_Q_HEADS // N_KV_HEADS
    k_halves = [_lane_halves(keys, hkv, n_keys) for hkv in range(N_KV_HEADS)]
    v_halves = [_lane_halves(vals, hkv, n_keys) for hkv in range(N_KV_HEADS)]
    scores = []
    for h in range(N_Q_HEADS):
        s = _dot_nt(q_ref[:, (h // 2) * LANES:(h // 2 + 1) * LANES], k_halves[h // group][h % 2])
        scores.append(s if valid is None else jnp.where(valid, s, MASK_VALUE))
    heads = range(N_Q_HEADS)
    maxes = [jnp.maximum(jnp.max(scores[h], axis=-1, keepdims=True), sink_ref[h]) for h in heads]
    exps = [jnp.exp(scores[h] - maxes[h]) for h in heads]
    denoms = [jnp.sum(exps[h], axis=-1, keepdims=True) + jnp.exp(sink_ref[h] - maxes[h]) for h in heads]
    probs = [(exps[h] * (1.0 / denoms[h])).astype(BF16) for h in heads]
    for j in range(N_Q_HEADS // 2):
        v_lo, v_hi = v_halves[(2 * j) // group]
        o_ref[:, j * LANES:(j + 1) * LANES] = \
            (_dot(probs[2 * j], v_lo) + _dot(probs[2 * j + 1], v_hi)).astype(BF16)


def _attn_prompt_kernel(sink_ref, q_ref, k_ref, v_ref, o_ref):
    _attend(sink_ref, q_ref, o_ref, k_ref[...], v_ref[...], SEQ, None)


def _attn_prompt(sink, q, k, v):
    nk = N_KV_HEADS * HEAD_DIM
    return pl.pallas_call(
        _attn_prompt_kernel,
        grid=(BATCH,),
        in_specs=[
            pl.BlockSpec(memory_space=pltpu.SMEM),
            pl.BlockSpec((SEQ, D_MODEL), lambda b: (b, 0)),
            pl.BlockSpec((SEQ, nk), lambda b: (b, 0)),
            pl.BlockSpec((SEQ, nk), lambda b: (b, 0)),
        ],
        out_specs=pl.BlockSpec((SEQ, D_MODEL), lambda b: (b, 0)),
        out_shape=jax.ShapeDtypeStruct((N_PROMPT, D_MODEL), BF16),
        compiler_params=pltpu.CompilerParams(dimension_semantics=("parallel",)),
        name="attn_prompt",
    )(sink, q, k, v)


def _attn_sample_kernel(sink_ref, q_ref, k_ref, v_ref, ck_ref, cv_ref, o_ref):
    qb = pl.program_id(1)
    nblk = DEC_SEQ // ATTN_BLOCK
    B = ATTN_BLOCK
    starts = [
        pl.multiple_of(jnp.maximum(qb - 1, 0) * B, B),
        pl.multiple_of(qb * B, B),
        pl.multiple_of(jnp.minimum(qb + 1, nblk - 1) * B, B),
    ]
    keys = jnp.concatenate([k_ref[pl.ds(s, B), :] for s in starts] + [ck_ref[...]], axis=0)
    vals = jnp.concatenate([v_ref[pl.ds(s, B), :] for s in starts] + [cv_ref[...]], axis=0)
    n_keys = 3 * B + PAST_LEN
    r = lax.broadcasted_iota(jnp.int32, (B, n_keys), 0)
    c = lax.broadcasted_iota(jnp.int32, (B, n_keys), 1)
    prev_bad = (c < B) & ((c < r) | (qb == 0))
    next_bad = (c >= 2 * B) & (c < 3 * B) & (((c - 2 * B) > r) | (qb == nblk - 1))
    valid = jnp.logical_not(prev_bad | next_bad)
    _attend(sink_ref, q_ref, o_ref, keys, vals, n_keys, valid)


def _attn_sample(sink, q, k, v, cache_k, cache_v):
    nk = N_KV_HEADS * HEAD_DIM
    nblk = DEC_SEQ // ATTN_BLOCK
    q_off = N_PROMPT // ATTN_BLOCK
    return pl.pallas_call(
        _attn_sample_kernel,
        grid=(DEC_BATCH, nblk),
        in_specs=[
            pl.BlockSpec(memory_space=pltpu.SMEM),
            pl.BlockSpec((ATTN_BLOCK, D_MODEL), lambda b, t: (q_off + b * nblk + t, 0)),
            pl.BlockSpec((DEC_SEQ, nk), lambda b, t: (b, 0)),
            pl.BlockSpec((DEC_SEQ, nk), lambda b, t: (b, 0)),
            pl.BlockSpec((None, PAST_LEN, nk), lambda b, t: (b, 0, 0)),
            pl.BlockSpec((None, PAST_LEN, nk), lambda b, t: (b, 0, 0)),
        ],
        out_specs=pl.BlockSpec((ATTN_BLOCK, D_MODEL), lambda b, t: (b * nblk + t, 0)),
        out_shape=jax.ShapeDtypeStruct((N_SAMPLE, D_MODEL), BF16),
        compiler_params=pltpu.CompilerParams(dimension_semantics=("parallel", "parallel")),
        name="attn_sample",
    )(sink, q, k, v, cache_k, cache_v)


def _mix_out_kernel(op_ref, os_ref, x_ref, mod_ref, gpost_ref, w_ref, out_ref):
    is_prompt = pl.program_id(0) < PROMPT_TILES

    def finish(o_ref):
        y = _dot(o_ref[...], w_ref[...])
        out_ref[...] = _mod_out(x_ref[...], y, gpost_ref[...], mod_ref, 1, 1.0)

    @pl.when(is_prompt)
    def _():
        finish(op_ref)

    @pl.when(jnp.logical_not(is_prompt))
    def _():
        finish(os_ref)


def _mix_out(o_prompt, o_sample, x, mod, g_post, w, lead):
    vmem = D_MODEL * D_MODEL * 2 + 12 * ROW_TILE * D_MODEL * 4
    return pl.pallas_call(
        _mix_out_kernel,
        grid=(N_TOK // ROW_TILE,),
        in_specs=[
            _prompt_row_spec(D_MODEL),
            _sample_row_spec(D_MODEL),
            _row_spec(D_MODEL),
            _mod_spec(),
            _const_spec((1, D_MODEL)),
            _const_spec((D_MODEL, D_MODEL), lead),
        ],
        out_specs=_row_spec(D_MODEL),
        out_shape=jax.ShapeDtypeStruct((N_TOK, D_MODEL), F32),
        compiler_params=pltpu.CompilerParams(
            dimension_semantics=("parallel",), vmem_limit_bytes=_vmem_limit(vmem)),
        name="mixer_out",
    )(o_prompt, o_sample, x, mod, g_post.reshape(1, D_MODEL), w)


def _chunk_masks():
    r = lax.broadcasted_iota(jnp.int32, (REC_GROUP, REC_GROUP), 0)
    c = lax.broadcasted_iota(jnp.int32, (REC_GROUP, REC_GROUP), 1)
    same = (r // CHUNK) == (c // CHUNK)
    return same & (c <= r), same & (c >= r)


def _rec_in_kernel(x_ref, mod_ref, gpre_ref, w_ref, lbf_ref, lbb_ref,
                   qdf_ref, kif_ref, kef_ref, qdb_ref, kib_ref, keb_ref, v_ref, sg_ref, decf_ref, decb_ref):
    G = REC_GROUP
    CB = REC_IN_COLS
    h = _mod_in(x_ref[...], gpre_ref[...], mod_ref, 1).astype(BF16)
    ri = lax.broadcasted_iota(jnp.int32, (CHUNK, CHUNK), 0)
    ci = lax.broadcasted_iota(jnp.int32, (CHUNK, CHUNK), 1)
    tris = tuple(jnp.concatenate([jnp.where(m, 1.0, 0.0).astype(BF16)] * 3, axis=1) for m in (ci <= ri, ci >= ri))
    outs = ((qdf_ref, kif_ref, kef_ref, decf_ref, lbf_ref), (qdb_ref, kib_ref, keb_ref, decb_ref, lbb_ref))

    def project(cb, part):
        return _dot(h, w_ref[:, part * D_MODEL + cb * CB:part * D_MODEL + (cb + 1) * CB])

    def finish_qvg(cb, yq, yv, yg):
        cols = slice(cb * CB, (cb + 1) * CB)
        v_ref[:, cols] = yv.astype(BF16)
        sg_ref[:, cols] = yg * _sigmoid(yg)
        return yq * _sigmoid(yq) * (REC_DK ** -0.5)

    def finish_dir(cb, d, z, qf):
        cols = slice(cb * CB, (cb + 1) * CB)
        qd_ref, ki_ref, ke_ref, dec_ref, lb_ref = outs[d]
        one_m_lb = 1.0 - lb_ref[:, cols]
        e = jnp.exp(-jnp.abs(z))
        rcp = 1.0 / (1.0 + e)
        key = one_m_lb * jnp.where(z >= 0, e * rcp, rcp)
        logf = jnp.log(1.0 - key)
        hi = logf.astype(BF16)
        rest = logf - hi.astype(F32)
        mid = rest.astype(BF16)
        lo = (rest - mid.astype(F32)).astype(BF16)
        edge = CHUNK - 1 if d == 0 else 0
        for c in range(ROW_TILE // CHUNK):
            rows = slice(c * CHUNK, (c + 1) * CHUNK)
            bc = _dot(tris[d], jnp.concatenate([hi[rows], mid[rows], lo[rows]], axis=0))
            decay = jnp.exp(bc)
            k_inv = key[rows] * (1.0 / decay)
            chunk_decay = decay[edge:edge + 1, :]
            qd_ref[rows, cols] = (qf[rows] * decay).astype(BF16)
            ki_ref[rows, cols] = k_inv.astype(BF16)
            ke_ref[rows, cols] = (k_inv * chunk_decay).astype(BF16)
            dec_ref[c // (G // CHUNK), c % (G // CHUNK):c % (G // CHUNK) + 1, cols] = chunk_decay

    n_blocks = D_MODEL // CB
    cur = [project(0, part) for part in range(5)]
    for cb in range(n_blocks):
        more = cb + 1 < n_blocks
        nxt = [None] * 5
        if more:
            nxt[0], nxt[1] = project(cb + 1, 0), project(cb + 1, 1)
        qf = finish_qvg(cb, cur[0], cur[1], cur[4])
        if more:
            nxt[2], nxt[3] = project(cb + 1, 2), project(cb + 1, 3)
        finish_dir(cb, 0, cur[2], qf)
        if more:
            nxt[4] = project(cb + 1, 4)
        finish_dir(cb, 1, cur[3], qf)
        cur = nxt


def _rec_in(x, mod, g_pre, w, lead, lb_f, lb_b):
    n_chunks = REC_GROUP // CHUNK
    groups_per_tile = ROW_TILE // REC_GROUP
    vmem = D_MODEL * REC_IN_DIM * 2 + 4 * ROW_TILE * D_MODEL * 4 + 2 * ROW_TILE * D_MODEL * (7 * 2 + 4) \
        + 12 * ROW_TILE * D_MODEL * 4
    act = jax.ShapeDtypeStruct((N_TOK, D_MODEL), BF16)
    dec = jax.ShapeDtypeStruct((N_TOK // REC_GROUP, n_chunks, D_MODEL), F32)
    dec_spec = pl.BlockSpec((groups_per_tile, n_chunks, D_MODEL), lambda i: (i, 0, 0))
    return pl.pallas_call(
        _rec_in_kernel,
        grid=(N_TOK // ROW_TILE,),
        in_specs=[
            _row_spec(D_MODEL),
            _mod_spec(),
            _const_spec((1, D_MODEL)),
            _const_spec((D_MODEL, REC_IN_DIM), lead),
            _const_spec((1, D_MODEL)),
            _const_spec((1, D_MODEL)),
        ],
        out_specs=[_row_spec(D_MODEL) for _ in range(8)] + [dec_spec, dec_spec],
        out_shape=[act] * 7 + [jax.ShapeDtypeStruct((N_TOK, D_MODEL), F32), dec, dec],
        compiler_params=pltpu.CompilerParams(
            dimension_semantics=("parallel",), vmem_limit_bytes=_vmem_limit(vmem)),
        name="rec_in",
    )(x, mod, g_pre.reshape(1, D_MODEL), w, lb_f.reshape(1, D_MODEL), lb_b.reshape(1, D_MODEL))


def _rec_kernel(*refs, seq_len, n_heads, has_init):
    qdf_ref, kif_ref, kef_ref, qdb_ref, kib_ref, keb_ref, v_ref, sg_ref, decf_ref, decb_ref, gn_ref = refs[:11]
    if has_init:
        s0_ref, o_ref = refs[11:]
        st_ref = None
    else:
        o_ref, st_ref = refs[11:]
        s0_ref = None
    G = REC_GROUP
    per_group = G // CHUNK
    n_chunks = seq_len // CHUNK
    mask_f, mask_b = _chunk_masks()

    def rows_of(c):
        return slice(c * CHUNK, (c + 1) * CHUNK)

    heads = range(n_heads)
    cols = [slice(hh * REC_DK, (hh + 1) * REC_DK) for hh in heads]
    groups = range(seq_len // G)

    def group_rows(g):
        return slice(g * G, (g + 1) * G)

    upd = [[_dot_tn(v_ref[rows_of(c), cs],
                    jnp.concatenate([kef_ref[rows_of(c), cs], keb_ref[rows_of(c), cs]], axis=1))
            for c in range(n_chunks)] for cs in cols]
    a = [[(jnp.where(mask_f, _dot_nt(qdf_ref[group_rows(g), cs], kif_ref[group_rows(g), cs]), 0.0)
           + jnp.where(mask_b, _dot_nt(qdb_ref[group_rows(g), cs], kib_ref[group_rows(g), cs]), 0.0)
           ).astype(BF16) for g in groups] for cs in cols]
    enter = []
    for hh in heads:
        cs = cols[hh]
        if has_init:
            s_f, s_b = s0_ref[0, hh].T, s0_ref[1, hh].T
        else:
            s_f = s_b = jnp.zeros((REC_DV, REC_DK), F32)
        ent = [None] * n_chunks
        for c in range(n_chunks):
            ent[c] = s_f.astype(BF16)
            s_f = s_f * decf_ref[c // per_group, c % per_group:c % per_group + 1, cs] + upd[hh][c][:, :REC_DK]
        for c in range(n_chunks - 1, -1, -1):
            ent[c] = jnp.concatenate([ent[c], s_b.astype(BF16)], axis=1)
            s_b = s_b * decb_ref[c // per_group, c % per_group:c % per_group + 1, cs] + upd[hh][c][:, REC_DK:]
        enter.append(ent)
        if st_ref is not None:
            st_ref[0, hh] = s_f.T
            st_ref[1, hh] = s_b.T
    o_intra = [[_dot(a[hh][g], v_ref[group_rows(g), cols[hh]]) for g in groups] for hh in heads]
    for hh in heads:
        cs = cols[hh]
        for c in range(n_chunks):
            q_cat = jnp.concatenate([qdf_ref[rows_of(c), cs], qdb_ref[rows_of(c), cs]], axis=1)
            o_c = o_intra[hh][c // per_group][rows_of(c % per_group)] + _dot_nt(q_cat, enter[hh][c])
            o_ref[rows_of(c), cs] = (_rms(o_c, gn_ref[hh]) * sg_ref[rows_of(c), cs]).astype(BF16)


def _rec(rec_acts, g_norm, s0, *, seq_len, n_seq, row_block_off):
    hb = min(REC_HEADS, REC_STEP_ROWS_X_HEADS // seq_len)
    w = hb * REC_DK
    n_hp = REC_HEADS // hb
    n_groups = seq_len // REC_GROUP
    has_init = s0 is not None

    def act_spec():
        return pl.BlockSpec((seq_len, w), lambda b, h: (row_block_off + b, h))

    def dec_spec():
        return pl.BlockSpec((n_groups, REC_GROUP // CHUNK, w), lambda b, h: (row_block_off + b, 0, h))

    in_specs = [act_spec() for _ in range(8)] + [dec_spec(), dec_spec(),
                                                 pl.BlockSpec((hb, 1, REC_DV), lambda b, h: (h, 0, 0))]
    args = list(rec_acts) + [g_norm.reshape(REC_HEADS, 1, REC_DV)]
    state_spec = pl.BlockSpec((None, 2, hb, REC_DK, REC_DV), lambda b, h: (b, 0, h, 0, 0))
    o_spec = pl.BlockSpec((seq_len, w), lambda b, h: (b, h))
    o_shape = jax.ShapeDtypeStruct((n_seq * seq_len, D_MODEL), BF16)
    if has_init:
        in_specs.append(state_spec)
        args.append(s0)
        out_specs, out_shape = o_spec, o_shape
    else:
        out_specs = [o_spec, state_spec]
        out_shape = [o_shape, jax.ShapeDtypeStruct((n_seq, 2, REC_HEADS, REC_DK, REC_DV), F32)]
    return pl.pallas_call(
        functools.partial(_rec_kernel, seq_len=seq_len, n_heads=hb, has_init=has_init),
        grid=(n_seq, n_hp),
        in_specs=in_specs,
        out_specs=out_specs,
        out_shape=out_shape,
        compiler_params=pltpu.CompilerParams(
            dimension_semantics=("parallel", "parallel"),
            vmem_limit_bytes=_vmem_limit(40 * seq_len * w * 4)),
        name="rec_scan_init" if has_init else "rec_scan",
    )(*args)


def kernel(x_prompt, x_sample, c, cache_k, cache_v, state_s, c_ctx, w_ada, b_ada, norm_pre, norm_post,
           w_ffn_in, w_ffn_out, w_qkv, w_attn_out, attn_sink, w_rec_in, rec_lb_logits, rec_norm, w_rec_out):
    x = (x_prompt.reshape(N_PROMPT, D_MODEL), x_sample.reshape(N_SAMPLE, D_MODEL))
    cond = jnp.concatenate([c_ctx[None], c, jnp.zeros((SUBLANES - N_COND, D_MODEL), F32)], axis=0)
    mods = _ada_mod(cond, w_ada, b_ada)

    lb_soft = jax.nn.softmax(rec_lb_logits.astype(F32), axis=1)
    lb_all = jnp.cumsum(lb_soft, axis=1) - lb_soft[:, :1]

    nk = N_KV_HEADS * HEAD_DIM
    ffn_w = (w_ffn_in[0, 0].astype(BF16), w_ffn_out[0, 0].astype(BF16))
    new_k = new_v = new_s = None
    for i in range(DEPTH):
        mod = mods[i]
        j = i // 2
        mixer_params = ((w_qkv, (j,)), (w_attn_out, (j,))) if i % 2 == 0 else ((w_rec_in, (j,)), (w_rec_out, (j,)))
        x, cast = _ffn(x, mod, norm_pre[i, 0], norm_post[i, 0], *ffn_w, 0,
                       casts=mixer_params + ((w_ffn_in, (i, 1)), (w_ffn_out, (i, 1))))
        w_mix_in, w_mix_out, ffn_w = cast[0], cast[1], tuple(cast[2:])
        if i % 2 == 0:
            q, k_p, v_p, k_s, v_s = _qkv(x, mod, norm_pre[i, 1], w_mix_in, ())
            o_p = _attn_prompt(attn_sink[j], q, k_p, v_p)
            o_s = _attn_sample(attn_sink[j], q, k_s, v_s,
                               cache_k[:, j].reshape(DEC_BATCH, PAST_LEN, nk),
                               cache_v[:, j].reshape(DEC_BATCH, PAST_LEN, nk))
            x = _mix_out(o_p, o_s, x, mod, norm_post[i, 1], w_mix_out, ())
            new_k = k_p.reshape(BATCH, 1, SEQ, N_KV_HEADS, HEAD_DIM)
            new_v = v_p.reshape(BATCH, 1, SEQ, N_KV_HEADS, HEAD_DIM)
        else:
            acts = _rec_in(x, mod, norm_pre[i, 1], w_mix_in, (), lb_all[0, i], lb_all[1, i])
            o_p, s_p = _rec(acts, rec_norm[j], None, seq_len=SEQ, n_seq=BATCH, row_block_off=0)
            o_s = _rec(acts, rec_norm[j], state_s[:, j],
                       seq_len=DEC_SEQ, n_seq=DEC_BATCH, row_block_off=N_PROMPT // DEC_SEQ)
            x = _mix_out(o_p, o_s, x, mod, norm_post[i, 1], w_mix_out, ())
            new_s = s_p.reshape(BATCH, 1, 2, REC_HEADS, REC_DK, REC_DV)
        last = i == DEPTH - 1
        x, cast = _ffn(x, mod, norm_pre[i, 2], norm_post[i, 2], *ffn_w, 2, split_out=last,
                       casts=() if last else ((w_ffn_in, (i + 1, 0)), (w_ffn_out, (i + 1, 0))))
        ffn_w = tuple(cast)
    y_prompt, y_sample = x
    return (y_prompt.reshape(BATCH, SEQ, D_MODEL), y_sample.reshape(DEC_BATCH, DEC_SEQ, D_MODEL),
            new_k, new_v, new_s)
```

```python
import functools

import jax
import jax.numpy as jnp
from jax import lax
from jax.experimental import pallas as pl
from jax.experimental.pallas import tpu as pltpu

F32 = jnp.float32
BF16 = jnp.bfloat16

D_MODEL = 1024
BATCH = 32
SEQ = 256
DEPTH = 2
DEC_BATCH = 2
DEC_SEQ = 1024
PAST_LEN = 256
GRID_W = 64
HEAD_DIM = 64
N_Q_HEADS = 16
N_KV_HEADS = 4
QKV_DIM = (N_Q_HEADS + 2 * N_KV_HEADS) * HEAD_DIM
ATTN_BLOCK = 128
ROPE_BASE = 10000.0
REC_HEADS = 8
REC_DK = 128
REC_DV = 128
REC_IN_DIM = 5 * D_MODEL
CHUNK = 64
D_FF = 2816
EPS = 1e-6
MASK_VALUE = -1e30

N_PROMPT = BATCH * SEQ
N_SAMPLE = DEC_BATCH * DEC_SEQ
N_TOK = N_PROMPT + N_SAMPLE
N_COND = 1 + DEC_BATCH

LANES = 128
SUBLANES = 8
VMEM_BYTES_V7X = 64 * 1024 * 1024

ROW_TILE = 512
FF_CHUNK = 256
ADA_TILE = 1024
REC_GROUP = 256
REC_STEP_ROWS_X_HEADS = 4096
REC_IN_COLS = 256
FFN_SUBTILES = 2
CAST_STEPS = 16

PROMPT_TILES = N_PROMPT // ROW_TILE
TILES_PER_SAMPLE = DEC_SEQ // ROW_TILE


def _vmem_limit(nbytes):
    return int(min(VMEM_BYTES_V7X - 8 * 1024 * 1024, max(nbytes, 16 * 1024 * 1024)))


def _sigmoid(x):
    return 1.0 / (1.0 + jnp.exp(-x))


def _rms(x, g):
    ms = jnp.mean(x * x, axis=-1, keepdims=True)
    return x * lax.rsqrt(ms + EPS) * g


def _mod_in(x, g_pre, mod_ref, slot):
    shift = mod_ref[slot * 3:slot * 3 + 1, :]
    scale = mod_ref[slot * 3 + 1:slot * 3 + 2, :]
    return _rms(x, g_pre * (1.0 + scale)) + shift


def _mod_out(x, y, g_post, mod_ref, slot, weight):
    gate = mod_ref[slot * 3 + 2:slot * 3 + 3, :]
    return x + _rms(y, (weight * gate) * g_post)


def _dot(a, b):
    return jnp.dot(a, b, preferred_element_type=F32)


def _dot_nt(a, b):
    return lax.dot_general(a, b, (((1,), (1,)), ((), ())), preferred_element_type=F32)


def _dot_tn(a, b):
    return lax.dot_general(a, b, (((0,), (0,)), ((), ())), preferred_element_type=F32)


def _tile_group(i):
    return jnp.where(i < PROMPT_TILES, 0, 1 + (i - PROMPT_TILES) // TILES_PER_SAMPLE)


def _row_spec(width):
    return pl.BlockSpec((ROW_TILE, width), lambda i: (i, 0))


def _prompt_row_spec(width):
    return pl.BlockSpec((ROW_TILE, width), lambda i: (jnp.minimum(i, PROMPT_TILES - 1), 0))


def _sample_row_spec(width):
    return pl.BlockSpec((ROW_TILE, width), lambda i: (jnp.maximum(i - PROMPT_TILES, 0), 0))


def _mod_spec():
    return pl.BlockSpec((None, 9, D_MODEL), lambda i: (_tile_group(i), 0, 0))


def _const_spec(shape, lead=()):
    nd = len(shape)
    return pl.BlockSpec((None,) * len(lead) + tuple(shape), lambda *_: tuple(lead) + (0,) * nd,
                        pipeline_mode=pl.Buffered(1))


def _ada_kernel(cond_ref, w_ref, b_ref, o_ref):
    c = cond_ref[...]
    s = (c * _sigmoid(c)).astype(BF16)
    o_ref[...] = _dot(s, w_ref[...].astype(BF16)) + b_ref[...]


def _ada_mod(cond, w_ada, b_ada):
    n_out = 9 * D_MODEL
    out = pl.pallas_call(
        _ada_kernel,
        grid=(DEPTH, n_out // ADA_TILE),
        in_specs=[
            pl.BlockSpec((SUBLANES, D_MODEL), lambda l, j: (0, 0)),
            pl.BlockSpec((None, D_MODEL, ADA_TILE), lambda l, j: (l, 0, j)),
            pl.BlockSpec((None, 1, ADA_TILE), lambda l, j: (l, 0, j)),
        ],
        out_specs=pl.BlockSpec((None, SUBLANES, ADA_TILE), lambda l, j: (l, 0, j)),
        out_shape=jax.ShapeDtypeStruct((DEPTH, SUBLANES, n_out), F32),
        compiler_params=pltpu.CompilerParams(
            dimension_semantics=("parallel", "parallel"),
            vmem_limit_bytes=_vmem_limit(4 * D_MODEL * ADA_TILE * 4)),
        name="ada_mod",
    )(cond, w_ada, b_ada.reshape(DEPTH, 1, n_out))
    return out[:, :N_COND].reshape(DEPTH, N_COND, 9, D_MODEL)


def _ffn_kernel(*refs, slot, split_in, split_out, n_casts):
    refs = list(refs)
    x_refs = [refs.pop(0) for _ in range(2 if split_in else 1)]
    mod_ref, gpre_ref, gpost_ref, win_ref, wout_ref = (refs.pop(0) for _ in range(5))
    cast_in = [refs.pop(0) for _ in range(n_casts)]
    o_refs = [refs.pop(0) for _ in range(2 if split_out else 1)]
    cast_out = [refs.pop(0) for _ in range(n_casts)]
    g_scr = refs.pop(0)
    is_prompt = pl.program_id(0) < PROMPT_TILES
    if split_in:
        x_scr = refs.pop(0)

        @pl.when(is_prompt)
        def _():
            x_scr[...] = x_refs[0][...]

        @pl.when(jnp.logical_not(is_prompt))
        def _():
            x_scr[...] = x_refs[1][...]

        x = x_scr[...]
    else:
        x = x_refs[0][...]
    parts = []
    sub = ROW_TILE // FFN_SUBTILES
    for s in range(FFN_SUBTILES):
        rows = slice(s * sub, (s + 1) * sub)
        xs = x[rows]
        h = _mod_in(xs, gpre_ref[...], mod_ref, slot).astype(BF16)
        for c in range(D_FF // FF_CHUNK):
            lo = c * FF_CHUNK
            a = _dot(h, win_ref[:, lo:lo + FF_CHUNK])
            b = _dot(h, win_ref[:, D_FF + lo:D_FF + lo + FF_CHUNK])
            g_scr[rows, lo:lo + FF_CHUNK] = (a * _sigmoid(a) * b).astype(BF16)
            if s == 0 and c < len(cast_in):
                cast_out[c][...] = cast_in[c][...].astype(BF16)
        y = _dot(g_scr[rows, :], wout_ref[...])
        parts.append(_mod_out(xs, y, gpost_ref[...], mod_ref, slot, 0.5))
    out = jnp.concatenate(parts, axis=0)
    if split_out:
        @pl.when(is_prompt)
        def _():
            o_refs[0][...] = out

        @pl.when(jnp.logical_not(is_prompt))
        def _():
            o_refs[1][...] = out
    else:
        o_refs[0][...] = out


def _ffn(x, mod, g_pre, g_post, w_in, w_out, slot, split_out=False, casts=()):
    split_in = isinstance(x, tuple)
    xs = x if split_in else (x,)
    vmem = (2 * D_MODEL * D_FF + D_FF * D_MODEL) * 2 + 4 * ROW_TILE * D_MODEL * 4 \
        + ROW_TILE * D_FF * 2 + 8 * ROW_TILE * FF_CHUNK * 4 + 6 * ROW_TILE * D_MODEL * 4
    pair_specs = [_prompt_row_spec(D_MODEL), _sample_row_spec(D_MODEL)]
    scratch = [pltpu.VMEM((ROW_TILE, D_FF), BF16)]
    if split_in:
        scratch.append(pltpu.VMEM((ROW_TILE, D_MODEL), F32))
    if split_out:
        out_specs = [_prompt_row_spec(D_MODEL), _sample_row_spec(D_MODEL)]
        out_shape = [jax.ShapeDtypeStruct((N_PROMPT, D_MODEL), F32),
                     jax.ShapeDtypeStruct((N_SAMPLE, D_MODEL), F32)]
    else:
        out_specs = [_row_spec(D_MODEL)]
        out_shape = [jax.ShapeDtypeStruct((N_TOK, D_MODEL), F32)]
    cast_in_specs = []
    for arr, lead in casts:
        rows, cols = arr.shape[-2:]
        blk = rows // CAST_STEPS
        assert blk * CAST_STEPS == rows and blk % 16 == 0 and CAST_STEPS <= N_TOK // ROW_TILE
        cast_in_specs.append(pl.BlockSpec(
            (None,) * len(lead) + (blk, cols),
            lambda i, lead=lead: tuple(lead) + (jnp.minimum(i, CAST_STEPS - 1), 0)))
        out_specs.append(pl.BlockSpec((blk, cols), lambda i: (jnp.minimum(i, CAST_STEPS - 1), 0)))
        out_shape.append(jax.ShapeDtypeStruct((rows, cols), BF16))
        vmem += 2 * blk * cols * (4 + 2)
    res = pl.pallas_call(
        functools.partial(_ffn_kernel, slot=slot, split_in=split_in, split_out=split_out, n_casts=len(casts)),
        grid=(N_TOK // ROW_TILE,),
        in_specs=(pair_specs if split_in else [_row_spec(D_MODEL)]) + [
            _mod_spec(),
            _const_spec((1, D_MODEL)),
            _const_spec((1, D_MODEL)),
            _const_spec((D_MODEL, 2 * D_FF)),
            _const_spec((D_FF, D_MODEL)),
        ] + cast_in_specs,
        out_specs=out_specs,
        out_shape=out_shape,
        scratch_shapes=scratch,
        compiler_params=pltpu.CompilerParams(
            dimension_semantics=("arbitrary",), vmem_limit_bytes=_vmem_limit(vmem)),
        name="ffn_sublayer",
    )(*xs, mod, g_pre.reshape(1, D_MODEL), g_post.reshape(1, D_MODEL), w_in, w_out, *[a for a, _ in casts])
    n_stream = 2 if split_out else 1
    stream = tuple(res[:n_stream]) if split_out else res[0]
    return stream, list(res[n_stream:])


def _qkv_kernel(x_ref, mod_ref, gpre_ref, w_ref, cos_ref, sin_ref, q_ref, kp_ref, vp_ref, k_ref, v_ref):
    i = pl.program_id(0)
    scale = HEAD_DIM ** -0.5
    h = _mod_in(x_ref[...], gpre_ref[...], mod_ref, 1).astype(BF16)
    qkv = _dot(h, w_ref[...])
    nq = N_Q_HEADS * HEAD_DIM
    nk = N_KV_HEADS * HEAD_DIM

    @pl.when(i < PROMPT_TILES)
    def _():
        q_ref[...] = (qkv[:, :nq] * scale).astype(BF16)
        kp_ref[...] = qkv[:, nq:nq + nk]
        vp_ref[...] = qkv[:, nq + nk:]

    @pl.when(i >= PROMPT_TILES)
    def _():
        v_ref[...] = qkv[:, nq + nk:]
        cos = cos_ref[...]
        sin = sin_ref[...]
        lane = lax.broadcasted_iota(jnp.int32, (ROW_TILE, LANES), 1)
        first = (lane & (HEAD_DIM // 4)) == 0

        def rope(xg):
            up = pltpu.roll(xg, LANES - HEAD_DIM // 4, 1)
            down = pltpu.roll(xg, HEAD_DIM // 4, 1)
            return xg * cos + jnp.where(first, up, down) * sin

        for j in range(nq // LANES):
            q_ref[:, j * LANES:(j + 1) * LANES] = (rope(qkv[:, j * LANES:(j + 1) * LANES]) * scale).astype(BF16)
        for j in range(nk // LANES):
            k_ref[:, j * LANES:(j + 1) * LANES] = rope(qkv[:, nq + j * LANES:nq + (j + 1) * LANES])


def _rope_tables():
    t = jnp.arange(DEC_SEQ)
    row = (t // GRID_W).astype(F32)
    col = (t % GRID_W).astype(F32)
    nf = HEAD_DIM // 4
    inv = ROPE_BASE ** (-jnp.arange(nf, dtype=F32) / nf)
    ar = row[:, None] * inv[None, :]
    ac = col[:, None] * inv[None, :]
    cos = jnp.concatenate([jnp.cos(ar), jnp.cos(ar), jnp.cos(ac), jnp.cos(ac)], axis=-1)
    sin = jnp.concatenate([-jnp.sin(ar), jnp.sin(ar), -jnp.sin(ac), jnp.sin(ac)], axis=-1)
    reps = LANES // HEAD_DIM
    return jnp.tile(cos, (1, reps)), jnp.tile(sin, (1, reps))


def _qkv(x, mod, g_pre, w_qkv, lead):
    cos, sin = _rope_tables()
    tab_spec = pl.BlockSpec(
        (ROW_TILE, LANES), lambda i: (jnp.maximum(i - PROMPT_TILES, 0) % TILES_PER_SAMPLE, 0))
    nk = N_KV_HEADS * HEAD_DIM
    vmem = D_MODEL * QKV_DIM * 2 + 6 * ROW_TILE * D_MODEL * 4 + 4 * ROW_TILE * QKV_DIM * 4
    return pl.pallas_call(
        _qkv_kernel,
        grid=(N_TOK // ROW_TILE,),
        in_specs=[
            _row_spec(D_MODEL),
            _mod_spec(),
            _const_spec((1, D_MODEL)),
            _const_spec((D_MODEL, QKV_DIM), lead),
            tab_spec,
            tab_spec,
        ],
        out_specs=[_row_spec(D_MODEL), _prompt_row_spec(nk), _prompt_row_spec(nk),
                   _sample_row_spec(nk), _sample_row_spec(nk)],
        out_shape=[
            jax.ShapeDtypeStruct((N_TOK, D_MODEL), BF16),
            jax.ShapeDtypeStruct((N_PROMPT, nk), F32),
            jax.ShapeDtypeStruct((N_PROMPT, nk), F32),
            jax.ShapeDtypeStruct((N_SAMPLE, nk), F32),
            jax.ShapeDtypeStruct((N_SAMPLE, nk), F32),
        ],
        compiler_params=pltpu.CompilerParams(
            dimension_semantics=("arbitrary",), vmem_limit_bytes=_vmem_limit(vmem)),
        name="attn_qkv",
    )(x, mod, g_pre.reshape(1, D_MODEL), w_qkv, cos, sin)


def _lane_halves(ref_or_val, hkv, rows):
    grp = ref_or_val[:, (hkv // 2) * LANES:(hkv // 2 + 1) * LANES]
    lane = lax.broadcasted_iota(jnp.int32, (rows, LANES), 1)
    in_low = lane < HEAD_DIM
    if hkv % 2 == 0:
        lo = jnp.where(in_low, grp, 0.0)
        hi = pltpu.roll(lo, HEAD_DIM, 1)
    else:
        hi = jnp.where(in_low, 0.0, grp)
        lo = pltpu.roll(hi, HEAD_DIM, 1)
    return lo.astype(BF16), hi.astype(BF16)


def _attend(sink_ref, q_ref, keys, vals, n_keys, valid):
    group = N_Q_HEADS // N_KV_HEADS
    k_halves = [_lane_halves(keys, hkv, n_keys) for hkv in range(N_KV_HEADS)]
    v_halves = [_lane_halves(vals, hkv, n_keys) for hkv in range(N_KV_HEADS)]
    scores = []
    for h in range(N_Q_HEADS):
        s = _dot_nt(q_ref[:, (h // 2) * LANES:(h // 2 + 1) * LANES], k_halves[h // group][h % 2])
        scores.append(s if valid is None else jnp.where(valid, s, MASK_VALUE))
    heads = range(N_Q_HEADS)
    maxes = [jnp.maximum(jnp.max(scores[h], axis=-1, keepdims=True), sink_ref[h]) for h in heads]
    exps = [jnp.exp(scores[h] - maxes[h]) for h in heads]
    denoms = [jnp.sum(exps[h], axis=-1, keepdims=True) + jnp.exp(sink_ref[h] - maxes[h]) for h in heads]
    probs = [(exps[h] * (1.0 / denoms[h])).astype(BF16) for h in heads]
    out = []
    for j in range(N_Q_HEADS // 2):
        v_lo, v_hi = v_halves[(2 * j) // group]
        out.append((_dot(probs[2 * j], v_lo) + _dot(probs[2 * j + 1], v_hi)).astype(BF16))
    return jnp.concatenate(out, axis=1)


def _mixer_finish(o, x_ref, mod_ref, gpost_ref, w_ref, out_ref):
    out_ref[...] = _mod_out(x_ref[...], _dot(o, w_ref[...]), gpost_ref[...], mod_ref, 1, 1.0)


def _mixer_specs(rows, x_index, group_index):
    return [
        pl.BlockSpec((rows, D_MODEL), x_index),
        pl.BlockSpec((None, 9, D_MODEL), group_index),
        _const_spec((1, D_MODEL)),
        _const_spec((D_MODEL, D_MODEL)),
    ]


def _attn_prompt_kernel(sink_ref, q_ref, k_ref, v_ref, x_ref, mod_ref, gpost_ref, w_ref, out_ref):
    o = _attend(sink_ref, q_ref, k_ref[...], v_ref[...], SEQ, None)
    _mixer_finish(o, x_ref, mod_ref, gpost_ref, w_ref, out_ref)


def _attn_prompt(sink, q, k, v, x, mod, g_post, w_out):
    nk = N_KV_HEADS * HEAD_DIM
    return pl.pallas_call(
        _attn_prompt_kernel,
        grid=(BATCH,),
        in_specs=[
            pl.BlockSpec(memory_space=pltpu.SMEM),
            pl.BlockSpec((SEQ, D_MODEL), lambda b: (b, 0)),
            pl.BlockSpec((SEQ, nk), lambda b: (b, 0)),
            pl.BlockSpec((SEQ, nk), lambda b: (b, 0)),
        ] + _mixer_specs(SEQ, lambda b: (b, 0), lambda b: (0, 0, 0)),
        out_specs=pl.BlockSpec((SEQ, D_MODEL), lambda b: (b, 0)),
        out_shape=jax.ShapeDtypeStruct((N_PROMPT, D_MODEL), F32),
        compiler_params=pltpu.CompilerParams(
            dimension_semantics=("parallel",), vmem_limit_bytes=_vmem_limit(32 * 1024 * 1024)),
        name="attn_prompt",
    )(sink, q, k, v, x, mod, g_post.reshape(1, D_MODEL), w_out)


def _attn_sample_kernel(sink_ref, q_ref, k_ref, v_ref, ck_ref, cv_ref,
                        x_ref, mod_ref, gpost_ref, w_ref, out_ref):
    qb = pl.program_id(1)
    nblk = DEC_SEQ // ATTN_BLOCK
    B = ATTN_BLOCK
    starts = [
        pl.multiple_of(jnp.maximum(qb - 1, 0) * B, B),
        pl.multiple_of(qb * B, B),
        pl.multiple_of(jnp.minimum(qb + 1, nblk - 1) * B, B),
    ]
    keys = jnp.concatenate([k_ref[pl.ds(s, B), :] for s in starts] + [ck_ref[...]], axis=0)
    vals = jnp.concatenate([v_ref[pl.ds(s, B), :] for s in starts] + [cv_ref[...]], axis=0)
    n_keys = 3 * B + PAST_LEN
    r = lax.broadcasted_iota(jnp.int32, (B, n_keys), 0)
    c = lax.broadcasted_iota(jnp.int32, (B, n_keys), 1)
    prev_bad = (c < B) & ((c < r) | (qb == 0))
    next_bad = (c >= 2 * B) & (c < 3 * B) & (((c - 2 * B) > r) | (qb == nblk - 1))
    valid = jnp.logical_not(prev_bad | next_bad)
    o = _attend(sink_ref, q_ref, keys, vals, n_keys, valid)
    _mixer_finish(o, x_ref, mod_ref, gpost_ref, w_ref, out_ref)


def _attn_sample(sink, q, k, v, cache_k, cache_v, x, mod, g_post, w_out):
    nk = N_KV_HEADS * HEAD_DIM
    nblk = DEC_SEQ // ATTN_BLOCK
    q_off = N_PROMPT // ATTN_BLOCK
    return pl.pallas_call(
        _attn_sample_kernel,
        grid=(DEC_BATCH, nblk),
        in_specs=[
            pl.BlockSpec(memory_space=pltpu.SMEM),
            pl.BlockSpec((ATTN_BLOCK, D_MODEL), lambda b, t: (q_off + b * nblk + t, 0)),
            pl.BlockSpec((DEC_SEQ, nk), lambda b, t: (b, 0)),
            pl.BlockSpec((DEC_SEQ, nk), lambda b, t: (b, 0)),
            pl.BlockSpec((None, PAST_LEN, nk), lambda b, t: (b, 0, 0)),
            pl.BlockSpec((None, PAST_LEN, nk), lambda b, t: (b, 0, 0)),
        ] + _mixer_specs(ATTN_BLOCK, lambda b, t: (q_off + b * nblk + t, 0), lambda b, t: (1 + b, 0, 0)),
        out_specs=pl.BlockSpec((ATTN_BLOCK, D_MODEL), lambda b, t: (b * nblk + t, 0)),
        out_shape=jax.ShapeDtypeStruct((N_SAMPLE, D_MODEL), F32),
        compiler_params=pltpu.CompilerParams(
            dimension_semantics=("parallel", "parallel"), vmem_limit_bytes=_vmem_limit(32 * 1024 * 1024)),
        name="attn_sample",
    )(sink, q, k, v, cache_k, cache_v, x, mod, g_post.reshape(1, D_MODEL), w_out)


def _mix_out_sample_kernel(o_ref, x_ref, mod_ref, gpost_ref, w_ref, out_ref):
    _mixer_finish(o_ref[...], x_ref, mod_ref, gpost_ref, w_ref, out_ref)


def _mix_out_sample(o_sample, x, mod, g_post, w):
    vmem = D_MODEL * D_MODEL * 2 + 12 * ROW_TILE * D_MODEL * 4
    return pl.pallas_call(
        _mix_out_sample_kernel,
        grid=(N_SAMPLE // ROW_TILE,),
        in_specs=[pl.BlockSpec((ROW_TILE, D_MODEL), lambda i: (i, 0))] + _mixer_specs(
            ROW_TILE, lambda i: (PROMPT_TILES + i, 0), lambda i: (1 + i // TILES_PER_SAMPLE, 0, 0)),
        out_specs=pl.BlockSpec((ROW_TILE, D_MODEL), lambda i: (i, 0)),
        out_shape=jax.ShapeDtypeStruct((N_SAMPLE, D_MODEL), F32),
        compiler_params=pltpu.CompilerParams(
            dimension_semantics=("parallel",), vmem_limit_bytes=_vmem_limit(vmem)),
        name="mixer_out_sample",
    )(o_sample, x, mod, g_post.reshape(1, D_MODEL), w)


def _chunk_masks():
    r = lax.broadcasted_iota(jnp.int32, (REC_GROUP, REC_GROUP), 0)
    c = lax.broadcasted_iota(jnp.int32, (REC_GROUP, REC_GROUP), 1)
    same = (r // CHUNK) == (c // CHUNK)
    return same & (c <= r), same & (c >= r)


def _rec_in_kernel(x_ref, mod_ref, gpre_ref, w_ref, lbf_ref, lbb_ref,
                   qdf_ref, kif_ref, kef_ref, qdb_ref, kib_ref, keb_ref, v_ref, sg_ref, decf_ref, decb_ref):
    G = REC_GROUP
    CB = REC_IN_COLS
    h = _mod_in(x_ref[...], gpre_ref[...], mod_ref, 1).astype(BF16)
    ri = lax.broadcasted_iota(jnp.int32, (CHUNK, CHUNK), 0)
    ci = lax.broadcasted_iota(jnp.int32, (CHUNK, CHUNK), 1)
    tris = tuple(jnp.concatenate([jnp.where(m, 1.0, 0.0).astype(BF16)] * 3, axis=1) for m in (ci <= ri, ci >= ri))
    outs = ((qdf_ref, kif_ref, kef_ref, decf_ref, lbf_ref), (qdb_ref, kib_ref, keb_ref, decb_ref, lbb_ref))

    def project(cb, part):
        return _dot(h, w_ref[:, part * D_MODEL + cb * CB:part * D_MODEL + (cb + 1) * CB])

    def finish_qvg(cb, yq, yv, yg):
        cols = slice(cb * CB, (cb + 1) * CB)
        v_ref[:, cols] = yv.astype(BF16)
        sg_ref[:, cols] = yg * _sigmoid(yg)
        return yq * _sigmoid(yq) * (REC_DK ** -0.5)

    def finish_dir(cb, d, z, qf):
        cols = slice(cb * CB, (cb + 1) * CB)
        qd_ref, ki_ref, ke_ref, dec_ref, lb_ref = outs[d]
        one_m_lb = 1.0 - lb_ref[:, cols]
        e = jnp.exp(-jnp.abs(z))
        rcp = 1.0 / (1.0 + e)
        key = one_m_lb * jnp.where(z >= 0, e * rcp, rcp)
        logf = jnp.log(1.0 - key)
        hi = logf.astype(BF16)
        rest = logf - hi.astype(F32)
        mid = rest.astype(BF16)
        lo = (rest - mid.astype(F32)).astype(BF16)
        edge = CHUNK - 1 if d == 0 else 0
        for c in range(ROW_TILE // CHUNK):
            rows = slice(c * CHUNK, (c + 1) * CHUNK)
            bc = _dot(tris[d], jnp.concatenate([hi[rows], mid[rows], lo[rows]], axis=0))
            decay = jnp.exp(bc)
            k_inv = key[rows] * (1.0 / decay)
            chunk_decay = decay[edge:edge + 1, :]
            qd_ref[rows, cols] = (qf[rows] * decay).astype(BF16)
            ki_ref[rows, cols] = k_inv.astype(BF16)
            ke_ref[rows, cols] = (k_inv * chunk_decay).astype(BF16)
            dec_ref[c // (G // CHUNK), c % (G // CHUNK):c % (G // CHUNK) + 1, cols] = chunk_decay

    n_blocks = D_MODEL // CB
    cur = [project(0, part) for part in range(5)]
    for cb in range(n_blocks):
        more = cb + 1 < n_blocks
        nxt = [None] * 5
        if more:
            nxt[0], nxt[1] = project(cb + 1, 0), project(cb + 1, 1)
        qf = finish_qvg(cb, cur[0], cur[1], cur[4])
        if more:
            nxt[2], nxt[3] = project(cb + 1, 2), project(cb + 1, 3)
        finish_dir(cb, 0, cur[2], qf)
        if more:
            nxt[4] = project(cb + 1, 4)
        finish_dir(cb, 1, cur[3], qf)
        cur = nxt


def _rec_in(x, mod, g_pre, w, lead, lb_f, lb_b):
    n_chunks = REC_GROUP // CHUNK
    groups_per_tile = ROW_TILE // REC_GROUP
    vmem = D_MODEL * REC_IN_DIM * 2 + 4 * ROW_TILE * D_MODEL * 4 + 2 * ROW_TILE * D_MODEL * (7 * 2 + 4) \
        + 12 * ROW_TILE * D_MODEL * 4
    act = jax.ShapeDtypeStruct((N_TOK, D_MODEL), BF16)
    dec = jax.ShapeDtypeStruct((N_TOK // REC_GROUP, n_chunks, D_MODEL), F32)
    dec_spec = pl.BlockSpec((groups_per_tile, n_chunks, D_MODEL), lambda i: (i, 0, 0))
    return pl.pallas_call(
        _rec_in_kernel,
        grid=(N_TOK // ROW_TILE,),
        in_specs=[
            _row_spec(D_MODEL),
            _mod_spec(),
            _const_spec((1, D_MODEL)),
            _const_spec((D_MODEL, REC_IN_DIM), lead),
            _const_spec((1, D_MODEL)),
            _const_spec((1, D_MODEL)),
        ],
        out_specs=[_row_spec(D_MODEL) for _ in range(8)] + [dec_spec, dec_spec],
        out_shape=[act] * 7 + [jax.ShapeDtypeStruct((N_TOK, D_MODEL), F32), dec, dec],
        compiler_params=pltpu.CompilerParams(
            dimension_semantics=("parallel",), vmem_limit_bytes=_vmem_limit(vmem)),
        name="rec_in",
    )(x, mod, g_pre.reshape(1, D_MODEL), w, lb_f.reshape(1, D_MODEL), lb_b.reshape(1, D_MODEL))


def _rec_kernel(*refs, seq_len, n_heads, has_init):
    qdf_ref, kif_ref, kef_ref, qdb_ref, kib_ref, keb_ref, v_ref, sg_ref, decf_ref, decb_ref, gn_ref = refs[:11]
    if has_init:
        s0_ref, o_ref = refs[11:]
        st_ref = mixer_refs = None
    else:
        *mixer_refs, out_ref, st_ref, o_ref = refs[11:]
        s0_ref = None
    G = REC_GROUP
    per_group = G // CHUNK
    n_chunks = seq_len // CHUNK
    mask_f, mask_b = _chunk_masks()

    def rows_of(c):
        return slice(c * CHUNK, (c + 1) * CHUNK)

    heads = range(n_heads)
    cols = [slice(hh * REC_DK, (hh + 1) * REC_DK) for hh in heads]
    groups = range(seq_len // G)

    def group_rows(g):
        return slice(g * G, (g + 1) * G)

    upd = [[_dot_tn(v_ref[rows_of(c), cs],
                    jnp.concatenate([kef_ref[rows_of(c), cs], keb_ref[rows_of(c), cs]], axis=1))
            for c in range(n_chunks)] for cs in cols]
    a = [[(jnp.where(mask_f, _dot_nt(qdf_ref[group_rows(g), cs], kif_ref[group_rows(g), cs]), 0.0)
           + jnp.where(mask_b, _dot_nt(qdb_ref[group_rows(g), cs], kib_ref[group_rows(g), cs]), 0.0)
           ).astype(BF16) for g in groups] for cs in cols]
    enter = []
    for hh in heads:
        cs = cols[hh]
        if has_init:
            s_f, s_b = s0_ref[0, hh].T, s0_ref[1, hh].T
        else:
            s_f = s_b = jnp.zeros((REC_DV, REC_DK), F32)
        ent = [None] * n_chunks
        for c in range(n_chunks):
            ent[c] = s_f.astype(BF16)
            s_f = s_f * decf_ref[c // per_group, c % per_group:c % per_group + 1, cs] + upd[hh][c][:, :REC_DK]
        for c in range(n_chunks - 1, -1, -1):
            ent[c] = jnp.concatenate([ent[c], s_b.astype(BF16)], axis=1)
            s_b = s_b * decb_ref[c // per_group, c % per_group:c % per_group + 1, cs] + upd[hh][c][:, REC_DK:]
        enter.append(ent)
        if st_ref is not None:
            st_ref[0, hh] = s_f.T
            st_ref[1, hh] = s_b.T
    o_intra = [[_dot(a[hh][g], v_ref[group_rows(g), cols[hh]]) for g in groups] for hh in heads]
    for hh in heads:
        cs = cols[hh]
        for c in range(n_chunks):
            q_cat = jnp.concatenate([qdf_ref[rows_of(c), cs], qdb_ref[rows_of(c), cs]], axis=1)
            o_c = o_intra[hh][c // per_group][rows_of(c % per_group)] + _dot_nt(q_cat, enter[hh][c])
            o_ref[rows_of(c), cs] = (_rms(o_c, gn_ref[hh]) * sg_ref[rows_of(c), cs]).astype(BF16)
    if mixer_refs is not None:
        _mixer_finish(o_ref[...], *mixer_refs, out_ref)


def _rec(rec_acts, g_norm, s0, mixer, *, seq_len, n_seq, row_block_off):
    hb = min(REC_HEADS, REC_STEP_ROWS_X_HEADS // seq_len)
    w = hb * REC_DK
    n_hp = REC_HEADS // hb
    n_groups = seq_len // REC_GROUP
    has_init = s0 is not None

    def act_spec():
        return pl.BlockSpec((seq_len, w), lambda b, h: (row_block_off + b, h))

    def dec_spec():
        return pl.BlockSpec((n_groups, REC_GROUP // CHUNK, w), lambda b, h: (row_block_off + b, 0, h))

    in_specs = [act_spec() for _ in range(8)] + [dec_spec(), dec_spec(),
                                                 pl.BlockSpec((hb, 1, REC_DV), lambda b, h: (h, 0, 0))]
    args = list(rec_acts) + [g_norm.reshape(REC_HEADS, 1, REC_DV)]
    state_spec = pl.BlockSpec((None, 2, hb, REC_DK, REC_DV), lambda b, h: (b, 0, h, 0, 0))
    scratch = []
    if has_init:
        in_specs.append(state_spec)
        args.append(s0)
        out_specs = pl.BlockSpec((seq_len, w), lambda b, h: (b, h))
        out_shape = jax.ShapeDtypeStruct((n_seq * seq_len, D_MODEL), BF16)
    else:
        assert n_hp == 1
        x, mod, g_post, w_out = mixer
        in_specs += _mixer_specs(seq_len, lambda b, h: (row_block_off + b, 0), lambda b, h: (0, 0, 0))
        args += [x, mod, g_post.reshape(1, D_MODEL), w_out]
        out_specs = [pl.BlockSpec((seq_len, D_MODEL), lambda b, h: (b, 0)), state_spec]
        out_shape = [jax.ShapeDtypeStruct((n_seq * seq_len, D_MODEL), F32),
                     jax.ShapeDtypeStruct((n_seq, 2, REC_HEADS, REC_DK, REC_DV), F32)]
        scratch = [pltpu.VMEM((seq_len, D_MODEL), BF16)]
    return pl.pallas_call(
        functools.partial(_rec_kernel, seq_len=seq_len, n_heads=hb, has_init=has_init),
        grid=(n_seq, n_hp),
        in_specs=in_specs,
        out_specs=out_specs,
        out_shape=out_shape,
        scratch_shapes=scratch,
        compiler_params=pltpu.CompilerParams(
            dimension_semantics=("parallel", "parallel"),
            vmem_limit_bytes=_vmem_limit(40 * seq_len * w * 4)),
        name="rec_scan_init" if has_init else "rec_scan",
    )(*args)


def kernel(x_prompt, x_sample, c, cache_k, cache_v, state_s, c_ctx, w_ada, b_ada, norm_pre, norm_post,
           w_ffn_in, w_ffn_out, w_qkv, w_attn_out, attn_sink, w_rec_in, rec_lb_logits, rec_norm, w_rec_out):
    x = (x_prompt.reshape(N_PROMPT, D_MODEL), x_sample.reshape(N_SAMPLE, D_MODEL))
    cond = jnp.concatenate([c_ctx[None], c, jnp.zeros((SUBLANES - N_COND, D_MODEL), F32)], axis=0)
    mods = _ada_mod(cond, w_ada, b_ada)

    lb_soft = jax.nn.softmax(rec_lb_logits.astype(F32), axis=1)
    lb_all = jnp.cumsum(lb_soft, axis=1) - lb_soft[:, :1]

    nk = N_KV_HEADS * HEAD_DIM
    ffn_w = (w_ffn_in[0, 0].astype(BF16), w_ffn_out[0, 0].astype(BF16))
    new_k = new_v = new_s = None
    for i in range(DEPTH):
        mod = mods[i]
        j = i // 2
        mixer_params = ((w_qkv, (j,)), (w_attn_out, (j,))) if i % 2 == 0 else ((w_rec_in, (j,)), (w_rec_out, (j,)))
        x, cast = _ffn(x, mod, norm_pre[i, 0], norm_post[i, 0], *ffn_w, 0,
                       casts=mixer_params + ((w_ffn_in, (i, 1)), (w_ffn_out, (i, 1))))
        w_mix_in, w_mix_out, ffn_w = cast[0], cast[1], tuple(cast[2:])
        if i % 2 == 0:
            q, k_p, v_p, k_s, v_s = _qkv(x, mod, norm_pre[i, 1], w_mix_in, ())
            x_p = _attn_prompt(attn_sink[j], q, k_p, v_p, x, mod, norm_post[i, 1], w_mix_out)
            x_s = _attn_sample(attn_sink[j], q, k_s, v_s,
                               cache_k[:, j].reshape(DEC_BATCH, PAST_LEN, nk),
                               cache_v[:, j].reshape(DEC_BATCH, PAST_LEN, nk),
                               x, mod, norm_post[i, 1], w_mix_out)
            new_k = k_p.reshape(BATCH, 1, SEQ, N_KV_HEADS, HEAD_DIM)
            new_v = v_p.reshape(BATCH, 1, SEQ, N_KV_HEADS, HEAD_DIM)
        else:
            acts = _rec_in(x, mod, norm_pre[i, 1], w_mix_in, (), lb_all[0, i], lb_all[1, i])
            x_p, s_p = _rec(acts, rec_norm[j], None, (x, mod, norm_post[i, 1], w_mix_out),
                            seq_len=SEQ, n_seq=BATCH, row_block_off=0)
            o_s = _rec(acts, rec_norm[j], state_s[:, j], None,
                       seq_len=DEC_SEQ, n_seq=DEC_BATCH, row_block_off=N_PROMPT // DEC_SEQ)
            x_s = _mix_out_sample(o_s, x, mod, norm_post[i, 1], w_mix_out)
            new_s = s_p.reshape(BATCH, 1, 2, REC_HEADS, REC_DK, REC_DV)
        last = i == DEPTH - 1
        x, cast = _ffn((x_p, x_s), mod, norm_pre[i, 2], norm_post[i, 2], *ffn_w, 2, split_out=last,
                       casts=() if last else ((w_ffn_in, (i + 1, 0)), (w_ffn_out, (i + 1, 0))))
        ffn_w = tuple(cast)
    y_prompt, y_sample = x
    return (y_prompt.reshape(BATCH, SEQ, D_MODEL), y_sample.reshape(DEC_BATCH, DEC_SEQ, D_MODEL),
            new_k, new_v, new_s)
```

```python
import functools

import jax
import jax.numpy as jnp
from jax import lax
from jax.experimental import pallas as pl
from jax.experimental.pallas import tpu as pltpu

F32 = jnp.float32
BF16 = jnp.bfloat16

D_MODEL = 1024
BATCH = 32
SEQ = 256
DEPTH = 2
DEC_BATCH = 2
DEC_SEQ = 1024
PAST_LEN = 256
GRID_W = 64
HEAD_DIM = 64
N_Q_HEADS = 16
N_KV_HEADS = 4
QKV_DIM = (N_Q_HEADS + 2 * N_KV_HEADS) * HEAD_DIM
ATTN_BLOCK = 128
ROPE_BASE = 10000.0
REC_HEADS = 8
REC_DK = 128
REC_DV = 128
REC_IN_DIM = 5 * D_MODEL
CHUNK = 64
D_FF = 2816
EPS = 1e-6
MASK_VALUE = -1e30

N_PROMPT = BATCH * SEQ
N_SAMPLE = DEC_BATCH * DEC_SEQ
N_TOK = N_PROMPT + N_SAMPLE
N_COND = 1 + DEC_BATCH

LANES = 128
SUBLANES = 8
VMEM_BYTES_V7X = 64 * 1024 * 1024

ROW_TILE = 512
FF_CHUNK = 256
ADA_TILE = 1024
REC_GROUP = 256
REC_STEP_ROWS_X_HEADS = 4096
REC_IN_COLS = 256
FFN_SUBTILES = 2
CAST_STEPS = 16

PROMPT_TILES = N_PROMPT // ROW_TILE
TILES_PER_SAMPLE = DEC_SEQ // ROW_TILE


def _vmem_limit(nbytes):
    del nbytes
    return VMEM_BYTES_V7X - 8 * 1024 * 1024


def _sigmoid(x):
    return 1.0 / (1.0 + jnp.exp(-x))


def _rms(x, g):
    ms = jnp.mean(x * x, axis=-1, keepdims=True)
    return x * lax.rsqrt(ms + EPS) * g


def _mod_in(x, g_pre, mod_ref, slot):
    shift = mod_ref[slot * 3:slot * 3 + 1, :]
    scale = mod_ref[slot * 3 + 1:slot * 3 + 2, :]
    return _rms(x, g_pre * (1.0 + scale)) + shift


def _mod_out(x, y, g_post, mod_ref, slot, weight):
    gate = mod_ref[slot * 3 + 2:slot * 3 + 3, :]
    return x + _rms(y, (weight * gate) * g_post)


def _dot(a, b):
    return jnp.dot(a, b, preferred_element_type=F32)


def _dot_nt(a, b):
    return lax.dot_general(a, b, (((1,), (1,)), ((), ())), preferred_element_type=F32)


def _dot_tn(a, b):
    return lax.dot_general(a, b, (((0,), (0,)), ((), ())), preferred_element_type=F32)


def _tile_group(i):
    return jnp.where(i < PROMPT_TILES, 0, 1 + (i - PROMPT_TILES) // TILES_PER_SAMPLE)


def _row_spec(width):
    return pl.BlockSpec((ROW_TILE, width), lambda i: (i, 0))


def _prompt_row_spec(width):
    return pl.BlockSpec((ROW_TILE, width), lambda i: (jnp.minimum(i, PROMPT_TILES - 1), 0))


def _sample_row_spec(width):
    return pl.BlockSpec((ROW_TILE, width), lambda i: (jnp.maximum(i - PROMPT_TILES, 0), 0))


def _mod_spec():
    return pl.BlockSpec((None, 9, D_MODEL), lambda i: (_tile_group(i), 0, 0))


def _const_spec(shape, lead=()):
    nd = len(shape)
    return pl.BlockSpec((None,) * len(lead) + tuple(shape), lambda *_: tuple(lead) + (0,) * nd,
                        pipeline_mode=pl.Buffered(1))


def _ada_kernel(cond_ref, w_ref, b_ref, o_ref):
    c = cond_ref[...]
    s = (c * _sigmoid(c)).astype(BF16)
    o_ref[...] = _dot(s, w_ref[...].astype(BF16)) + b_ref[...]


def _ada_mod(cond, w_ada, b_ada):
    n_out = 9 * D_MODEL
    out = pl.pallas_call(
        _ada_kernel,
        grid=(DEPTH, n_out // ADA_TILE),
        in_specs=[
            pl.BlockSpec((SUBLANES, D_MODEL), lambda l, j: (0, 0)),
            pl.BlockSpec((None, D_MODEL, ADA_TILE), lambda l, j: (l, 0, j)),
            pl.BlockSpec((None, 1, ADA_TILE), lambda l, j: (l, 0, j)),
        ],
        out_specs=pl.BlockSpec((None, SUBLANES, ADA_TILE), lambda l, j: (l, 0, j)),
        out_shape=jax.ShapeDtypeStruct((DEPTH, SUBLANES, n_out), F32),
        compiler_params=pltpu.CompilerParams(
            dimension_semantics=("parallel", "parallel"),
            vmem_limit_bytes=_vmem_limit(4 * D_MODEL * ADA_TILE * 4)),
        name="ada_mod",
    )(cond, w_ada, b_ada.reshape(DEPTH, 1, n_out))
    return out[:, :N_COND].reshape(DEPTH, N_COND, 9, D_MODEL)


def _ffn_kernel(*refs, slot, split_in, split_out, n_casts):
    refs = list(refs)
    x_refs = [refs.pop(0) for _ in range(2 if split_in else 1)]
    mod_ref, gpre_ref, gpost_ref, win_ref, wout_ref = (refs.pop(0) for _ in range(5))
    cast_in = [refs.pop(0) for _ in range(n_casts)]
    o_refs = [refs.pop(0) for _ in range(2 if split_out else 1)]
    cast_out = [refs.pop(0) for _ in range(n_casts)]
    g_scr = refs.pop(0)
    is_prompt = pl.program_id(0) < PROMPT_TILES
    if split_in:
        x_scr = refs.pop(0)

        @pl.when(is_prompt)
        def _():
            x_scr[...] = x_refs[0][...]

        @pl.when(jnp.logical_not(is_prompt))
        def _():
            x_scr[...] = x_refs[1][...]

        x = x_scr[...]
    else:
        x = x_refs[0][...]
    parts = []
    sub = ROW_TILE // FFN_SUBTILES
    for s in range(FFN_SUBTILES):
        rows = slice(s * sub, (s + 1) * sub)
        xs = x[rows]
        h = _mod_in(xs, gpre_ref[...], mod_ref, slot).astype(BF16)
        for c in range(D_FF // FF_CHUNK):
            lo = c * FF_CHUNK
            a = _dot(h, win_ref[:, lo:lo + FF_CHUNK])
            b = _dot(h, win_ref[:, D_FF + lo:D_FF + lo + FF_CHUNK])
            g_scr[rows, lo:lo + FF_CHUNK] = (a * _sigmoid(a) * b).astype(BF16)
            if s == 0 and c < len(cast_in):
                cast_out[c][...] = cast_in[c][...].astype(BF16)
        y = _dot(g_scr[rows, :], wout_ref[...])
        parts.append(_mod_out(xs, y, gpost_ref[...], mod_ref, slot, 0.5))
    out = jnp.concatenate(parts, axis=0)
    if split_out:
        @pl.when(is_prompt)
        def _():
            o_refs[0][...] = out

        @pl.when(jnp.logical_not(is_prompt))
        def _():
            o_refs[1][...] = out
    else:
        o_refs[0][...] = out


def _ffn(x, mod, g_pre, g_post, w_in, w_out, slot, split_out=False, casts=()):
    split_in = isinstance(x, tuple)
    xs = x if split_in else (x,)
    vmem = (2 * D_MODEL * D_FF + D_FF * D_MODEL) * 2 + 4 * ROW_TILE * D_MODEL * 4 \
        + ROW_TILE * D_FF * 2 + 8 * ROW_TILE * FF_CHUNK * 4 + 6 * ROW_TILE * D_MODEL * 4
    pair_specs = [_prompt_row_spec(D_MODEL), _sample_row_spec(D_MODEL)]
    scratch = [pltpu.VMEM((ROW_TILE, D_FF), BF16)]
    if split_in:
        scratch.append(pltpu.VMEM((ROW_TILE, D_MODEL), F32))
    if split_out:
        out_specs = [_prompt_row_spec(D_MODEL), _sample_row_spec(D_MODEL)]
        out_shape = [jax.ShapeDtypeStruct((N_PROMPT, D_MODEL), F32),
                     jax.ShapeDtypeStruct((N_SAMPLE, D_MODEL), F32)]
    else:
        out_specs = [_row_spec(D_MODEL)]
        out_shape = [jax.ShapeDtypeStruct((N_TOK, D_MODEL), F32)]
    cast_in_specs = []
    for arr, lead in casts:
        rows, cols = arr.shape[-2:]
        blk = rows // CAST_STEPS
        assert blk * CAST_STEPS == rows and blk % 16 == 0 and CAST_STEPS <= N_TOK // ROW_TILE
        cast_in_specs.append(pl.BlockSpec(
            (None,) * len(lead) + (blk, cols),
            lambda i, lead=lead: tuple(lead) + (jnp.minimum(i, CAST_STEPS - 1), 0)))
        out_specs.append(pl.BlockSpec((blk, cols), lambda i: (jnp.minimum(i, CAST_STEPS - 1), 0)))
        out_shape.append(jax.ShapeDtypeStruct((rows, cols), BF16))
        vmem += 2 * blk * cols * (4 + 2)
    res = pl.pallas_call(
        functools.partial(_ffn_kernel, slot=slot, split_in=split_in, split_out=split_out, n_casts=len(casts)),
        grid=(N_TOK // ROW_TILE,),
        in_specs=(pair_specs if split_in else [_row_spec(D_MODEL)]) + [
            _mod_spec(),
            _const_spec((1, D_MODEL)),
            _const_spec((1, D_MODEL)),
            _const_spec((D_MODEL, 2 * D_FF)),
            _const_spec((D_FF, D_MODEL)),
        ] + cast_in_specs,
        out_specs=out_specs,
        out_shape=out_shape,
        scratch_shapes=scratch,
        compiler_params=pltpu.CompilerParams(
            dimension_semantics=("arbitrary",), vmem_limit_bytes=_vmem_limit(vmem)),
        name="ffn_sublayer",
    )(*xs, mod, g_pre.reshape(1, D_MODEL), g_post.reshape(1, D_MODEL), w_in, w_out, *[a for a, _ in casts])
    n_stream = 2 if split_out else 1
    stream = tuple(res[:n_stream]) if split_out else res[0]
    return stream, list(res[n_stream:])


def _qkv_kernel(x_ref, mod_ref, gpre_ref, w_ref, cos_ref, sin_ref, q_ref, kp_ref, vp_ref, k_ref, v_ref):
    i = pl.program_id(0)
    scale = HEAD_DIM ** -0.5
    h = _mod_in(x_ref[...], gpre_ref[...], mod_ref, 1).astype(BF16)
    qkv = _dot(h, w_ref[...])
    nq = N_Q_HEADS * HEAD_DIM
    nk = N_KV_HEADS * HEAD_DIM

    @pl.when(i < PROMPT_TILES)
    def _():
        q_ref[...] = (qkv[:, :nq] * scale).astype(BF16)
        kp_ref[...] = qkv[:, nq:nq + nk]
        vp_ref[...] = qkv[:, nq + nk:]

    @pl.when(i >= PROMPT_TILES)
    def _():
        v_ref[...] = qkv[:, nq + nk:]
        cos = cos_ref[...]
        sin = sin_ref[...]
        lane = lax.broadcasted_iota(jnp.int32, (ROW_TILE, LANES), 1)
        first = (lane & (HEAD_DIM // 4)) == 0

        def rope(xg):
            up = pltpu.roll(xg, LANES - HEAD_DIM // 4, 1)
            down = pltpu.roll(xg, HEAD_DIM // 4, 1)
            return xg * cos + jnp.where(first, up, down) * sin

        for j in range(nq // LANES):
            q_ref[:, j * LANES:(j + 1) * LANES] = (rope(qkv[:, j * LANES:(j + 1) * LANES]) * scale).astype(BF16)
        for j in range(nk // LANES):
            k_ref[:, j * LANES:(j + 1) * LANES] = rope(qkv[:, nq + j * LANES:nq + (j + 1) * LANES])


def _rope_tables():
    t = jnp.arange(DEC_SEQ)
    row = (t // GRID_W).astype(F32)
    col = (t % GRID_W).astype(F32)
    nf = HEAD_DIM // 4
    inv = ROPE_BASE ** (-jnp.arange(nf, dtype=F32) / nf)
    ar = row[:, None] * inv[None, :]
    ac = col[:, None] * inv[None, :]
    cos = jnp.concatenate([jnp.cos(ar), jnp.cos(ar), jnp.cos(ac), jnp.cos(ac)], axis=-1)
    sin = jnp.concatenate([-jnp.sin(ar), jnp.sin(ar), -jnp.sin(ac), jnp.sin(ac)], axis=-1)
    reps = LANES // HEAD_DIM
    return jnp.tile(cos, (1, reps)), jnp.tile(sin, (1, reps))


def _qkv(x, mod, g_pre, w_qkv, lead):
    cos, sin = _rope_tables()
    tab_spec = pl.BlockSpec(
        (ROW_TILE, LANES), lambda i: (jnp.maximum(i - PROMPT_TILES, 0) % TILES_PER_SAMPLE, 0))
    nk = N_KV_HEADS * HEAD_DIM
    vmem = D_MODEL * QKV_DIM * 2 + 6 * ROW_TILE * D_MODEL * 4 + 4 * ROW_TILE * QKV_DIM * 4
    return pl.pallas_call(
        _qkv_kernel,
        grid=(N_TOK // ROW_TILE,),
        in_specs=[
            _row_spec(D_MODEL),
            _mod_spec(),
            _const_spec((1, D_MODEL)),
            _const_spec((D_MODEL, QKV_DIM), lead),
            tab_spec,
            tab_spec,
        ],
        out_specs=[_row_spec(D_MODEL), _prompt_row_spec(nk), _prompt_row_spec(nk),
                   _sample_row_spec(nk), _sample_row_spec(nk)],
        out_shape=[
            jax.ShapeDtypeStruct((N_TOK, D_MODEL), BF16),
            jax.ShapeDtypeStruct((N_PROMPT, nk), F32),
            jax.ShapeDtypeStruct((N_PROMPT, nk), F32),
            jax.ShapeDtypeStruct((N_SAMPLE, nk), F32),
            jax.ShapeDtypeStruct((N_SAMPLE, nk), F32),
        ],
        compiler_params=pltpu.CompilerParams(
            dimension_semantics=("arbitrary",), vmem_limit_bytes=_vmem_limit(vmem)),
        name="attn_qkv",
    )(x, mod, g_pre.reshape(1, D_MODEL), w_qkv, cos, sin)


def _lane_halves(ref_or_val, hkv, rows):
    grp = ref_or_val[:, (hkv // 2) * LANES:(hkv // 2 + 1) * LANES]
    lane = lax.broadcasted_iota(jnp.int32, (rows, LANES), 1)
    in_low = lane < HEAD_DIM
    if hkv % 2 == 0:
        lo = jnp.where(in_low, grp, 0.0)
        hi = pltpu.roll(lo, HEAD_DIM, 1)
    else:
        hi = jnp.where(in_low, 0.0, grp)
        lo = pltpu.roll(hi, HEAD_DIM, 1)
    return lo.astype(BF16), hi.astype(BF16)


def _attend(sink_ref, q_ref, keys, vals, n_keys, valid):
    group = N_Q_HEADS // N_KV_HEADS
    k_halves = [_lane_halves(keys, hkv, n_keys) for hkv in range(N_KV_HEADS)]
    v_halves = [_lane_halves(vals, hkv, n_keys) for hkv in range(N_KV_HEADS)]
    scores = []
    for h in range(N_Q_HEADS):
        s = _dot_nt(q_ref[:, (h // 2) * LANES:(h // 2 + 1) * LANES], k_halves[h // group][h % 2])
        scores.append(s if valid is None else jnp.where(valid, s, MASK_VALUE))
    heads = range(N_Q_HEADS)
    maxes = [jnp.maximum(jnp.max(scores[h], axis=-1, keepdims=True), sink_ref[h]) for h in heads]
    exps = [jnp.exp(scores[h] - maxes[h]) for h in heads]
    denoms = [jnp.sum(exps[h], axis=-1, keepdims=True) + jnp.exp(sink_ref[h] - maxes[h]) for h in heads]
    probs = [(exps[h] * (1.0 / denoms[h])).astype(BF16) for h in heads]
    out = []
    for j in range(N_Q_HEADS // 2):
        v_lo, v_hi = v_halves[(2 * j) // group]
        out.append((_dot(probs[2 * j], v_lo) + _dot(probs[2 * j + 1], v_hi)).astype(BF16))
    return jnp.concatenate(out, axis=1)


def _mixer_finish(o, x_ref, mod_ref, gpost_ref, w_ref, out_ref):
    out_ref[...] = _mod_out(x_ref[...], _dot(o, w_ref[...]), gpost_ref[...], mod_ref, 1, 1.0)


def _mixer_specs(rows, x_index, group_index):
    return [
        pl.BlockSpec((rows, D_MODEL), x_index),
        pl.BlockSpec((None, 9, D_MODEL), group_index),
        _const_spec((1, D_MODEL)),
        _const_spec((D_MODEL, D_MODEL)),
    ]


def _attn_prompt_kernel(sink_ref, q_ref, k_ref, v_ref, x_ref, mod_ref, gpost_ref, w_ref, out_ref):
    o = _attend(sink_ref, q_ref, k_ref[...], v_ref[...], SEQ, None)
    _mixer_finish(o, x_ref, mod_ref, gpost_ref, w_ref, out_ref)


def _attn_prompt(sink, q, k, v, x, mod, g_post, w_out):
    nk = N_KV_HEADS * HEAD_DIM
    return pl.pallas_call(
        _attn_prompt_kernel,
        grid=(BATCH,),
        in_specs=[
            pl.BlockSpec(memory_space=pltpu.SMEM),
            pl.BlockSpec((SEQ, D_MODEL), lambda b: (b, 0)),
            pl.BlockSpec((SEQ, nk), lambda b: (b, 0)),
            pl.BlockSpec((SEQ, nk), lambda b: (b, 0)),
        ] + _mixer_specs(SEQ, lambda b: (b, 0), lambda b: (0, 0, 0)),
        out_specs=pl.BlockSpec((SEQ, D_MODEL), lambda b: (b, 0)),
        out_shape=jax.ShapeDtypeStruct((N_PROMPT, D_MODEL), F32),
        compiler_params=pltpu.CompilerParams(
            dimension_semantics=("parallel",), vmem_limit_bytes=_vmem_limit(32 * 1024 * 1024)),
        name="attn_prompt",
    )(sink, q, k, v, x, mod, g_post.reshape(1, D_MODEL), w_out)


def _attn_sample_kernel(sink_ref, q_ref, k_ref, v_ref, ck_ref, cv_ref,
                        x_ref, mod_ref, gpost_ref, w_ref, out_ref):
    qb = pl.program_id(1)
    nblk = DEC_SEQ // ATTN_BLOCK
    B = ATTN_BLOCK
    starts = [
        pl.multiple_of(jnp.maximum(qb - 1, 0) * B, B),
        pl.multiple_of(qb * B, B),
        pl.multiple_of(jnp.minimum(qb + 1, nblk - 1) * B, B),
    ]
    keys = jnp.concatenate([k_ref[pl.ds(s, B), :] for s in starts] + [ck_ref[...]], axis=0)
    vals = jnp.concatenate([v_ref[pl.ds(s, B), :] for s in starts] + [cv_ref[...]], axis=0)
    n_keys = 3 * B + PAST_LEN
    r = lax.broadcasted_iota(jnp.int32, (B, n_keys), 0)
    c = lax.broadcasted_iota(jnp.int32, (B, n_keys), 1)
    prev_bad = (c < B) & ((c < r) | (qb == 0))
    next_bad = (c >= 2 * B) & (c < 3 * B) & (((c - 2 * B) > r) | (qb == nblk - 1))
    valid = jnp.logical_not(prev_bad | next_bad)
    o = _attend(sink_ref, q_ref, keys, vals, n_keys, valid)
    _mixer_finish(o, x_ref, mod_ref, gpost_ref, w_ref, out_ref)


def _attn_sample(sink, q, k, v, cache_k, cache_v, x, mod, g_post, w_out):
    nk = N_KV_HEADS * HEAD_DIM
    nblk = DEC_SEQ // ATTN_BLOCK
    q_off = N_PROMPT // ATTN_BLOCK
    return pl.pallas_call(
        _attn_sample_kernel,
        grid=(DEC_BATCH, nblk),
        in_specs=[
            pl.BlockSpec(memory_space=pltpu.SMEM),
            pl.BlockSpec((ATTN_BLOCK, D_MODEL), lambda b, t: (q_off + b * nblk + t, 0)),
            pl.BlockSpec((DEC_SEQ, nk), lambda b, t: (b, 0)),
            pl.BlockSpec((DEC_SEQ, nk), lambda b, t: (b, 0)),
            pl.BlockSpec((None, PAST_LEN, nk), lambda b, t: (b, 0, 0)),
            pl.BlockSpec((None, PAST_LEN, nk), lambda b, t: (b, 0, 0)),
        ] + _mixer_specs(ATTN_BLOCK, lambda b, t: (q_off + b * nblk + t, 0), lambda b, t: (1 + b, 0, 0)),
        out_specs=pl.BlockSpec((ATTN_BLOCK, D_MODEL), lambda b, t: (b * nblk + t, 0)),
        out_shape=jax.ShapeDtypeStruct((N_SAMPLE, D_MODEL), F32),
        compiler_params=pltpu.CompilerParams(
            dimension_semantics=("parallel", "parallel"), vmem_limit_bytes=_vmem_limit(32 * 1024 * 1024)),
        name="attn_sample",
    )(sink, q, k, v, cache_k, cache_v, x, mod, g_post.reshape(1, D_MODEL), w_out)


def _mix_out_sample_kernel(o_ref, x_ref, mod_ref, gpost_ref, w_ref, out_ref):
    _mixer_finish(o_ref[...], x_ref, mod_ref, gpost_ref, w_ref, out_ref)


def _mix_out_sample(o_sample, x, mod, g_post, w):
    vmem = D_MODEL * D_MODEL * 2 + 12 * ROW_TILE * D_MODEL * 4
    return pl.pallas_call(
        _mix_out_sample_kernel,
        grid=(N_SAMPLE // ROW_TILE,),
        in_specs=[pl.BlockSpec((ROW_TILE, D_MODEL), lambda i: (i, 0))] + _mixer_specs(
            ROW_TILE, lambda i: (PROMPT_TILES + i, 0), lambda i: (1 + i // TILES_PER_SAMPLE, 0, 0)),
        out_specs=pl.BlockSpec((ROW_TILE, D_MODEL), lambda i: (i, 0)),
        out_shape=jax.ShapeDtypeStruct((N_SAMPLE, D_MODEL), F32),
        compiler_params=pltpu.CompilerParams(
            dimension_semantics=("parallel",), vmem_limit_bytes=_vmem_limit(vmem)),
        name="mixer_out_sample",
    )(o_sample, x, mod, g_post.reshape(1, D_MODEL), w)


def _chunk_masks():
    r = lax.broadcasted_iota(jnp.int32, (REC_GROUP, REC_GROUP), 0)
    c = lax.broadcasted_iota(jnp.int32, (REC_GROUP, REC_GROUP), 1)
    same = (r // CHUNK) == (c // CHUNK)
    return same & (c <= r), same & (c >= r)


def _rec_in_kernel(x_ref, mod_ref, gpre_ref, w_ref, lbf_ref, lbb_ref,
                   qdf_ref, kif_ref, kef_ref, qdb_ref, kib_ref, keb_ref, v_ref, sg_ref, decf_ref, decb_ref):
    G = REC_GROUP
    CB = REC_IN_COLS
    h = _mod_in(x_ref[...], gpre_ref[...], mod_ref, 1).astype(BF16)
    ri = lax.broadcasted_iota(jnp.int32, (CHUNK, CHUNK), 0)
    ci = lax.broadcasted_iota(jnp.int32, (CHUNK, CHUNK), 1)
    tris = tuple(jnp.concatenate([jnp.where(m, 1.0, 0.0).astype(BF16)] * 3, axis=1) for m in (ci <= ri, ci >= ri))
    outs = ((qdf_ref, kif_ref, kef_ref, decf_ref, lbf_ref), (qdb_ref, kib_ref, keb_ref, decb_ref, lbb_ref))

    def project(cb, part):
        return _dot(h, w_ref[:, part * D_MODEL + cb * CB:part * D_MODEL + (cb + 1) * CB])

    def finish_qvg(cb, yq, yv, yg):
        cols = slice(cb * CB, (cb + 1) * CB)
        v_ref[:, cols] = yv.astype(BF16)
        sg_ref[:, cols] = yg * _sigmoid(yg)
        return yq * _sigmoid(yq) * (REC_DK ** -0.5)

    def finish_dir(cb, d, z, qf):
        cols = slice(cb * CB, (cb + 1) * CB)
        qd_ref, ki_ref, ke_ref, dec_ref, lb_ref = outs[d]
        one_m_lb = 1.0 - lb_ref[:, cols]
        e = jnp.exp(-jnp.abs(z))
        rcp = 1.0 / (1.0 + e)
        key = one_m_lb * jnp.where(z >= 0, e * rcp, rcp)
        logf = jnp.log(1.0 - key)
        hi = logf.astype(BF16)
        rest = logf - hi.astype(F32)
        mid = rest.astype(BF16)
        lo = (rest - mid.astype(F32)).astype(BF16)
        edge = CHUNK - 1 if d == 0 else 0
        for c in range(ROW_TILE // CHUNK):
            rows = slice(c * CHUNK, (c + 1) * CHUNK)
            bc = _dot(tris[d], jnp.concatenate([hi[rows], mid[rows], lo[rows]], axis=0))
            decay = jnp.exp(bc)
            k_inv = key[rows] * (1.0 / decay)
            chunk_decay = decay[edge:edge + 1, :]
            qd_ref[rows, cols] = (qf[rows] * decay).astype(BF16)
            ki_ref[rows, cols] = k_inv.astype(BF16)
            ke_ref[rows, cols] = (k_inv * chunk_decay).astype(BF16)
            dec_ref[c // (G // CHUNK), c % (G // CHUNK):c % (G // CHUNK) + 1, cols] = chunk_decay

    n_blocks = D_MODEL // CB
    cur = [project(0, part) for part in range(5)]
    for cb in range(n_blocks):
        more = cb + 1 < n_blocks
        nxt = [None] * 5
        if more:
            nxt[0], nxt[1] = project(cb + 1, 0), project(cb + 1, 1)
        qf = finish_qvg(cb, cur[0], cur[1], cur[4])
        if more:
            nxt[2], nxt[3] = project(cb + 1, 2), project(cb + 1, 3)
        finish_dir(cb, 0, cur[2], qf)
        if more:
            nxt[4] = project(cb + 1, 4)
        finish_dir(cb, 1, cur[3], qf)
        cur = nxt


def _rec_in(x, mod, g_pre, w, lead, lb_f, lb_b):
    n_chunks = REC_GROUP // CHUNK
    groups_per_tile = ROW_TILE // REC_GROUP
    vmem = D_MODEL * REC_IN_DIM * 2 + 4 * ROW_TILE * D_MODEL * 4 + 2 * ROW_TILE * D_MODEL * (7 * 2 + 4) \
        + 12 * ROW_TILE * D_MODEL * 4
    act = jax.ShapeDtypeStruct((N_TOK, D_MODEL), BF16)
    dec = jax.ShapeDtypeStruct((N_TOK // REC_GROUP, n_chunks, D_MODEL), F32)
    dec_spec = pl.BlockSpec((groups_per_tile, n_chunks, D_MODEL), lambda i: (i, 0, 0))
    return pl.pallas_call(
        _rec_in_kernel,
        grid=(N_TOK // ROW_TILE,),
        in_specs=[
            _row_spec(D_MODEL),
            _mod_spec(),
            _const_spec((1, D_MODEL)),
            _const_spec((D_MODEL, REC_IN_DIM), lead),
            _const_spec((1, D_MODEL)),
            _const_spec((1, D_MODEL)),
        ],
        out_specs=[_row_spec(D_MODEL) for _ in range(8)] + [dec_spec, dec_spec],
        out_shape=[act] * 7 + [jax.ShapeDtypeStruct((N_TOK, D_MODEL), F32), dec, dec],
        compiler_params=pltpu.CompilerParams(
            dimension_semantics=("parallel",), vmem_limit_bytes=_vmem_limit(vmem)),
        name="rec_in",
    )(x, mod, g_pre.reshape(1, D_MODEL), w, lb_f.reshape(1, D_MODEL), lb_b.reshape(1, D_MODEL))


def _rec_kernel(*refs, seq_len, n_heads, has_init):
    qdf_ref, kif_ref, kef_ref, qdb_ref, kib_ref, keb_ref, v_ref, sg_ref, decf_ref, decb_ref, gn_ref = refs[:11]
    if has_init:
        s0_ref, o_ref = refs[11:]
        st_ref = mixer_refs = None
    else:
        *mixer_refs, out_ref, st_ref, o_ref = refs[11:]
        s0_ref = None
    G = REC_GROUP
    per_group = G // CHUNK
    n_chunks = seq_len // CHUNK
    mask_f, mask_b = _chunk_masks()

    def rows_of(c):
        return slice(c * CHUNK, (c + 1) * CHUNK)

    heads = range(n_heads)
    cols = [slice(hh * REC_DK, (hh + 1) * REC_DK) for hh in heads]
    groups = range(seq_len // G)

    def group_rows(g):
        return slice(g * G, (g + 1) * G)

    upd = [[_dot_tn(v_ref[rows_of(c), cs],
                    jnp.concatenate([kef_ref[rows_of(c), cs], keb_ref[rows_of(c), cs]], axis=1))
            for c in range(n_chunks)] for cs in cols]
    a = [[(jnp.where(mask_f, _dot_nt(qdf_ref[group_rows(g), cs], kif_ref[group_rows(g), cs]), 0.0)
           + jnp.where(mask_b, _dot_nt(qdb_ref[group_rows(g), cs], kib_ref[group_rows(g), cs]), 0.0)
           ).astype(BF16) for g in groups] for cs in cols]
    enter = []
    for hh in heads:
        cs = cols[hh]
        if has_init:
            s_f, s_b = s0_ref[0, hh].T, s0_ref[1, hh].T
        else:
            s_f = s_b = jnp.zeros((REC_DV, REC_DK), F32)
        ent = [None] * n_chunks
        for c in range(n_chunks):
            ent[c] = s_f.astype(BF16)
            s_f = s_f * decf_ref[c // per_group, c % per_group:c % per_group + 1, cs] + upd[hh][c][:, :REC_DK]
        for c in range(n_chunks - 1, -1, -1):
            ent[c] = jnp.concatenate([ent[c], s_b.astype(BF16)], axis=1)
            s_b = s_b * decb_ref[c // per_group, c % per_group:c % per_group + 1, cs] + upd[hh][c][:, REC_DK:]
        enter.append(ent)
        if st_ref is not None:
            st_ref[0, hh] = s_f.T
            st_ref[1, hh] = s_b.T
    o_intra = [[_dot(a[hh][g], v_ref[group_rows(g), cols[hh]]) for g in groups] for hh in heads]
    for hh in heads:
        cs = cols[hh]
        for c in range(n_chunks):
            q_cat = jnp.concatenate([qdf_ref[rows_of(c), cs], qdb_ref[rows_of(c), cs]], axis=1)
            o_c = o_intra[hh][c // per_group][rows_of(c % per_group)] + _dot_nt(q_cat, enter[hh][c])
            o_ref[rows_of(c), cs] = (_rms(o_c, gn_ref[hh]) * sg_ref[rows_of(c), cs]).astype(BF16)
    if mixer_refs is not None:
        _mixer_finish(o_ref[...], *mixer_refs, out_ref)


def _rec(rec_acts, g_norm, s0, mixer, *, seq_len, n_seq, row_block_off):
    hb = min(REC_HEADS, REC_STEP_ROWS_X_HEADS // seq_len)
    w = hb * REC_DK
    n_hp = REC_HEADS // hb
    n_groups = seq_len // REC_GROUP
    has_init = s0 is not None

    def act_spec():
        return pl.BlockSpec((seq_len, w), lambda b, h: (row_block_off + b, h))

    def dec_spec():
        return pl.BlockSpec((n_groups, REC_GROUP // CHUNK, w), lambda b, h: (row_block_off + b, 0, h))

    in_specs = [act_spec() for _ in range(8)] + [dec_spec(), dec_spec(),
                                                 pl.BlockSpec((hb, 1, REC_DV), lambda b, h: (h, 0, 0))]
    args = list(rec_acts) + [g_norm.reshape(REC_HEADS, 1, REC_DV)]
    state_spec = pl.BlockSpec((None, 2, hb, REC_DK, REC_DV), lambda b, h: (b, 0, h, 0, 0))
    scratch = []
    if has_init:
        in_specs.append(state_spec)
        args.append(s0)
        out_specs = pl.BlockSpec((seq_len, w), lambda b, h: (b, h))
        out_shape = jax.ShapeDtypeStruct((n_seq * seq_len, D_MODEL), BF16)
    else:
        assert n_hp == 1
        x, mod, g_post, w_out = mixer
        in_specs += _mixer_specs(seq_len, lambda b, h: (row_block_off + b, 0), lambda b, h: (0, 0, 0))
        args += [x, mod, g_post.reshape(1, D_MODEL), w_out]
        out_specs = [pl.BlockSpec((seq_len, D_MODEL), lambda b, h: (b, 0)), state_spec]
        out_shape = [jax.ShapeDtypeStruct((n_seq * seq_len, D_MODEL), F32),
                     jax.ShapeDtypeStruct((n_seq, 2, REC_HEADS, REC_DK, REC_DV), F32)]
        scratch = [pltpu.VMEM((seq_len, D_MODEL), BF16)]
    return pl.pallas_call(
        functools.partial(_rec_kernel, seq_len=seq_len, n_heads=hb, has_init=has_init),
        grid=(n_seq, n_hp),
        in_specs=in_specs,
        out_specs=out_specs,
        out_shape=out_shape,
        scratch_shapes=scratch,
        compiler_params=pltpu.CompilerParams(
            dimension_semantics=("parallel", "parallel"),
            vmem_limit_bytes=_vmem_limit(40 * seq_len * w * 4)),
        name="rec_scan_init" if has_init else "rec_scan",
    )(*args)


def kernel(x_prompt, x_sample, c, cache_k, cache_v, state_s, c_ctx, w_ada, b_ada, norm_pre, norm_post,
           w_ffn_in, w_ffn_out, w_qkv, w_attn_out, attn_sink, w_rec_in, rec_lb_logits, rec_norm, w_rec_out):
    x = (x_prompt.reshape(N_PROMPT, D_MODEL), x_sample.reshape(N_SAMPLE, D_MODEL))
    cond = jnp.concatenate([c_ctx[None], c, jnp.zeros((SUBLANES - N_COND, D_MODEL), F32)], axis=0)
    mods = _ada_mod(cond, w_ada, b_ada)

    lb_soft = jax.nn.softmax(rec_lb_logits.astype(F32), axis=1)
    lb_all = jnp.cumsum(lb_soft, axis=1) - lb_soft[:, :1]

    nk = N_KV_HEADS * HEAD_DIM
    ffn_w = (w_ffn_in[0, 0].astype(BF16), w_ffn_out[0, 0].astype(BF16))
    new_k = new_v = new_s = None
    for i in range(DEPTH):
        mod = mods[i]
        j = i // 2
        mixer_params = ((w_qkv, (j,)), (w_attn_out, (j,))) if i % 2 == 0 else ((w_rec_in, (j,)), (w_rec_out, (j,)))
        x, cast = _ffn(x, mod, norm_pre[i, 0], norm_post[i, 0], *ffn_w, 0,
                       casts=mixer_params + ((w_ffn_in, (i, 1)), (w_ffn_out, (i, 1))))
        w_mix_in, w_mix_out, ffn_w = cast[0], cast[1], tuple(cast[2:])
        if i % 2 == 0:
            q, k_p, v_p, k_s, v_s = _qkv(x, mod, norm_pre[i, 1], w_mix_in, ())
            x_p = _attn_prompt(attn_sink[j], q, k_p, v_p, x, mod, norm_post[i, 1], w_mix_out)
            x_s = _attn_sample(attn_sink[j], q, k_s, v_s,
                               cache_k[:, j].reshape(DEC_BATCH, PAST_LEN, nk),
                               cache_v[:, j].reshape(DEC_BATCH, PAST_LEN, nk),
                               x, mod, norm_post[i, 1], w_mix_out)
            new_k = k_p.reshape(BATCH, 1, SEQ, N_KV_HEADS, HEAD_DIM)
            new_v = v_p.reshape(BATCH, 1, SEQ, N_KV_HEADS, HEAD_DIM)
        else:
            acts = _rec_in(x, mod, norm_pre[i, 1], w_mix_in, (), lb_all[0, i], lb_all[1, i])
            x_p, s_p = _rec(acts, rec_norm[j], None, (x, mod, norm_post[i, 1], w_mix_out),
                            seq_len=SEQ, n_seq=BATCH, row_block_off=0)
            o_s = _rec(acts, rec_norm[j], state_s[:, j], None,
                       seq_len=DEC_SEQ, n_seq=DEC_BATCH, row_block_off=N_PROMPT // DEC_SEQ)
            x_s = _mix_out_sample(o_s, x, mod, norm_post[i, 1], w_mix_out)
            new_s = s_p.reshape(BATCH, 1, 2, REC_HEADS, REC_DK, REC_DV)
        last = i == DEPTH - 1
        x, cast = _ffn((x_p, x_s), mod, norm_pre[i, 2], norm_post[i, 2], *ffn_w, 2, split_out=last,
                       casts=() if last else ((w_ffn_in, (i + 1, 0)), (w_ffn_out, (i + 1, 0))))
        ffn_w = tuple(cast)
    y_prompt, y_sample = x
    return (y_prompt.reshape(BATCH, SEQ, D_MODEL), y_sample.reshape(DEC_BATCH, DEC_SEQ, D_MODEL),
            new_k, new_v, new_s)
```

```python
import functools

import jax
import jax.numpy as jnp
from jax import lax
from jax.experimental import pallas as pl
from jax.experimental.pallas import tpu as pltpu

F32 = jnp.float32
BF16 = jnp.bfloat16

D_MODEL = 1024
BATCH = 32
SEQ = 256
DEPTH = 2
DEC_BATCH = 2
DEC_SEQ = 1024
PAST_LEN = 256
GRID_W = 64
HEAD_DIM = 64
N_Q_HEADS = 16
N_KV_HEADS = 4
QKV_DIM = (N_Q_HEADS + 2 * N_KV_HEADS) * HEAD_DIM
ATTN_BLOCK = 128
ROPE_BASE = 10000.0
REC_HEADS = 8
REC_DK = 128
REC_DV = 128
REC_IN_DIM = 5 * D_MODEL
CHUNK = 64
D_FF = 2816
EPS = 1e-6
MASK_VALUE = -1e30

N_PROMPT = BATCH * SEQ
N_SAMPLE = DEC_BATCH * DEC_SEQ
N_TOK = N_PROMPT + N_SAMPLE
N_COND = 1 + DEC_BATCH

LANES = 128
SUBLANES = 8
VMEM_BYTES_V7X = 64 * 1024 * 1024

ROW_TILE = 512
FF_CHUNK = 256
ADA_TILE = 1024
ADA_SIDE_TILE = 512
REC_GROUP = 256
REC_STEP_ROWS_X_HEADS = 4096
REC_IN_COLS = 256
FFN_SUBTILES = 2
CAST_STEPS = 16

PROMPT_TILES = N_PROMPT // ROW_TILE
TILES_PER_SAMPLE = DEC_SEQ // ROW_TILE


def _vmem_limit(nbytes):
    return int(min(VMEM_BYTES_V7X - 8 * 1024 * 1024, max(nbytes, 16 * 1024 * 1024)))


def _sigmoid(x):
    return 1.0 / (1.0 + jnp.exp(-x))


def _rms(x, g):
    ms = jnp.mean(x * x, axis=-1, keepdims=True)
    return x * lax.rsqrt(ms + EPS) * g


def _mod_in(x, g_pre, mod_ref, slot):
    shift = mod_ref[slot * 3:slot * 3 + 1, :]
    scale = mod_ref[slot * 3 + 1:slot * 3 + 2, :]
    return _rms(x, g_pre * (1.0 + scale)) + shift


def _mod_out(x, y, g_post, mod_ref, slot, weight):
    gate = mod_ref[slot * 3 + 2:slot * 3 + 3, :]
    return x + _rms(y, (weight * gate) * g_post)


def _dot(a, b):
    return jnp.dot(a, b, preferred_element_type=F32)


def _dot_nt(a, b):
    return lax.dot_general(a, b, (((1,), (1,)), ((), ())), preferred_element_type=F32)


def _dot_tn(a, b):
    return lax.dot_general(a, b, (((0,), (0,)), ((), ())), preferred_element_type=F32)


def _tile_group(i):
    return jnp.where(i < PROMPT_TILES, 0, 1 + (i - PROMPT_TILES) // TILES_PER_SAMPLE)


def _row_spec(width):
    return pl.BlockSpec((ROW_TILE, width), lambda i: (i, 0))


def _prompt_row_spec(width):
    return pl.BlockSpec((ROW_TILE, width), lambda i: (jnp.minimum(i, PROMPT_TILES - 1), 0))


def _sample_row_spec(width):
    return pl.BlockSpec((ROW_TILE, width), lambda i: (jnp.maximum(i - PROMPT_TILES, 0), 0))


def _mod_spec():
    return pl.BlockSpec((None, 9, D_MODEL), lambda i: (_tile_group(i), 0, 0))


def _const_spec(shape, lead=()):
    nd = len(shape)
    return pl.BlockSpec((None,) * len(lead) + tuple(shape), lambda *_: tuple(lead) + (0,) * nd,
                        pipeline_mode=pl.Buffered(1))


def _ada_block(cond_ref, w_ref, b_ref, o_ref):
    c = cond_ref[...]
    s = (c * _sigmoid(c)).astype(BF16)
    o_ref[...] = _dot(s, w_ref[...].astype(BF16)) + b_ref[...]


def _ada_specs(layer, tile, col_index):
    in_specs = [
        pl.BlockSpec((SUBLANES, D_MODEL), lambda *i: (0, 0)),
        pl.BlockSpec((None, D_MODEL, tile), lambda *i: (layer, 0, col_index(*i))),
        pl.BlockSpec((None, 1, tile), lambda *i: (layer, 0, col_index(*i))),
    ]
    return in_specs, pl.BlockSpec((SUBLANES, tile), lambda *i: (0, col_index(*i)))


def _mod_rows(out):
    return out[:N_COND].reshape(N_COND, 9, D_MODEL)


def _ada_mod(cond, w_ada, b_ada, layer):
    n_out = 9 * D_MODEL
    in_specs, out_spec = _ada_specs(layer, ADA_TILE, lambda j: j)
    out = pl.pallas_call(
        _ada_block,
        grid=(n_out // ADA_TILE,),
        in_specs=in_specs,
        out_specs=out_spec,
        out_shape=jax.ShapeDtypeStruct((SUBLANES, n_out), F32),
        compiler_params=pltpu.CompilerParams(
            dimension_semantics=("parallel",),
            vmem_limit_bytes=_vmem_limit(4 * D_MODEL * ADA_TILE * 4)),
        name="ada_mod",
    )(cond, w_ada, b_ada.reshape(DEPTH, 1, n_out))
    return _mod_rows(out)


def _ffn_kernel(*refs, slot, split_in, split_out, n_casts, with_ada):
    refs = list(refs)
    x_refs = [refs.pop(0) for _ in range(2 if split_in else 1)]
    mod_ref, gpre_ref, gpost_ref, win_ref, wout_ref = (refs.pop(0) for _ in range(5))
    cast_in = [refs.pop(0) for _ in range(n_casts)]
    ada_in = [refs.pop(0) for _ in range(3 if with_ada else 0)]
    o_refs = [refs.pop(0) for _ in range(2 if split_out else 1)]
    cast_out = [refs.pop(0) for _ in range(n_casts)]
    ada_out = refs.pop(0) if with_ada else None
    g_scr = refs.pop(0)
    is_prompt = pl.program_id(0) < PROMPT_TILES
    if split_in:
        x = jnp.where(is_prompt, x_refs[0][...], x_refs[1][...])
    else:
        x = x_refs[0][...]
    parts = []
    sub = ROW_TILE // FFN_SUBTILES
    for s in range(FFN_SUBTILES):
        rows = slice(s * sub, (s + 1) * sub)
        xs = x[rows]
        h = _mod_in(xs, gpre_ref[...], mod_ref, slot).astype(BF16)
        for c in range(D_FF // FF_CHUNK):
            lo = c * FF_CHUNK
            a = _dot(h, win_ref[:, lo:lo + FF_CHUNK])
            b = _dot(h, win_ref[:, D_FF + lo:D_FF + lo + FF_CHUNK])
            g_scr[rows, lo:lo + FF_CHUNK] = (a * _sigmoid(a) * b).astype(BF16)
            if s == 0 and c < len(cast_in):
                cast_out[c][...] = cast_in[c][...].astype(BF16)
            if s == 0 and c == len(cast_in) and with_ada:
                _ada_block(*ada_in, ada_out)
        y = _dot(g_scr[rows, :], wout_ref[...])
        parts.append(_mod_out(xs, y, gpost_ref[...], mod_ref, slot, 0.5))
    out = jnp.concatenate(parts, axis=0)
    if split_out:
        @pl.when(is_prompt)
        def _():
            o_refs[0][...] = out

        @pl.when(jnp.logical_not(is_prompt))
        def _():
            o_refs[1][...] = out
    else:
        o_refs[0][...] = out


def _ffn(x, mod, g_pre, g_post, w_in, w_out, slot, split_out=False, casts=(), ada=None):
    split_in = isinstance(x, tuple)
    xs = x if split_in else (x,)
    vmem = (2 * D_MODEL * D_FF + D_FF * D_MODEL) * 2 + 4 * ROW_TILE * D_MODEL * 4 \
        + ROW_TILE * D_FF * 2 + 8 * ROW_TILE * FF_CHUNK * 4 + 6 * ROW_TILE * D_MODEL * 4
    pair_specs = [_prompt_row_spec(D_MODEL), _sample_row_spec(D_MODEL)]
    scratch = [pltpu.VMEM((ROW_TILE, D_FF), BF16)]
    if split_out:
        out_specs = [_prompt_row_spec(D_MODEL), _sample_row_spec(D_MODEL)]
        out_shape = [jax.ShapeDtypeStruct((N_PROMPT, D_MODEL), F32),
                     jax.ShapeDtypeStruct((N_SAMPLE, D_MODEL), F32)]
    else:
        out_specs = [_row_spec(D_MODEL)]
        out_shape = [jax.ShapeDtypeStruct((N_TOK, D_MODEL), F32)]
    cast_in_specs = []
    for arr, lead in casts:
        rows, cols = arr.shape[-2:]
        blk = rows // CAST_STEPS
        assert blk * CAST_STEPS == rows and blk % 16 == 0 and CAST_STEPS <= N_TOK // ROW_TILE
        cast_in_specs.append(pl.BlockSpec(
            (None,) * len(lead) + (blk, cols),
            lambda i, lead=lead: tuple(lead) + (jnp.minimum(i, CAST_STEPS - 1), 0)))
        out_specs.append(pl.BlockSpec((blk, cols), lambda i: (jnp.minimum(i, CAST_STEPS - 1), 0)))
        out_shape.append(jax.ShapeDtypeStruct((rows, cols), BF16))
        vmem += 2 * blk * cols * (4 + 2)
    ada_specs, ada_args = [], []
    if ada is not None:
        cond, w_ada, b_ada, layer = ada
        n_out = 9 * D_MODEL
        last_col = n_out // ADA_SIDE_TILE - 1
        assert last_col < N_TOK // ROW_TILE
        ada_specs, ada_out_spec = _ada_specs(layer, ADA_SIDE_TILE, lambda i: jnp.minimum(i, last_col))
        ada_args = [cond, w_ada, b_ada.reshape(DEPTH, 1, n_out)]
        out_specs.append(ada_out_spec)
        out_shape.append(jax.ShapeDtypeStruct((SUBLANES, n_out), F32))
        vmem += 2 * D_MODEL * ADA_SIDE_TILE * 4 + D_MODEL * ADA_SIDE_TILE * 2
    res = pl.pallas_call(
        functools.partial(_ffn_kernel, slot=slot, split_in=split_in, split_out=split_out, n_casts=len(casts),
                          with_ada=ada is not None),
        grid=(N_TOK // ROW_TILE,),
        in_specs=(pair_specs if split_in else [_row_spec(D_MODEL)]) + [
            _mod_spec(),
            _const_spec((1, D_MODEL)),
            _const_spec((1, D_MODEL)),
            _const_spec((D_MODEL, 2 * D_FF)),
            _const_spec((D_FF, D_MODEL)),
        ] + cast_in_specs + ada_specs,
        out_specs=out_specs,
        out_shape=out_shape,
        scratch_shapes=scratch,
        compiler_params=pltpu.CompilerParams(
            dimension_semantics=("arbitrary",), vmem_limit_bytes=_vmem_limit(vmem)),
        name="ffn_sublayer",
    )(*xs, mod, g_pre.reshape(1, D_MODEL), g_post.reshape(1, D_MODEL), w_in, w_out, *[a for a, _ in casts],
      *ada_args)
    n_stream = 2 if split_out else 1
    stream = tuple(res[:n_stream]) if split_out else res[0]
    side = list(res[n_stream:])
    if ada is not None:
        side[-1] = _mod_rows(side[-1])
    return stream, side


def _qkv_kernel(x_ref, mod_ref, gpre_ref, w_ref, cos_ref, sin_ref, q_ref, kp_ref, vp_ref, k_ref, v_ref):
    i = pl.program_id(0)
    scale = HEAD_DIM ** -0.5
    h = _mod_in(x_ref[...], gpre_ref[...], mod_ref, 1).astype(BF16)
    qkv = _dot(h, w_ref[...])
    nq = N_Q_HEADS * HEAD_DIM
    nk = N_KV_HEADS * HEAD_DIM

    @pl.when(i < PROMPT_TILES)
    def _():
        q_ref[...] = (qkv[:, :nq] * scale).astype(BF16)
        kp_ref[...] = qkv[:, nq:nq + nk]
        vp_ref[...] = qkv[:, nq + nk:]

    @pl.when(i >= PROMPT_TILES)
    def _():
        v_ref[...] = qkv[:, nq + nk:]
        cos = cos_ref[...]
        sin = sin_ref[...]
        lane = lax.broadcasted_iota(jnp.int32, (ROW_TILE, LANES), 1)
        first = (lane & (HEAD_DIM // 4)) == 0

        def rope(xg):
            up = pltpu.roll(xg, LANES - HEAD_DIM // 4, 1)
            down = pltpu.roll(xg, HEAD_DIM // 4, 1)
            return xg * cos + jnp.where(first, up, down) * sin

        for j in range(nq // LANES):
            q_ref[:, j * LANES:(j + 1) * LANES] = (rope(qkv[:, j * LANES:(j + 1) * LANES]) * scale).astype(BF16)
        for j in range(nk // LANES):
            k_ref[:, j * LANES:(j + 1) * LANES] = rope(qkv[:, nq + j * LANES:nq + (j + 1) * LANES])


def _rope_tables():
    t = jnp.arange(DEC_SEQ)
    row = (t // GRID_W).astype(F32)
    col = (t % GRID_W).astype(F32)
    nf = HEAD_DIM // 4
    inv = ROPE_BASE ** (-jnp.arange(nf, dtype=F32) / nf)
    ar = row[:, None] * inv[None, :]
    ac = col[:, None] * inv[None, :]
    cos = jnp.concatenate([jnp.cos(ar), jnp.cos(ar), jnp.cos(ac), jnp.cos(ac)], axis=-1)
    sin = jnp.concatenate([-jnp.sin(ar), jnp.sin(ar), -jnp.sin(ac), jnp.sin(ac)], axis=-1)
    reps = LANES // HEAD_DIM
    return jnp.tile(cos, (1, reps)), jnp.tile(sin, (1, reps))


def _qkv(x, mod, g_pre, w_qkv, lead):
    cos, sin = _rope_tables()
    tab_spec = pl.BlockSpec(
        (ROW_TILE, LANES), lambda i: (jnp.maximum(i - PROMPT_TILES, 0) % TILES_PER_SAMPLE, 0))
    nk = N_KV_HEADS * HEAD_DIM
    vmem = D_MODEL * QKV_DIM * 2 + 6 * ROW_TILE * D_MODEL * 4 + 4 * ROW_TILE * QKV_DIM * 4
    return pl.pallas_call(
        _qkv_kernel,
        grid=(N_TOK // ROW_TILE,),
        in_specs=[
            _row_spec(D_MODEL),
            _mod_spec(),
            _const_spec((1, D_MODEL)),
            _const_spec((D_MODEL, QKV_DIM), lead),
            tab_spec,
            tab_spec,
        ],
        out_specs=[_row_spec(D_MODEL), _prompt_row_spec(nk), _prompt_row_spec(nk),
                   _sample_row_spec(nk), _sample_row_spec(nk)],
        out_shape=[
            jax.ShapeDtypeStruct((N_TOK, D_MODEL), BF16),
            jax.ShapeDtypeStruct((N_PROMPT, nk), F32),
            jax.ShapeDtypeStruct((N_PROMPT, nk), F32),
            jax.ShapeDtypeStruct((N_SAMPLE, nk), F32),
            jax.ShapeDtypeStruct((N_SAMPLE, nk), F32),
        ],
        compiler_params=pltpu.CompilerParams(
            dimension_semantics=("arbitrary",), vmem_limit_bytes=_vmem_limit(vmem)),
        name="attn_qkv",
    )(x, mod, g_pre.reshape(1, D_MODEL), w_qkv, cos, sin)


def _lane_halves(ref_or_val, hkv, rows):
    grp = ref_or_val[:, (hkv // 2) * LANES:(hkv // 2 + 1) * LANES]
    lane = lax.broadcasted_iota(jnp.int32, (rows, LANES), 1)
    in_low = lane < HEAD_DIM
    if hkv % 2 == 0:
        lo = jnp.where(in_low, grp, 0.0)
        hi = pltpu.roll(lo, HEAD_DIM, 1)
    else:
        hi = jnp.where(in_low, 0.0, grp)
        lo = pltpu.roll(hi, HEAD_DIM, 1)
    return lo.astype(BF16), hi.astype(BF16)


def _attend(sink_ref, q_ref, keys, vals, n_keys, valid):
    group = N_Q_HEADS // N_KV_HEADS
    k_halves = [_lane_halves(keys, hkv, n_keys) for hkv in range(N_KV_HEADS)]
    v_halves = [_lane_halves(vals, hkv, n_keys) for hkv in range(N_KV_HEADS)]
    scores = []
    for h in range(N_Q_HEADS):
        s = _dot_nt(q_ref[:, (h // 2) * LANES:(h // 2 + 1) * LANES], k_halves[h // group][h % 2])
        scores.append(s if valid is None else jnp.where(valid, s, MASK_VALUE))
    heads = range(N_Q_HEADS)
    maxes = [jnp.maximum(jnp.max(scores[h], axis=-1, keepdims=True), sink_ref[h]) for h in heads]
    exps = [jnp.exp(scores[h] - maxes[h]) for h in heads]
    denoms = [jnp.sum(exps[h], axis=-1, keepdims=True) + jnp.exp(sink_ref[h] - maxes[h]) for h in heads]
    probs = [(exps[h] * (1.0 / denoms[h])).astype(BF16) for h in heads]
    out = []
    for j in range(N_Q_HEADS // 2):
        v_lo, v_hi = v_halves[(2 * j) // group]
        out.append((_dot(probs[2 * j], v_lo) + _dot(probs[2 * j + 1], v_hi)).astype(BF16))
    return jnp.concatenate(out, axis=1)


def _mixer_finish(o, x_ref, mod_ref, gpost_ref, w_ref, out_ref):
    out_ref[...] = _mod_out(x_ref[...], _dot(o, w_ref[...]), gpost_ref[...], mod_ref, 1, 1.0)


def _mixer_specs(rows, x_index, group_index):
    return [
        pl.BlockSpec((rows, D_MODEL), x_index),
        pl.BlockSpec((None, 9, D_MODEL), group_index),
        _const_spec((1, D_MODEL)),
        _const_spec((D_MODEL, D_MODEL)),
    ]


def _attn_prompt_kernel(sink_ref, q_ref, k_ref, v_ref, x_ref, mod_ref, gpost_ref, w_ref, out_ref):
    o = _attend(sink_ref, q_ref, k_ref[...], v_ref[...], SEQ, None)
    _mixer_finish(o, x_ref, mod_ref, gpost_ref, w_ref, out_ref)


def _attn_prompt(sink, q, k, v, x, mod, g_post, w_out):
    nk = N_KV_HEADS * HEAD_DIM
    return pl.pallas_call(
        _attn_prompt_kernel,
        grid=(BATCH,),
        in_specs=[
            pl.BlockSpec(memory_space=pltpu.SMEM),
            pl.BlockSpec((SEQ, D_MODEL), lambda b: (b, 0)),
            pl.BlockSpec((SEQ, nk), lambda b: (b, 0)),
            pl.BlockSpec((SEQ, nk), lambda b: (b, 0)),
        ] + _mixer_specs(SEQ, lambda b: (b, 0), lambda b: (0, 0, 0)),
        out_specs=pl.BlockSpec((SEQ, D_MODEL), lambda b: (b, 0)),
        out_shape=jax.ShapeDtypeStruct((N_PROMPT, D_MODEL), F32),
        compiler_params=pltpu.CompilerParams(
            dimension_semantics=("parallel",), vmem_limit_bytes=_vmem_limit(32 * 1024 * 1024)),
        name="attn_prompt",
    )(sink, q, k, v, x, mod, g_post.reshape(1, D_MODEL), w_out)


def _attn_sample_kernel(sink_ref, q_ref, k_ref, v_ref, ck_ref, cv_ref,
                        x_ref, mod_ref, gpost_ref, w_ref, out_ref):
    qb = pl.program_id(1)
    nblk = DEC_SEQ // ATTN_BLOCK
    B = ATTN_BLOCK
    starts = [
        pl.multiple_of(jnp.maximum(qb - 1, 0) * B, B),
        pl.multiple_of(qb * B, B),
        pl.multiple_of(jnp.minimum(qb + 1, nblk - 1) * B, B),
    ]
    keys = jnp.concatenate([k_ref[pl.ds(s, B), :] for s in starts] + [ck_ref[...]], axis=0)
    vals = jnp.concatenate([v_ref[pl.ds(s, B), :] for s in starts] + [cv_ref[...]], axis=0)
    n_keys = 3 * B + PAST_LEN
    r = lax.broadcasted_iota(jnp.int32, (B, n_keys), 0)
    c = lax.broadcasted_iota(jnp.int32, (B, n_keys), 1)
    prev_bad = (c < B) & ((c < r) | (qb == 0))
    next_bad = (c >= 2 * B) & (c < 3 * B) & (((c - 2 * B) > r) | (qb == nblk - 1))
    valid = jnp.logical_not(prev_bad | next_bad)
    o = _attend(sink_ref, q_ref, keys, vals, n_keys, valid)
    _mixer_finish(o, x_ref, mod_ref, gpost_ref, w_ref, out_ref)


def _attn_sample(sink, q, k, v, cache_k, cache_v, x, mod, g_post, w_out):
    nk = N_KV_HEADS * HEAD_DIM
    nblk = DEC_SEQ // ATTN_BLOCK
    q_off = N_PROMPT // ATTN_BLOCK
    return pl.pallas_call(
        _attn_sample_kernel,
        grid=(DEC_BATCH, nblk),
        in_specs=[
            pl.BlockSpec(memory_space=pltpu.SMEM),
            pl.BlockSpec((ATTN_BLOCK, D_MODEL), lambda b, t: (q_off + b * nblk + t, 0)),
            pl.BlockSpec((DEC_SEQ, nk), lambda b, t: (b, 0)),
            pl.BlockSpec((DEC_SEQ, nk), lambda b, t: (b, 0)),
            pl.BlockSpec((None, PAST_LEN, nk), lambda b, t: (b, 0, 0)),
            pl.BlockSpec((None, PAST_LEN, nk), lambda b, t: (b, 0, 0)),
        ] + _mixer_specs(ATTN_BLOCK, lambda b, t: (q_off + b * nblk + t, 0), lambda b, t: (1 + b, 0, 0)),
        out_specs=pl.BlockSpec((ATTN_BLOCK, D_MODEL), lambda b, t: (b * nblk + t, 0)),
        out_shape=jax.ShapeDtypeStruct((N_SAMPLE, D_MODEL), F32),
        compiler_params=pltpu.CompilerParams(
            dimension_semantics=("parallel", "parallel"), vmem_limit_bytes=_vmem_limit(32 * 1024 * 1024)),
        name="attn_sample",
    )(sink, q, k, v, cache_k, cache_v, x, mod, g_post.reshape(1, D_MODEL), w_out)


def _mix_out_sample_kernel(o_ref, x_ref, mod_ref, gpost_ref, w_ref, out_ref):
    _mixer_finish(o_ref[...], x_ref, mod_ref, gpost_ref, w_ref, out_ref)


def _mix_out_sample(o_sample, x, mod, g_post, w):
    vmem = D_MODEL * D_MODEL * 2 + 12 * ROW_TILE * D_MODEL * 4
    return pl.pallas_call(
        _mix_out_sample_kernel,
        grid=(N_SAMPLE // ROW_TILE,),
        in_specs=[pl.BlockSpec((ROW_TILE, D_MODEL), lambda i: (i, 0))] + _mixer_specs(
            ROW_TILE, lambda i: (PROMPT_TILES + i, 0), lambda i: (1 + i // TILES_PER_SAMPLE, 0, 0)),
        out_specs=pl.BlockSpec((ROW_TILE, D_MODEL), lambda i: (i, 0)),
        out_shape=jax.ShapeDtypeStruct((N_SAMPLE, D_MODEL), F32),
        compiler_params=pltpu.CompilerParams(
            dimension_semantics=("parallel",), vmem_limit_bytes=_vmem_limit(vmem)),
        name="mixer_out_sample",
    )(o_sample, x, mod, g_post.reshape(1, D_MODEL), w)


def _chunk_masks():
    r = lax.broadcasted_iota(jnp.int32, (REC_GROUP, REC_GROUP), 0)
    c = lax.broadcasted_iota(jnp.int32, (REC_GROUP, REC_GROUP), 1)
    same = (r // CHUNK) == (c // CHUNK)
    return same & (c <= r), same & (c >= r)


def _rec_in_kernel(x_ref, mod_ref, gpre_ref, w_ref, lbf_ref, lbb_ref,
                   qdf_ref, kif_ref, kef_ref, qdb_ref, kib_ref, keb_ref, v_ref, sg_ref, decf_ref, decb_ref):
    G = REC_GROUP
    CB = REC_IN_COLS
    h = _mod_in(x_ref[...], gpre_ref[...], mod_ref, 1).astype(BF16)
    ri = lax.broadcasted_iota(jnp.int32, (CHUNK, CHUNK), 0)
    ci = lax.broadcasted_iota(jnp.int32, (CHUNK, CHUNK), 1)
    tris = tuple(jnp.concatenate([jnp.where(m, 1.0, 0.0).astype(BF16)] * 3, axis=1) for m in (ci <= ri, ci >= ri))
    outs = ((qdf_ref, kif_ref, kef_ref, decf_ref, lbf_ref), (qdb_ref, kib_ref, keb_ref, decb_ref, lbb_ref))

    def project(cb, part):
        return _dot(h, w_ref[:, part * D_MODEL + cb * CB:part * D_MODEL + (cb + 1) * CB])

    def finish_qvg(cb, yq, yv, yg):
        cols = slice(cb * CB, (cb + 1) * CB)
        v_ref[:, cols] = yv.astype(BF16)
        sg_ref[:, cols] = yg * _sigmoid(yg)
        return yq * _sigmoid(yq) * (REC_DK ** -0.5)

    def finish_dir(cb, d, z, qf):
        cols = slice(cb * CB, (cb + 1) * CB)
        qd_ref, ki_ref, ke_ref, dec_ref, lb_ref = outs[d]
        lb = lb_ref[:, cols]
        one_m_lb = 1.0 - lb
        e = jnp.exp(-jnp.abs(z))
        rcp = 1.0 / (1.0 + e)
        small = e * rcp
        pos = z >= 0
        key = one_m_lb * jnp.where(pos, small, rcp)
        logf = jnp.log(lb + one_m_lb * jnp.where(pos, rcp, small))
        hi = logf.astype(BF16)
        rest = logf - hi.astype(F32)
        mid = rest.astype(BF16)
        lo = (rest - mid.astype(F32)).astype(BF16)
        edge = CHUNK - 1 if d == 0 else 0
        for c in range(ROW_TILE // CHUNK):
            rows = slice(c * CHUNK, (c + 1) * CHUNK)
            bc = _dot(tris[d], jnp.concatenate([hi[rows], mid[rows], lo[rows]], axis=0))
            decay = jnp.exp(bc)
            k_inv = key[rows] * (1.0 / decay)
            chunk_decay = decay[edge:edge + 1, :]
            qd_ref[rows, cols] = (qf[rows] * decay).astype(BF16)
            ki_ref[rows, cols] = k_inv.astype(BF16)
            ke_ref[rows, cols] = (k_inv * chunk_decay).astype(BF16)
            dec_ref[c // (G // CHUNK), c % (G // CHUNK):c % (G // CHUNK) + 1, cols] = chunk_decay

    n_blocks = D_MODEL // CB
    cur = [project(0, part) for part in range(5)]
    for cb in range(n_blocks):
        more = cb + 1 < n_blocks
        nxt = [None] * 5
        if more:
            nxt[0], nxt[1] = project(cb + 1, 0), project(cb + 1, 1)
        qf = finish_qvg(cb, cur[0], cur[1], cur[4])
        if more:
            nxt[2], nxt[3] = project(cb + 1, 2), project(cb + 1, 3)
        finish_dir(cb, 0, cur[2], qf)
        if more:
            nxt[4] = project(cb + 1, 4)
        finish_dir(cb, 1, cur[3], qf)
        cur = nxt


def _rec_in(x, mod, g_pre, w, lead, lb_f, lb_b):
    n_chunks = REC_GROUP // CHUNK
    groups_per_tile = ROW_TILE // REC_GROUP
    vmem = D_MODEL * REC_IN_DIM * 2 + 4 * ROW_TILE * D_MODEL * 4 + 2 * ROW_TILE * D_MODEL * (7 * 2 + 4) \
        + 12 * ROW_TILE * D_MODEL * 4
    act = jax.ShapeDtypeStruct((N_TOK, D_MODEL), BF16)
    dec = jax.ShapeDtypeStruct((N_TOK // REC_GROUP, n_chunks, D_MODEL), F32)
    dec_spec = pl.BlockSpec((groups_per_tile, n_chunks, D_MODEL), lambda i: (i, 0, 0))
    return pl.pallas_call(
        _rec_in_kernel,
        grid=(N_TOK // ROW_TILE,),
        in_specs=[
            _row_spec(D_MODEL),
            _mod_spec(),
            _const_spec((1, D_MODEL)),
            _const_spec((D_MODEL, REC_IN_DIM), lead),
            _const_spec((1, D_MODEL)),
            _const_spec((1, D_MODEL)),
        ],
        out_specs=[_row_spec(D_MODEL) for _ in range(8)] + [dec_spec, dec_spec],
        out_shape=[act] * 7 + [jax.ShapeDtypeStruct((N_TOK, D_MODEL), F32), dec, dec],
        compiler_params=pltpu.CompilerParams(
            dimension_semantics=("parallel",), vmem_limit_bytes=_vmem_limit(vmem)),
        name="rec_in",
    )(x, mod, g_pre.reshape(1, D_MODEL), w, lb_f.reshape(1, D_MODEL), lb_b.reshape(1, D_MODEL))


def _rec_kernel(*refs, seq_len, n_heads, has_init):
    qdf_ref, kif_ref, kef_ref, qdb_ref, kib_ref, keb_ref, v_ref, sg_ref, decf_ref, decb_ref, gn_ref = refs[:11]
    if has_init:
        s0_ref, o_ref = refs[11:]
        st_ref = mixer_refs = None
    else:
        *mixer_refs, out_ref, st_ref, o_ref = refs[11:]
        s0_ref = None
    G = REC_GROUP
    per_group = G // CHUNK
    n_chunks = seq_len // CHUNK
    mask_f, mask_b = _chunk_masks()

    def rows_of(c):
        return slice(c * CHUNK, (c + 1) * CHUNK)

    heads = range(n_heads)
    cols = [slice(hh * REC_DK, (hh + 1) * REC_DK) for hh in heads]
    groups = range(seq_len // G)

    def group_rows(g):
        return slice(g * G, (g + 1) * G)

    upd = [[_dot_tn(v_ref[rows_of(c), cs],
                    jnp.concatenate([kef_ref[rows_of(c), cs], keb_ref[rows_of(c), cs]], axis=1))
            for c in range(n_chunks)] for cs in cols]
    a = [[(jnp.where(mask_f, _dot_nt(qdf_ref[group_rows(g), cs], kif_ref[group_rows(g), cs]), 0.0)
           + jnp.where(mask_b, _dot_nt(qdb_ref[group_rows(g), cs], kib_ref[group_rows(g), cs]), 0.0)
           ).astype(BF16) for g in groups] for cs in cols]
    enter = []
    for hh in heads:
        cs = cols[hh]
        if has_init:
            s_f, s_b = s0_ref[0, hh].T, s0_ref[1, hh].T
        else:
            s_f = s_b = jnp.zeros((REC_DV, REC_DK), F32)
        ent = [None] * n_chunks
        for c in range(n_chunks):
            ent[c] = s_f.astype(BF16)
            s_f = s_f * decf_ref[c // per_group, c % per_group:c % per_group + 1, cs] + upd[hh][c][:, :REC_DK]
        for c in range(n_chunks - 1, -1, -1):
            ent[c] = jnp.concatenate([ent[c], s_b.astype(BF16)], axis=1)
            s_b = s_b * decb_ref[c // per_group, c % per_group:c % per_group + 1, cs] + upd[hh][c][:, REC_DK:]
        enter.append(ent)
        if st_ref is not None:
            st_ref[0, hh] = s_f.T
            st_ref[1, hh] = s_b.T
    o_intra = [[_dot(a[hh][g], v_ref[group_rows(g), cols[hh]]) for g in groups] for hh in heads]
    for hh in heads:
        cs = cols[hh]
        for c in range(n_chunks):
            q_cat = jnp.concatenate([qdf_ref[rows_of(c), cs], qdb_ref[rows_of(c), cs]], axis=1)
            o_c = o_intra[hh][c // per_group][rows_of(c % per_group)] + _dot_nt(q_cat, enter[hh][c])
            o_ref[rows_of(c), cs] = (_rms(o_c, gn_ref[hh]) * sg_ref[rows_of(c), cs]).astype(BF16)
    if mixer_refs is not None:
        _mixer_finish(o_ref[...], *mixer_refs, out_ref)


def _rec(rec_acts, g_norm, s0, mixer, *, seq_len, n_seq, row_block_off):
    hb = min(REC_HEADS, REC_STEP_ROWS_X_HEADS // seq_len)
    w = hb * REC_DK
    n_hp = REC_HEADS // hb
    n_groups = seq_len // REC_GROUP
    has_init = s0 is not None

    def act_spec():
        return pl.BlockSpec((seq_len, w), lambda b, h: (row_block_off + b, h))

    def dec_spec():
        return pl.BlockSpec((n_groups, REC_GROUP // CHUNK, w), lambda b, h: (row_block_off + b, 0, h))

    in_specs = [act_spec() for _ in range(8)] + [dec_spec(), dec_spec(),
                                                 pl.BlockSpec((hb, 1, REC_DV), lambda b, h: (h, 0, 0))]
    args = list(rec_acts) + [g_norm.reshape(REC_HEADS, 1, REC_DV)]
    state_spec = pl.BlockSpec((None, 2, hb, REC_DK, REC_DV), lambda b, h: (b, 0, h, 0, 0))
    scratch = []
    if has_init:
        in_specs.append(state_spec)
        args.append(s0)
        out_specs = pl.BlockSpec((seq_len, w), lambda b, h: (b, h))
        out_shape = jax.ShapeDtypeStruct((n_seq * seq_len, D_MODEL), BF16)
    else:
        assert n_hp == 1
        x, mod, g_post, w_out = mixer
        in_specs += _mixer_specs(seq_len, lambda b, h: (row_block_off + b, 0), lambda b, h: (0, 0, 0))
        args += [x, mod, g_post.reshape(1, D_MODEL), w_out]
        out_specs = [pl.BlockSpec((seq_len, D_MODEL), lambda b, h: (b, 0)), state_spec]
        out_shape = [jax.ShapeDtypeStruct((n_seq * seq_len, D_MODEL), F32),
                     jax.ShapeDtypeStruct((n_seq, 2, REC_HEADS, REC_DK, REC_DV), F32)]
        scratch = [pltpu.VMEM((seq_len, D_MODEL), BF16)]
    return pl.pallas_call(
        functools.partial(_rec_kernel, seq_len=seq_len, n_heads=hb, has_init=has_init),
        grid=(n_seq, n_hp),
        in_specs=in_specs,
        out_specs=out_specs,
        out_shape=out_shape,
        scratch_shapes=scratch,
        compiler_params=pltpu.CompilerParams(
            dimension_semantics=("parallel", "parallel"),
            vmem_limit_bytes=_vmem_limit(40 * seq_len * w * 4)),
        name="rec_scan_init" if has_init else "rec_scan",
    )(*args)


def kernel(x_prompt, x_sample, c, cache_k, cache_v, state_s, c_ctx, w_ada, b_ada, norm_pre, norm_post,
           w_ffn_in, w_ffn_out, w_qkv, w_attn_out, attn_sink, w_rec_in, rec_lb_logits, rec_norm, w_rec_out):
    x = (x_prompt.reshape(N_PROMPT, D_MODEL), x_sample.reshape(N_SAMPLE, D_MODEL))
    cond = jnp.concatenate([c_ctx[None], c, jnp.zeros((SUBLANES - N_COND, D_MODEL), F32)], axis=0)
    mod = _ada_mod(cond, w_ada, b_ada, 0)

    lb_soft = jax.nn.softmax(rec_lb_logits.astype(F32), axis=1)
    lb_all = jnp.cumsum(lb_soft, axis=1) - lb_soft[:, :1]

    nk = N_KV_HEADS * HEAD_DIM
    ffn_w = (w_ffn_in[0, 0].astype(BF16), w_ffn_out[0, 0].astype(BF16))
    new_k = new_v = new_s = None
    for i in range(DEPTH):
        j = i // 2
        mixer_params = ((w_qkv, (j,)), (w_attn_out, (j,))) if i % 2 == 0 else ((w_rec_in, (j,)), (w_rec_out, (j,)))
        x, cast = _ffn(x, mod, norm_pre[i, 0], norm_post[i, 0], *ffn_w, 0,
                       casts=mixer_params + ((w_ffn_in, (i, 1)), (w_ffn_out, (i, 1))))
        w_mix_in, w_mix_out, ffn_w = cast[0], cast[1], tuple(cast[2:])
        if i % 2 == 0:
            q, k_p, v_p, k_s, v_s = _qkv(x, mod, norm_pre[i, 1], w_mix_in, ())
            x_p = _attn_prompt(attn_sink[j], q, k_p, v_p, x, mod, norm_post[i, 1], w_mix_out)
            x_s = _attn_sample(attn_sink[j], q, k_s, v_s,
                               cache_k[:, j].reshape(DEC_BATCH, PAST_LEN, nk),
                               cache_v[:, j].reshape(DEC_BATCH, PAST_LEN, nk),
                               x, mod, norm_post[i, 1], w_mix_out)
            new_k = k_p.reshape(BATCH, 1, SEQ, N_KV_HEADS, HEAD_DIM)
            new_v = v_p.reshape(BATCH, 1, SEQ, N_KV_HEADS, HEAD_DIM)
        else:
            acts = _rec_in(x, mod, norm_pre[i, 1], w_mix_in, (), lb_all[0, i], lb_all[1, i])
            x_p, s_p = _rec(acts, rec_norm[j], None, (x, mod, norm_post[i, 1], w_mix_out),
                            seq_len=SEQ, n_seq=BATCH, row_block_off=0)
            o_s = _rec(acts, rec_norm[j], state_s[:, j], None,
                       seq_len=DEC_SEQ, n_seq=DEC_BATCH, row_block_off=N_PROMPT // DEC_SEQ)
            x_s = _mix_out_sample(o_s, x, mod, norm_post[i, 1], w_mix_out)
            new_s = s_p.reshape(BATCH, 1, 2, REC_HEADS, REC_DK, REC_DV)
        last = i == DEPTH - 1
        x, cast = _ffn((x_p, x_s), mod, norm_pre[i, 2], norm_post[i, 2], *ffn_w, 2, split_out=last,
                       casts=() if last else ((w_ffn_in, (i + 1, 0)), (w_ffn_out, (i + 1, 0))),
                       ada=None if last else (cond, w_ada, b_ada, i + 1))
        if not last:
            ffn_w, mod = tuple(cast[:2]), cast[2]
    y_prompt, y_sample = x
    return (y_prompt.reshape(BATCH, SEQ, D_MODEL), y_sample.reshape(DEC_BATCH, DEC_SEQ, D_MODEL),
            new_k, new_v, new_s)
```

```python
import functools

import jax
import jax.numpy as jnp
from jax import lax
from jax.experimental import pallas as pl
from jax.experimental.pallas import tpu as pltpu

F32 = jnp.float32
BF16 = jnp.bfloat16

D_MODEL = 1024
BATCH = 32
SEQ = 256
DEPTH = 2
DEC_BATCH = 2
DEC_SEQ = 1024
PAST_LEN = 256
GRID_W = 64
HEAD_DIM = 64
N_Q_HEADS = 16
N_KV_HEADS = 4
QKV_DIM = (N_Q_HEADS + 2 * N_KV_HEADS) * HEAD_DIM
ATTN_BLOCK = 128
ROPE_BASE = 10000.0
REC_HEADS = 8
REC_DK = 128
REC_DV = 128
REC_IN_DIM = 5 * D_MODEL
CHUNK = 64
D_FF = 2816
EPS = 1e-6
MASK_VALUE = -1e30

N_PROMPT = BATCH * SEQ
N_SAMPLE = DEC_BATCH * DEC_SEQ
N_TOK = N_PROMPT + N_SAMPLE
N_COND = 1 + DEC_BATCH

LANES = 128
SUBLANES = 8
VMEM_BYTES_V7X = 64 * 1024 * 1024

ROW_TILE = 512
FF_CHUNK = 256
ADA_TILE = 1024
ADA_SIDE_TILE = 512
REC_GROUP = 256
REC_STEP_ROWS_X_HEADS = 4096
REC_IN_COLS = 256
FFN_SUBTILES = 2
CAST_STEPS = 16

PROMPT_TILES = N_PROMPT // ROW_TILE
TILES_PER_SAMPLE = DEC_SEQ // ROW_TILE


def _vmem_limit(nbytes):
    return int(min(VMEM_BYTES_V7X - 8 * 1024 * 1024, max(nbytes, 16 * 1024 * 1024)))


def _sigmoid(x):
    return 1.0 / (1.0 + jnp.exp(-x))


def _rms(x, g):
    ms = jnp.mean(x * x, axis=-1, keepdims=True)
    return x * lax.rsqrt(ms + EPS) * g


def _mod_in(x, g_pre, mod_ref, slot):
    shift = mod_ref[slot * 3:slot * 3 + 1, :]
    scale = mod_ref[slot * 3 + 1:slot * 3 + 2, :]
    return _rms(x, g_pre * (1.0 + scale)) + shift


def _mod_out(x, y, g_post, mod_ref, slot, weight):
    gate = mod_ref[slot * 3 + 2:slot * 3 + 3, :]
    return x + _rms(y, (weight * gate) * g_post)


def _dot(a, b):
    return jnp.dot(a, b, preferred_element_type=F32)


def _dot_nt(a, b):
    return lax.dot_general(a, b, (((1,), (1,)), ((), ())), preferred_element_type=F32)


def _dot_tn(a, b):
    return lax.dot_general(a, b, (((0,), (0,)), ((), ())), preferred_element_type=F32)


def _tile_group(i):
    return jnp.where(i < PROMPT_TILES, 0, 1 + (i - PROMPT_TILES) // TILES_PER_SAMPLE)


def _row_spec(width):
    return pl.BlockSpec((ROW_TILE, width), lambda i: (i, 0))


def _prompt_row_spec(width):
    return pl.BlockSpec((ROW_TILE, width), lambda i: (jnp.minimum(i, PROMPT_TILES - 1), 0))


def _sample_row_spec(width):
    return pl.BlockSpec((ROW_TILE, width), lambda i: (jnp.maximum(i - PROMPT_TILES, 0), 0))


def _mod_spec():
    return pl.BlockSpec((None, 9, D_MODEL), lambda i: (_tile_group(i), 0, 0))


def _const_spec(shape, lead=()):
    nd = len(shape)
    return pl.BlockSpec((None,) * len(lead) + tuple(shape), lambda *_: tuple(lead) + (0,) * nd,
                        pipeline_mode=pl.Buffered(1))


def _ada_block(cond_ref, w_ref, b_ref, o_ref):
    c = cond_ref[...]
    s = (c * _sigmoid(c)).astype(BF16)
    o_ref[...] = _dot(s, w_ref[...].astype(BF16)) + b_ref[...]


def _ada_specs(layer, tile, col_index):
    in_specs = [
        pl.BlockSpec((SUBLANES, D_MODEL), lambda *i: (0, 0)),
        pl.BlockSpec((None, D_MODEL, tile), lambda *i: (layer, 0, col_index(*i))),
        pl.BlockSpec((None, 1, tile), lambda *i: (layer, 0, col_index(*i))),
    ]
    return in_specs, pl.BlockSpec((SUBLANES, tile), lambda *i: (0, col_index(*i)))


def _mod_rows(out):
    return out[:N_COND].reshape(N_COND, 9, D_MODEL)


def _ada_mod(cond, w_ada, b_ada, layer):
    n_out = 9 * D_MODEL
    in_specs, out_spec = _ada_specs(layer, ADA_TILE, lambda j: j)
    out = pl.pallas_call(
        _ada_block,
        grid=(n_out // ADA_TILE,),
        in_specs=in_specs,
        out_specs=out_spec,
        out_shape=jax.ShapeDtypeStruct((SUBLANES, n_out), F32),
        compiler_params=pltpu.CompilerParams(
            dimension_semantics=("parallel",),
            vmem_limit_bytes=_vmem_limit(4 * D_MODEL * ADA_TILE * 4)),
        name="ada_mod",
    )(cond, w_ada, b_ada.reshape(DEPTH, 1, n_out))
    return _mod_rows(out)


def _ffn_kernel(*refs, slot, split_in, split_out, n_casts, with_ada):
    refs = list(refs)
    x_refs = [refs.pop(0) for _ in range(2 if split_in else 1)]
    mod_ref, gpre_ref, gpost_ref, win_ref, wout_ref = (refs.pop(0) for _ in range(5))
    cast_in = [refs.pop(0) for _ in range(n_casts)]
    ada_in = [refs.pop(0) for _ in range(3 if with_ada else 0)]
    o_refs = [refs.pop(0) for _ in range(2 if split_out else 1)]
    cast_out = [refs.pop(0) for _ in range(n_casts)]
    ada_out = refs.pop(0) if with_ada else None
    g_scr = refs.pop(0)
    is_prompt = pl.program_id(0) < PROMPT_TILES
    if split_in:
        x = jnp.where(is_prompt, x_refs[0][...], x_refs[1][...])
    else:
        x = x_refs[0][...]
    parts = []
    sub = ROW_TILE // FFN_SUBTILES
    for s in range(FFN_SUBTILES):
        rows = slice(s * sub, (s + 1) * sub)
        xs = x[rows]
        h = _mod_in(xs, gpre_ref[...], mod_ref, slot).astype(BF16)
        for c in range(D_FF // FF_CHUNK):
            lo = c * FF_CHUNK
            a = _dot(h, win_ref[:, lo:lo + FF_CHUNK])
            b = _dot(h, win_ref[:, D_FF + lo:D_FF + lo + FF_CHUNK])
            g_scr[rows, lo:lo + FF_CHUNK] = (a * _sigmoid(a) * b).astype(BF16)
            if s == 0 and c < len(cast_in):
                cast_out[c][...] = cast_in[c][...].astype(BF16)
            if s == 0 and c == len(cast_in) and with_ada:
                _ada_block(*ada_in, ada_out)
        y = _dot(g_scr[rows, :], wout_ref[...])
        parts.append(_mod_out(xs, y, gpost_ref[...], mod_ref, slot, 0.5))
    out = jnp.concatenate(parts, axis=0)
    if split_out:
        @pl.when(is_prompt)
        def _():
            o_refs[0][...] = out

        @pl.when(jnp.logical_not(is_prompt))
        def _():
            o_refs[1][...] = out
    else:
        o_refs[0][...] = out


def _ffn(x, mod, g_pre, g_post, w_in, w_out, slot, split_out=False, casts=(), ada=None):
    split_in = isinstance(x, tuple)
    xs = x if split_in else (x,)
    vmem = (2 * D_MODEL * D_FF + D_FF * D_MODEL) * 2 + 4 * ROW_TILE * D_MODEL * 4 \
        + ROW_TILE * D_FF * 2 + 8 * ROW_TILE * FF_CHUNK * 4 + 6 * ROW_TILE * D_MODEL * 4
    pair_specs = [_prompt_row_spec(D_MODEL), _sample_row_spec(D_MODEL)]
    scratch = [pltpu.VMEM((ROW_TILE, D_FF), BF16)]
    if split_out:
        out_specs = [_prompt_row_spec(D_MODEL), _sample_row_spec(D_MODEL)]
        out_shape = [jax.ShapeDtypeStruct((N_PROMPT, D_MODEL), F32),
                     jax.ShapeDtypeStruct((N_SAMPLE, D_MODEL), F32)]
    else:
        out_specs = [_row_spec(D_MODEL)]
        out_shape = [jax.ShapeDtypeStruct((N_TOK, D_MODEL), F32)]
    cast_in_specs = []
    for arr, lead in casts:
        rows, cols = arr.shape[-2:]
        blk = rows // CAST_STEPS
        assert blk * CAST_STEPS == rows and blk % 16 == 0 and CAST_STEPS <= N_TOK // ROW_TILE
        cast_in_specs.append(pl.BlockSpec(
            (None,) * len(lead) + (blk, cols),
            lambda i, lead=lead: tuple(lead) + (jnp.minimum(i, CAST_STEPS - 1), 0)))
        out_specs.append(pl.BlockSpec((blk, cols), lambda i: (jnp.minimum(i, CAST_STEPS - 1), 0)))
        out_shape.append(jax.ShapeDtypeStruct((rows, cols), BF16))
        vmem += 2 * blk * cols * (4 + 2)
    ada_specs, ada_args = [], []
    if ada is not None:
        cond, w_ada, b_ada, layer = ada
        n_out = 9 * D_MODEL
        last_col = n_out // ADA_SIDE_TILE - 1
        assert last_col < N_TOK // ROW_TILE
        ada_specs, ada_out_spec = _ada_specs(layer, ADA_SIDE_TILE, lambda i: jnp.minimum(i, last_col))
        ada_args = [cond, w_ada, b_ada.reshape(DEPTH, 1, n_out)]
        out_specs.append(ada_out_spec)
        out_shape.append(jax.ShapeDtypeStruct((SUBLANES, n_out), F32))
        vmem += 2 * D_MODEL * ADA_SIDE_TILE * 4 + D_MODEL * ADA_SIDE_TILE * 2
    res = pl.pallas_call(
        functools.partial(_ffn_kernel, slot=slot, split_in=split_in, split_out=split_out, n_casts=len(casts),
                          with_ada=ada is not None),
        grid=(N_TOK // ROW_TILE,),
        in_specs=(pair_specs if split_in else [_row_spec(D_MODEL)]) + [
            _mod_spec(),
            _const_spec((1, D_MODEL)),
            _const_spec((1, D_MODEL)),
            _const_spec((D_MODEL, 2 * D_FF)),
            _const_spec((D_FF, D_MODEL)),
        ] + cast_in_specs + ada_specs,
        out_specs=out_specs,
        out_shape=out_shape,
        scratch_shapes=scratch,
        compiler_params=pltpu.CompilerParams(
            dimension_semantics=("arbitrary",), vmem_limit_bytes=_vmem_limit(vmem)),
        name="ffn_sublayer",
    )(*xs, mod, g_pre.reshape(1, D_MODEL), g_post.reshape(1, D_MODEL), w_in, w_out, *[a for a, _ in casts],
      *ada_args)
    n_stream = 2 if split_out else 1
    stream = tuple(res[:n_stream]) if split_out else res[0]
    side = list(res[n_stream:])
    if ada is not None:
        side[-1] = _mod_rows(side[-1])
    return stream, side


def _qkv_kernel(x_ref, mod_ref, gpre_ref, w_ref, cos_ref, sin_ref, q_ref, kp_ref, vp_ref, k_ref, v_ref):
    i = pl.program_id(0)
    scale = HEAD_DIM ** -0.5
    h = _mod_in(x_ref[...], gpre_ref[...], mod_ref, 1).astype(BF16)
    qkv = _dot(h, w_ref[...])
    nq = N_Q_HEADS * HEAD_DIM
    nk = N_KV_HEADS * HEAD_DIM

    @pl.when(i < PROMPT_TILES)
    def _():
        q_ref[...] = (qkv[:, :nq] * scale).astype(BF16)
        kp_ref[...] = qkv[:, nq:nq + nk]
        vp_ref[...] = qkv[:, nq + nk:]

    @pl.when(i >= PROMPT_TILES)
    def _():
        v_ref[...] = qkv[:, nq + nk:]
        cos = cos_ref[...]
        sin = sin_ref[...]
        lane = lax.broadcasted_iota(jnp.int32, (ROW_TILE, LANES), 1)
        first = (lane & (HEAD_DIM // 4)) == 0

        def rope(xg):
            up = pltpu.roll(xg, LANES - HEAD_DIM // 4, 1)
            down = pltpu.roll(xg, HEAD_DIM // 4, 1)
            return xg * cos + jnp.where(first, up, down) * sin

        for j in range(nq // LANES):
            q_ref[:, j * LANES:(j + 1) * LANES] = (rope(qkv[:, j * LANES:(j + 1) * LANES]) * scale).astype(BF16)
        for j in range(nk // LANES):
            k_ref[:, j * LANES:(j + 1) * LANES] = rope(qkv[:, nq + j * LANES:nq + (j + 1) * LANES])


def _rope_tables():
    t = jnp.arange(DEC_SEQ)
    row = (t // GRID_W).astype(F32)
    col = (t % GRID_W).astype(F32)
    nf = HEAD_DIM // 4
    inv = ROPE_BASE ** (-jnp.arange(nf, dtype=F32) / nf)
    ar = row[:, None] * inv[None, :]
    ac = col[:, None] * inv[None, :]
    cos = jnp.concatenate([jnp.cos(ar), jnp.cos(ar), jnp.cos(ac), jnp.cos(ac)], axis=-1)
    sin = jnp.concatenate([-jnp.sin(ar), jnp.sin(ar), -jnp.sin(ac), jnp.sin(ac)], axis=-1)
    reps = LANES // HEAD_DIM
    return jnp.tile(cos, (1, reps)), jnp.tile(sin, (1, reps))


def _qkv(x, mod, g_pre, w_qkv, lead):
    cos, sin = _rope_tables()
    tab_spec = pl.BlockSpec(
        (ROW_TILE, LANES), lambda i: (jnp.maximum(i - PROMPT_TILES, 0) % TILES_PER_SAMPLE, 0))
    nk = N_KV_HEADS * HEAD_DIM
    vmem = D_MODEL * QKV_DIM * 2 + 6 * ROW_TILE * D_MODEL * 4 + 4 * ROW_TILE * QKV_DIM * 4
    return pl.pallas_call(
        _qkv_kernel,
        grid=(N_TOK // ROW_TILE,),
        in_specs=[
            _row_spec(D_MODEL),
            _mod_spec(),
            _const_spec((1, D_MODEL)),
            _const_spec((D_MODEL, QKV_DIM), lead),
            tab_spec,
            tab_spec,
        ],
        out_specs=[_row_spec(D_MODEL), _prompt_row_spec(nk), _prompt_row_spec(nk),
                   _sample_row_spec(nk), _sample_row_spec(nk)],
        out_shape=[
            jax.ShapeDtypeStruct((N_TOK, D_MODEL), BF16),
            jax.ShapeDtypeStruct((N_PROMPT, nk), F32),
            jax.ShapeDtypeStruct((N_PROMPT, nk), F32),
            jax.ShapeDtypeStruct((N_SAMPLE, nk), F32),
            jax.ShapeDtypeStruct((N_SAMPLE, nk), F32),
        ],
        compiler_params=pltpu.CompilerParams(
            dimension_semantics=("arbitrary",), vmem_limit_bytes=_vmem_limit(vmem)),
        name="attn_qkv",
    )(x, mod, g_pre.reshape(1, D_MODEL), w_qkv, cos, sin)


def _lane_halves(ref_or_val, hkv, rows):
    grp = ref_or_val[:, (hkv // 2) * LANES:(hkv // 2 + 1) * LANES]
    lane = lax.broadcasted_iota(jnp.int32, (rows, LANES), 1)
    in_low = lane < HEAD_DIM
    if hkv % 2 == 0:
        lo = jnp.where(in_low, grp, 0.0)
        hi = pltpu.roll(lo, HEAD_DIM, 1)
    else:
        hi = jnp.where(in_low, 0.0, grp)
        lo = pltpu.roll(hi, HEAD_DIM, 1)
    return lo.astype(BF16), hi.astype(BF16)


def _attend(sink_ref, q_ref, keys, vals, n_keys, valid):
    group = N_Q_HEADS // N_KV_HEADS
    k_halves = [_lane_halves(keys, hkv, n_keys) for hkv in range(N_KV_HEADS)]
    v_halves = [_lane_halves(vals, hkv, n_keys) for hkv in range(N_KV_HEADS)]
    scores = []
    for h in range(N_Q_HEADS):
        s = _dot_nt(q_ref[:, (h // 2) * LANES:(h // 2 + 1) * LANES], k_halves[h // group][h % 2])
        scores.append(s if valid is None else jnp.where(valid, s, MASK_VALUE))
    heads = range(N_Q_HEADS)
    maxes = [jnp.maximum(jnp.max(scores[h], axis=-1, keepdims=True), sink_ref[h]) for h in heads]
    exps = [jnp.exp(scores[h] - maxes[h]) for h in heads]
    denoms = [jnp.sum(exps[h], axis=-1, keepdims=True) + jnp.exp(sink_ref[h] - maxes[h]) for h in heads]
    probs = [(exps[h] * (1.0 / denoms[h])).astype(BF16) for h in heads]
    out = []
    for j in range(N_Q_HEADS // 2):
        v_lo, v_hi = v_halves[(2 * j) // group]
        out.append((_dot(probs[2 * j], v_lo) + _dot(probs[2 * j + 1], v_hi)).astype(BF16))
    return jnp.concatenate(out, axis=1)


def _mixer_finish(o, x_ref, mod_ref, gpost_ref, w_ref, out_ref):
    out_ref[...] = _mod_out(x_ref[...], _dot(o, w_ref[...]), gpost_ref[...], mod_ref, 1, 1.0)


def _mixer_specs(rows, x_index, group_index):
    return [
        pl.BlockSpec((rows, D_MODEL), x_index),
        pl.BlockSpec((None, 9, D_MODEL), group_index),
        _const_spec((1, D_MODEL)),
        _const_spec((D_MODEL, D_MODEL)),
    ]


def _attn_prompt_kernel(sink_ref, q_ref, k_ref, v_ref, x_ref, mod_ref, gpost_ref, w_ref, out_ref):
    o = _attend(sink_ref, q_ref, k_ref[...], v_ref[...], SEQ, None)
    _mixer_finish(o, x_ref, mod_ref, gpost_ref, w_ref, out_ref)


def _attn_prompt(sink, q, k, v, x, mod, g_post, w_out):
    nk = N_KV_HEADS * HEAD_DIM
    return pl.pallas_call(
        _attn_prompt_kernel,
        grid=(BATCH,),
        in_specs=[
            pl.BlockSpec(memory_space=pltpu.SMEM),
            pl.BlockSpec((SEQ, D_MODEL), lambda b: (b, 0)),
            pl.BlockSpec((SEQ, nk), lambda b: (b, 0)),
            pl.BlockSpec((SEQ, nk), lambda b: (b, 0)),
        ] + _mixer_specs(SEQ, lambda b: (b, 0), lambda b: (0, 0, 0)),
        out_specs=pl.BlockSpec((SEQ, D_MODEL), lambda b: (b, 0)),
        out_shape=jax.ShapeDtypeStruct((N_PROMPT, D_MODEL), F32),
        compiler_params=pltpu.CompilerParams(
            dimension_semantics=("parallel",), vmem_limit_bytes=_vmem_limit(32 * 1024 * 1024)),
        name="attn_prompt",
    )(sink, q, k, v, x, mod, g_post.reshape(1, D_MODEL), w_out)


def _attn_sample_kernel(sink_ref, q_ref, k_ref, v_ref, ck_ref, cv_ref,
                        x_ref, mod_ref, gpost_ref, w_ref, out_ref):
    qb = pl.program_id(1)
    nblk = DEC_SEQ // ATTN_BLOCK
    B = ATTN_BLOCK
    starts = [
        pl.multiple_of(jnp.maximum(qb - 1, 0) * B, B),
        pl.multiple_of(qb * B, B),
        pl.multiple_of(jnp.minimum(qb + 1, nblk - 1) * B, B),
    ]
    keys = jnp.concatenate([k_ref[pl.ds(s, B), :] for s in starts] + [ck_ref[...]], axis=0)
    vals = jnp.concatenate([v_ref[pl.ds(s, B), :] for s in starts] + [cv_ref[...]], axis=0)
    n_keys = 3 * B + PAST_LEN
    r = lax.broadcasted_iota(jnp.int32, (B, n_keys), 0)
    c = lax.broadcasted_iota(jnp.int32, (B, n_keys), 1)
    prev_bad = (c < B) & ((c < r) | (qb == 0))
    next_bad = (c >= 2 * B) & (c < 3 * B) & (((c - 2 * B) > r) | (qb == nblk - 1))
    valid = jnp.logical_not(prev_bad | next_bad)
    o = _attend(sink_ref, q_ref, keys, vals, n_keys, valid)
    _mixer_finish(o, x_ref, mod_ref, gpost_ref, w_ref, out_ref)


def _attn_sample(sink, q, k, v, cache_k, cache_v, x, mod, g_post, w_out):
    nk = N_KV_HEADS * HEAD_DIM
    nblk = DEC_SEQ // ATTN_BLOCK
    q_off = N_PROMPT // ATTN_BLOCK
    return pl.pallas_call(
        _attn_sample_kernel,
        grid=(DEC_BATCH, nblk),
        in_specs=[
            pl.BlockSpec(memory_space=pltpu.SMEM),
            pl.BlockSpec((ATTN_BLOCK, D_MODEL), lambda b, t: (q_off + b * nblk + t, 0)),
            pl.BlockSpec((DEC_SEQ, nk), lambda b, t: (b, 0)),
            pl.BlockSpec((DEC_SEQ, nk), lambda b, t: (b, 0)),
            pl.BlockSpec((None, PAST_LEN, nk), lambda b, t: (b, 0, 0)),
            pl.BlockSpec((None, PAST_LEN, nk), lambda b, t: (b, 0, 0)),
        ] + _mixer_specs(ATTN_BLOCK, lambda b, t: (q_off + b * nblk + t, 0), lambda b, t: (1 + b, 0, 0)),
        out_specs=pl.BlockSpec((ATTN_BLOCK, D_MODEL), lambda b, t: (b * nblk + t, 0)),
        out_shape=jax.ShapeDtypeStruct((N_SAMPLE, D_MODEL), F32),
        compiler_params=pltpu.CompilerParams(
            dimension_semantics=("parallel", "parallel"), vmem_limit_bytes=_vmem_limit(32 * 1024 * 1024)),
        name="attn_sample",
    )(sink, q, k, v, cache_k, cache_v, x, mod, g_post.reshape(1, D_MODEL), w_out)


def _mix_out_sample_kernel(o_ref, x_ref, mod_ref, gpost_ref, w_ref, out_ref):
    _mixer_finish(o_ref[...], x_ref, mod_ref, gpost_ref, w_ref, out_ref)


def _mix_out_sample(o_sample, x, mod, g_post, w):
    vmem = D_MODEL * D_MODEL * 2 + 12 * ROW_TILE * D_MODEL * 4
    return pl.pallas_call(
        _mix_out_sample_kernel,
        grid=(N_SAMPLE // ROW_TILE,),
        in_specs=[pl.BlockSpec((ROW_TILE, D_MODEL), lambda i: (i, 0))] + _mixer_specs(
            ROW_TILE, lambda i: (PROMPT_TILES + i, 0), lambda i: (1 + i // TILES_PER_SAMPLE, 0, 0)),
        out_specs=pl.BlockSpec((ROW_TILE, D_MODEL), lambda i: (i, 0)),
        out_shape=jax.ShapeDtypeStruct((N_SAMPLE, D_MODEL), F32),
        compiler_params=pltpu.CompilerParams(
            dimension_semantics=("parallel",), vmem_limit_bytes=_vmem_limit(vmem)),
        name="mixer_out_sample",
    )(o_sample, x, mod, g_post.reshape(1, D_MODEL), w)


def _chunk_masks():
    r = lax.broadcasted_iota(jnp.int32, (REC_GROUP, REC_GROUP), 0)
    c = lax.broadcasted_iota(jnp.int32, (REC_GROUP, REC_GROUP), 1)
    same = (r // CHUNK) == (c // CHUNK)
    return same & (c <= r), same & (c >= r)


def _rec_in_kernel(x_ref, mod_ref, gpre_ref, w_ref, lbf_ref, lbb_ref,
                   qdf_ref, kif_ref, qdb_ref, kib_ref, v_ref, sg_ref, decf_ref, decb_ref):
    G = REC_GROUP
    CB = REC_IN_COLS
    h = _mod_in(x_ref[...], gpre_ref[...], mod_ref, 1).astype(BF16)
    ri = lax.broadcasted_iota(jnp.int32, (CHUNK, CHUNK), 0)
    ci = lax.broadcasted_iota(jnp.int32, (CHUNK, CHUNK), 1)
    tris = tuple(jnp.concatenate([jnp.where(m, 1.0, 0.0).astype(BF16)] * 3, axis=1) for m in (ci <= ri, ci >= ri))
    outs = ((qdf_ref, kif_ref, decf_ref, lbf_ref), (qdb_ref, kib_ref, decb_ref, lbb_ref))

    def project(cb, part):
        return _dot(h, w_ref[:, part * D_MODEL + cb * CB:part * D_MODEL + (cb + 1) * CB])

    def finish_qvg(cb, yq, yv, yg):
        cols = slice(cb * CB, (cb + 1) * CB)
        v_ref[:, cols] = yv.astype(BF16)
        sg_ref[:, cols] = yg * _sigmoid(yg)
        return yq * _sigmoid(yq) * (REC_DK ** -0.5)

    def finish_dir(cb, d, z, qf):
        cols = slice(cb * CB, (cb + 1) * CB)
        qd_ref, ki_ref, dec_ref, lb_ref = outs[d]
        lb = lb_ref[:, cols]
        one_m_lb = 1.0 - lb
        e = jnp.exp(-jnp.abs(z))
        rcp = 1.0 / (1.0 + e)
        small = e * rcp
        pos = z >= 0
        key = one_m_lb * jnp.where(pos, small, rcp)
        logf = jnp.log(lb + one_m_lb * jnp.where(pos, rcp, small))
        hi = logf.astype(BF16)
        rest = logf - hi.astype(F32)
        mid = rest.astype(BF16)
        lo = (rest - mid.astype(F32)).astype(BF16)
        edge = CHUNK - 1 if d == 0 else 0
        for c in range(ROW_TILE // CHUNK):
            rows = slice(c * CHUNK, (c + 1) * CHUNK)
            bc = _dot(tris[d], jnp.concatenate([hi[rows], mid[rows], lo[rows]], axis=0))
            decay = jnp.exp(bc)
            qd_ref[rows, cols] = (qf[rows] * decay).astype(BF16)
            ki_ref[rows, cols] = (key[rows] * (1.0 / decay)).astype(BF16)
            dec_ref[c // (G // CHUNK), c % (G // CHUNK):c % (G // CHUNK) + 1, cols] = \
                decay[edge:edge + 1, :]

    n_blocks = D_MODEL // CB
    cur = [project(0, part) for part in range(5)]
    for cb in range(n_blocks):
        more = cb + 1 < n_blocks
        nxt = [None] * 5
        if more:
            nxt[0], nxt[1] = project(cb + 1, 0), project(cb + 1, 1)
        qf = finish_qvg(cb, cur[0], cur[1], cur[4])
        if more:
            nxt[2], nxt[3] = project(cb + 1, 2), project(cb + 1, 3)
        finish_dir(cb, 0, cur[2], qf)
        if more:
            nxt[4] = project(cb + 1, 4)
        finish_dir(cb, 1, cur[3], qf)
        cur = nxt


def _rec_in(x, mod, g_pre, w, lead, lb_f, lb_b):
    n_chunks = REC_GROUP // CHUNK
    groups_per_tile = ROW_TILE // REC_GROUP
    vmem = D_MODEL * REC_IN_DIM * 2 + 4 * ROW_TILE * D_MODEL * 4 + 2 * ROW_TILE * D_MODEL * (7 * 2 + 4) \
        + 12 * ROW_TILE * D_MODEL * 4
    act = jax.ShapeDtypeStruct((N_TOK, D_MODEL), BF16)
    dec = jax.ShapeDtypeStruct((N_TOK // REC_GROUP, n_chunks, D_MODEL), F32)
    dec_spec = pl.BlockSpec((groups_per_tile, n_chunks, D_MODEL), lambda i: (i, 0, 0))
    return pl.pallas_call(
        _rec_in_kernel,
        grid=(N_TOK // ROW_TILE,),
        in_specs=[
            _row_spec(D_MODEL),
            _mod_spec(),
            _const_spec((1, D_MODEL)),
            _const_spec((D_MODEL, REC_IN_DIM), lead),
            _const_spec((1, D_MODEL)),
            _const_spec((1, D_MODEL)),
        ],
        out_specs=[_row_spec(D_MODEL) for _ in range(6)] + [dec_spec, dec_spec],
        out_shape=[act] * 5 + [jax.ShapeDtypeStruct((N_TOK, D_MODEL), F32), dec, dec],
        compiler_params=pltpu.CompilerParams(
            dimension_semantics=("parallel",), vmem_limit_bytes=_vmem_limit(vmem)),
        name="rec_in",
    )(x, mod, g_pre.reshape(1, D_MODEL), w, lb_f.reshape(1, D_MODEL), lb_b.reshape(1, D_MODEL))


def _rec_kernel(*refs, seq_len, n_heads, has_init):
    qdf_ref, kif_ref, qdb_ref, kib_ref, v_ref, sg_ref, decf_ref, decb_ref, gn_ref = refs[:9]
    if has_init:
        s0_ref, o_ref = refs[9:]
        st_ref = mixer_refs = None
    else:
        *mixer_refs, out_ref, st_ref, o_ref = refs[9:]
        s0_ref = None
    G = REC_GROUP
    per_group = G // CHUNK
    n_chunks = seq_len // CHUNK
    mask_f, mask_b = _chunk_masks()

    def rows_of(c):
        return slice(c * CHUNK, (c + 1) * CHUNK)

    heads = range(n_heads)
    cols = [slice(hh * REC_DK, (hh + 1) * REC_DK) for hh in heads]
    groups = range(seq_len // G)

    def group_rows(g):
        return slice(g * G, (g + 1) * G)

    upd = [[_dot_tn(v_ref[rows_of(c), cs],
                    jnp.concatenate([kif_ref[rows_of(c), cs], kib_ref[rows_of(c), cs]], axis=1))
            for c in range(n_chunks)] for cs in cols]
    a = [[(jnp.where(mask_f, _dot_nt(qdf_ref[group_rows(g), cs], kif_ref[group_rows(g), cs]), 0.0)
           + jnp.where(mask_b, _dot_nt(qdb_ref[group_rows(g), cs], kib_ref[group_rows(g), cs]), 0.0)
           ).astype(BF16) for g in groups] for cs in cols]
    enter = []
    for hh in heads:
        cs = cols[hh]
        if has_init:
            s_f, s_b = s0_ref[0, hh].T, s0_ref[1, hh].T
        else:
            s_f = s_b = jnp.zeros((REC_DV, REC_DK), F32)
        ent = [None] * n_chunks
        for c in range(n_chunks):
            ent[c] = s_f.astype(BF16)
            dec = decf_ref[c // per_group, c % per_group:c % per_group + 1, cs]
            s_f = s_f * dec + upd[hh][c][:, :REC_DK] * dec
        for c in range(n_chunks - 1, -1, -1):
            ent[c] = jnp.concatenate([ent[c], s_b.astype(BF16)], axis=1)
            dec = decb_ref[c // per_group, c % per_group:c % per_group + 1, cs]
            s_b = s_b * dec + upd[hh][c][:, REC_DK:] * dec
        enter.append(ent)
        if st_ref is not None:
            st_ref[0, hh] = s_f.T
            st_ref[1, hh] = s_b.T
    o_intra = [[_dot(a[hh][g], v_ref[group_rows(g), cols[hh]]) for g in groups] for hh in heads]
    for hh in heads:
        cs = cols[hh]
        for c in range(n_chunks):
            q_cat = jnp.concatenate([qdf_ref[rows_of(c), cs], qdb_ref[rows_of(c), cs]], axis=1)
            o_c = o_intra[hh][c // per_group][rows_of(c % per_group)] + _dot_nt(q_cat, enter[hh][c])
            o_ref[rows_of(c), cs] = (_rms(o_c, gn_ref[hh]) * sg_ref[rows_of(c), cs]).astype(BF16)
    if mixer_refs is not None:
        _mixer_finish(o_ref[...], *mixer_refs, out_ref)


def _rec(rec_acts, g_norm, s0, mixer, *, seq_len, n_seq, row_block_off):
    hb = min(REC_HEADS, REC_STEP_ROWS_X_HEADS // seq_len)
    w = hb * REC_DK
    n_hp = REC_HEADS // hb
    n_groups = seq_len // REC_GROUP
    has_init = s0 is not None

    def act_spec():
        return pl.BlockSpec((seq_len, w), lambda b, h: (row_block_off + b, h))

    def dec_spec():
        return pl.BlockSpec((n_groups, REC_GROUP // CHUNK, w), lambda b, h: (row_block_off + b, 0, h))

    in_specs = [act_spec() for _ in range(6)] + [dec_spec(), dec_spec(),
                                                 pl.BlockSpec((hb, 1, REC_DV), lambda b, h: (h, 0, 0))]
    args = list(rec_acts) + [g_norm.reshape(REC_HEADS, 1, REC_DV)]
    state_spec = pl.BlockSpec((None, 2, hb, REC_DK, REC_DV), lambda b, h: (b, 0, h, 0, 0))
    scratch = []
    if has_init:
        in_specs.append(state_spec)
        args.append(s0)
        out_specs = pl.BlockSpec((seq_len, w), lambda b, h: (b, h))
        out_shape = jax.ShapeDtypeStruct((n_seq * seq_len, D_MODEL), BF16)
    else:
        assert n_hp == 1
        x, mod, g_post, w_out = mixer
        in_specs += _mixer_specs(seq_len, lambda b, h: (row_block_off + b, 0), lambda b, h: (0, 0, 0))
        args += [x, mod, g_post.reshape(1, D_MODEL), w_out]
        out_specs = [pl.BlockSpec((seq_len, D_MODEL), lambda b, h: (b, 0)), state_spec]
        out_shape = [jax.ShapeDtypeStruct((n_seq * seq_len, D_MODEL), F32),
                     jax.ShapeDtypeStruct((n_seq, 2, REC_HEADS, REC_DK, REC_DV), F32)]
        scratch = [pltpu.VMEM((seq_len, D_MODEL), BF16)]
    return pl.pallas_call(
        functools.partial(_rec_kernel, seq_len=seq_len, n_heads=hb, has_init=has_init),
        grid=(n_seq, n_hp),
        in_specs=in_specs,
        out_specs=out_specs,
        out_shape=out_shape,
        scratch_shapes=scratch,
        compiler_params=pltpu.CompilerParams(
            dimension_semantics=("parallel", "parallel"),
            vmem_limit_bytes=_vmem_limit(40 * seq_len * w * 4)),
        name="rec_scan_init" if has_init else "rec_scan",
    )(*args)


def kernel(x_prompt, x_sample, c, cache_k, cache_v, state_s, c_ctx, w_ada, b_ada, norm_pre, norm_post,
           w_ffn_in, w_ffn_out, w_qkv, w_attn_out, attn_sink, w_rec_in, rec_lb_logits, rec_norm, w_rec_out):
    x = (x_prompt.reshape(N_PROMPT, D_MODEL), x_sample.reshape(N_SAMPLE, D_MODEL))
    cond = jnp.concatenate([c_ctx[None], c, jnp.zeros((SUBLANES - N_COND, D_MODEL), F32)], axis=0)
    mod = _ada_mod(cond, w_ada, b_ada, 0)

    lb_soft = jax.nn.softmax(rec_lb_logits.astype(F32), axis=1)
    lb_all = jnp.cumsum(lb_soft, axis=1) - lb_soft[:, :1]

    nk = N_KV_HEADS * HEAD_DIM
    ffn_w = (w_ffn_in[0, 0].astype(BF16), w_ffn_out[0, 0].astype(BF16))
    new_k = new_v = new_s = None
    for i in range(DEPTH):
        j = i // 2
        mixer_params = ((w_qkv, (j,)), (w_attn_out, (j,))) if i % 2 == 0 else ((w_rec_in, (j,)), (w_rec_out, (j,)))
        x, cast = _ffn(x, mod, norm_pre[i, 0], norm_post[i, 0], *ffn_w, 0,
                       casts=mixer_params + ((w_ffn_in, (i, 1)), (w_ffn_out, (i, 1))))
        w_mix_in, w_mix_out, ffn_w = cast[0], cast[1], tuple(cast[2:])
        if i % 2 == 0:
            q, k_p, v_p, k_s, v_s = _qkv(x, mod, norm_pre[i, 1], w_mix_in, ())
            x_p = _attn_prompt(attn_sink[j], q, k_p, v_p, x, mod, norm_post[i, 1], w_mix_out)
            x_s = _attn_sample(attn_sink[j], q, k_s, v_s,
                               cache_k[:, j].reshape(DEC_BATCH, PAST_LEN, nk),
                               cache_v[:, j].reshape(DEC_BATCH, PAST_LEN, nk),
                               x, mod, norm_post[i, 1], w_mix_out)
            new_k = k_p.reshape(BATCH, 1, SEQ, N_KV_HEADS, HEAD_DIM)
            new_v = v_p.reshape(BATCH, 1, SEQ, N_KV_HEADS, HEAD_DIM)
        else:
            acts = _rec_in(x, mod, norm_pre[i, 1], w_mix_in, (), lb_all[0, i], lb_all[1, i])
            x_p, s_p = _rec(acts, rec_norm[j], None, (x, mod, norm_post[i, 1], w_mix_out),
                            seq_len=SEQ, n_seq=BATCH, row_block_off=0)
            o_s = _rec(acts, rec_norm[j], state_s[:, j], None,
                       seq_len=DEC_SEQ, n_seq=DEC_BATCH, row_block_off=N_PROMPT // DEC_SEQ)
            x_s = _mix_out_sample(o_s, x, mod, norm_post[i, 1], w_mix_out)
            new_s = s_p.reshape(BATCH, 1, 2, REC_HEADS, REC_DK, REC_DV)
        last = i == DEPTH - 1
        x, cast = _ffn((x_p, x_s), mod, norm_pre[i, 2], norm_post[i, 2], *ffn_w, 2, split_out=last,
                       casts=() if last else ((w_ffn_in, (i + 1, 0)), (w_ffn_out, (i + 1, 0))),
                       ada=None if last else (cond, w_ada, b_ada, i + 1))
        if not last:
            ffn_w, mod = tuple(cast[:2]), cast[2]
    y_prompt, y_sample = x
    return (y_prompt.reshape(BATCH, SEQ, D_MODEL), y_sample.reshape(DEC_BATCH, DEC_SEQ, D_MODEL),
            new_k, new_v, new_s)
```

```python
import functools

import jax
import jax.numpy as jnp
from jax import lax
from jax.experimental import pallas as pl
from jax.experimental.pallas import tpu as pltpu

F32 = jnp.float32
BF16 = jnp.bfloat16

D_MODEL = 1024
BATCH = 32
SEQ = 256
DEPTH = 2
DEC_BATCH = 2
DEC_SEQ = 1024
PAST_LEN = 256
GRID_W = 64
HEAD_DIM = 64
N_Q_HEADS = 16
N_KV_HEADS = 4
QKV_DIM = (N_Q_HEADS + 2 * N_KV_HEADS) * HEAD_DIM
ATTN_BLOCK = 128
ROPE_BASE = 10000.0
REC_HEADS = 8
REC_DK = 128
REC_DV = 128
REC_IN_DIM = 5 * D_MODEL
CHUNK = 64
D_FF = 2816
EPS = 1e-6
MASK_VALUE = -1e30

N_PROMPT = BATCH * SEQ
N_SAMPLE = DEC_BATCH * DEC_SEQ
N_TOK = N_PROMPT + N_SAMPLE
N_COND = 1 + DEC_BATCH

LANES = 128
SUBLANES = 8
VMEM_BYTES_V7X = 64 * 1024 * 1024

ROW_TILE = 512
FF_CHUNK = 256
ADA_TILE = 1024
ADA_SIDE_TILE = 512
REC_GROUP = 256
REC_STEP_ROWS_X_HEADS = 4096
REC_IN_COLS = 256
FFN_SUBTILES = 2
CAST_STEPS = 16
ATTN_Q_BLOCKS = 2
ATTN_PROMPT_SEQS = 2

PROMPT_TILES = N_PROMPT // ROW_TILE
TILES_PER_SAMPLE = DEC_SEQ // ROW_TILE


def _vmem_limit(nbytes):
    return int(min(VMEM_BYTES_V7X - 8 * 1024 * 1024, max(nbytes, 16 * 1024 * 1024)))


def _sigmoid(x):
    return 1.0 / (1.0 + jnp.exp(-x))


def _rms(x, g):
    ms = jnp.mean(x * x, axis=-1, keepdims=True)
    return x * lax.rsqrt(ms + EPS) * g


def _mod_in(x, g_pre, mod_ref, slot):
    shift = mod_ref[slot * 3:slot * 3 + 1, :]
    scale = mod_ref[slot * 3 + 1:slot * 3 + 2, :]
    return _rms(x, g_pre * (1.0 + scale)) + shift


def _mod_out(x, y, g_post, mod_ref, slot, weight):
    gate = mod_ref[slot * 3 + 2:slot * 3 + 3, :]
    return x + _rms(y, (weight * gate) * g_post)


def _dot(a, b):
    return jnp.dot(a, b, preferred_element_type=F32)


def _dot_nt(a, b):
    return lax.dot_general(a, b, (((1,), (1,)), ((), ())), preferred_element_type=F32)


def _dot_tn(a, b):
    return lax.dot_general(a, b, (((0,), (0,)), ((), ())), preferred_element_type=F32)


def _tile_group(i):
    return jnp.where(i < PROMPT_TILES, 0, 1 + (i - PROMPT_TILES) // TILES_PER_SAMPLE)


def _row_spec(width):
    return pl.BlockSpec((ROW_TILE, width), lambda i: (i, 0))


def _prompt_row_spec(width):
    return pl.BlockSpec((ROW_TILE, width), lambda i: (jnp.minimum(i, PROMPT_TILES - 1), 0))


def _sample_row_spec(width):
    return pl.BlockSpec((ROW_TILE, width), lambda i: (jnp.maximum(i - PROMPT_TILES, 0), 0))


def _mod_spec():
    return pl.BlockSpec((None, 9, D_MODEL), lambda i: (_tile_group(i), 0, 0))


def _const_spec(shape, lead=()):
    nd = len(shape)
    return pl.BlockSpec((None,) * len(lead) + tuple(shape), lambda *_: tuple(lead) + (0,) * nd,
                        pipeline_mode=pl.Buffered(1))


def _ada_block(cond_ref, w_ref, b_ref, o_ref):
    c = cond_ref[...]
    s = (c * _sigmoid(c)).astype(BF16)
    o_ref[...] = _dot(s, w_ref[...].astype(BF16)) + b_ref[...]


def _ada_specs(layer, tile, col_index):
    in_specs = [
        pl.BlockSpec((SUBLANES, D_MODEL), lambda *i: (0, 0)),
        pl.BlockSpec((None, D_MODEL, tile), lambda *i: (layer, 0, col_index(*i))),
        pl.BlockSpec((None, 1, tile), lambda *i: (layer, 0, col_index(*i))),
    ]
    return in_specs, pl.BlockSpec((SUBLANES, tile), lambda *i: (0, col_index(*i)))


def _mod_rows(out):
    return out[:N_COND].reshape(N_COND, 9, D_MODEL)


def _ada_mod(cond, w_ada, b_ada, layer):
    n_out = 9 * D_MODEL
    in_specs, out_spec = _ada_specs(layer, ADA_TILE, lambda j: j)
    out = pl.pallas_call(
        _ada_block,
        grid=(n_out // ADA_TILE,),
        in_specs=in_specs,
        out_specs=out_spec,
        out_shape=jax.ShapeDtypeStruct((SUBLANES, n_out), F32),
        compiler_params=pltpu.CompilerParams(
            dimension_semantics=("parallel",),
            vmem_limit_bytes=_vmem_limit(4 * D_MODEL * ADA_TILE * 4)),
        name="ada_mod",
    )(cond, w_ada, b_ada.reshape(DEPTH, 1, n_out))
    return _mod_rows(out)


def _ffn_kernel(*refs, slot, split_in, split_out, n_casts, with_ada):
    refs = list(refs)
    x_refs = [refs.pop(0) for _ in range(2 if split_in else 1)]
    mod_ref, gpre_ref, gpost_ref, win_ref, wout_ref = (refs.pop(0) for _ in range(5))
    cast_in = [refs.pop(0) for _ in range(n_casts)]
    ada_in = [refs.pop(0) for _ in range(3 if with_ada else 0)]
    o_refs = [refs.pop(0) for _ in range(2 if split_out else 1)]
    cast_out = [refs.pop(0) for _ in range(n_casts)]
    ada_out = refs.pop(0) if with_ada else None
    g_scr = refs.pop(0)
    is_prompt = pl.program_id(0) < PROMPT_TILES
    if split_in:
        x = jnp.where(is_prompt, x_refs[0][...], x_refs[1][...])
    else:
        x = x_refs[0][...]
    parts = []
    sub = ROW_TILE // FFN_SUBTILES
    for s in range(FFN_SUBTILES):
        rows = slice(s * sub, (s + 1) * sub)
        xs = x[rows]
        h = _mod_in(xs, gpre_ref[...], mod_ref, slot).astype(BF16)
        for c in range(D_FF // FF_CHUNK):
            lo = c * FF_CHUNK
            a = _dot(h, win_ref[:, lo:lo + FF_CHUNK])
            b = _dot(h, win_ref[:, D_FF + lo:D_FF + lo + FF_CHUNK])
            g_scr[rows, lo:lo + FF_CHUNK] = (a * _sigmoid(a) * b).astype(BF16)
            if s == 0 and c < len(cast_in):
                cast_out[c][...] = cast_in[c][...].astype(BF16)
            if s == 0 and c == len(cast_in) and with_ada:
                _ada_block(*ada_in, ada_out)
        y = _dot(g_scr[rows, :], wout_ref[...])
        parts.append(_mod_out(xs, y, gpost_ref[...], mod_ref, slot, 0.5))
    out = jnp.concatenate(parts, axis=0)
    if split_out:
        @pl.when(is_prompt)
        def _():
            o_refs[0][...] = out

        @pl.when(jnp.logical_not(is_prompt))
        def _():
            o_refs[1][...] = out
    else:
        o_refs[0][...] = out


def _ffn(x, mod, g_pre, g_post, w_in, w_out, slot, split_out=False, casts=(), ada=None):
    split_in = isinstance(x, tuple)
    xs = x if split_in else (x,)
    vmem = (2 * D_MODEL * D_FF + D_FF * D_MODEL) * 2 + 4 * ROW_TILE * D_MODEL * 4 \
        + ROW_TILE * D_FF * 2 + 8 * ROW_TILE * FF_CHUNK * 4 + 6 * ROW_TILE * D_MODEL * 4
    pair_specs = [_prompt_row_spec(D_MODEL), _sample_row_spec(D_MODEL)]
    scratch = [pltpu.VMEM((ROW_TILE, D_FF), BF16)]
    if split_out:
        out_specs = [_prompt_row_spec(D_MODEL), _sample_row_spec(D_MODEL)]
        out_shape = [jax.ShapeDtypeStruct((N_PROMPT, D_MODEL), F32),
                     jax.ShapeDtypeStruct((N_SAMPLE, D_MODEL), F32)]
    else:
        out_specs = [_row_spec(D_MODEL)]
        out_shape = [jax.ShapeDtypeStruct((N_TOK, D_MODEL), F32)]
    cast_in_specs = []
    for arr, lead in casts:
        rows, cols = arr.shape[-2:]
        blk = rows // CAST_STEPS
        assert blk * CAST_STEPS == rows and blk % 16 == 0 and CAST_STEPS <= N_TOK // ROW_TILE
        cast_in_specs.append(pl.BlockSpec(
            (None,) * len(lead) + (blk, cols),
            lambda i, lead=lead: tuple(lead) + (jnp.minimum(i, CAST_STEPS - 1), 0)))
        out_specs.append(pl.BlockSpec((blk, cols), lambda i: (jnp.minimum(i, CAST_STEPS - 1), 0)))
        out_shape.append(jax.ShapeDtypeStruct((rows, cols), BF16))
        vmem += 2 * blk * cols * (4 + 2)
    ada_specs, ada_args = [], []
    if ada is not None:
        cond, w_ada, b_ada, layer = ada
        n_out = 9 * D_MODEL
        last_col = n_out // ADA_SIDE_TILE - 1
        assert last_col < N_TOK // ROW_TILE
        ada_specs, ada_out_spec = _ada_specs(layer, ADA_SIDE_TILE, lambda i: jnp.minimum(i, last_col))
        ada_args = [cond, w_ada, b_ada.reshape(DEPTH, 1, n_out)]
        out_specs.append(ada_out_spec)
        out_shape.append(jax.ShapeDtypeStruct((SUBLANES, n_out), F32))
        vmem += 2 * D_MODEL * ADA_SIDE_TILE * 4 + D_MODEL * ADA_SIDE_TILE * 2
    res = pl.pallas_call(
        functools.partial(_ffn_kernel, slot=slot, split_in=split_in, split_out=split_out, n_casts=len(casts),
                          with_ada=ada is not None),
        grid=(N_TOK // ROW_TILE,),
        in_specs=(pair_specs if split_in else [_row_spec(D_MODEL)]) + [
            _mod_spec(),
            _const_spec((1, D_MODEL)),
            _const_spec((1, D_MODEL)),
            _const_spec((D_MODEL, 2 * D_FF)),
            _const_spec((D_FF, D_MODEL)),
        ] + cast_in_specs + ada_specs,
        out_specs=out_specs,
        out_shape=out_shape,
        scratch_shapes=scratch,
        compiler_params=pltpu.CompilerParams(
            dimension_semantics=("arbitrary",), vmem_limit_bytes=_vmem_limit(vmem)),
        name="ffn_sublayer",
    )(*xs, mod, g_pre.reshape(1, D_MODEL), g_post.reshape(1, D_MODEL), w_in, w_out, *[a for a, _ in casts],
      *ada_args)
    n_stream = 2 if split_out else 1
    stream = tuple(res[:n_stream]) if split_out else res[0]
    side = list(res[n_stream:])
    if ada is not None:
        side[-1] = _mod_rows(side[-1])
    return stream, side


def _qkv_kernel(x_ref, mod_ref, gpre_ref, w_ref, cos_ref, sin_ref, q_ref, kp_ref, vp_ref, k_ref, v_ref):
    i = pl.program_id(0)
    scale = HEAD_DIM ** -0.5
    h = _mod_in(x_ref[...], gpre_ref[...], mod_ref, 1).astype(BF16)
    qkv = _dot(h, w_ref[...])
    nq = N_Q_HEADS * HEAD_DIM
    nk = N_KV_HEADS * HEAD_DIM

    @pl.when(i < PROMPT_TILES)
    def _():
        q_ref[...] = (qkv[:, :nq] * scale).astype(BF16)
        kp_ref[...] = qkv[:, nq:nq + nk]
        vp_ref[...] = qkv[:, nq + nk:]

    @pl.when(i >= PROMPT_TILES)
    def _():
        v_ref[...] = qkv[:, nq + nk:]
        cos = cos_ref[...]
        sin = sin_ref[...]
        lane = lax.broadcasted_iota(jnp.int32, (ROW_TILE, LANES), 1)
        first = (lane & (HEAD_DIM // 4)) == 0

        def rope(xg):
            up = pltpu.roll(xg, LANES - HEAD_DIM // 4, 1)
            down = pltpu.roll(xg, HEAD_DIM // 4, 1)
            return xg * cos + jnp.where(first, up, down) * sin

        for j in range(nq // LANES):
            q_ref[:, j * LANES:(j + 1) * LANES] = (rope(qkv[:, j * LANES:(j + 1) * LANES]) * scale).astype(BF16)
        for j in range(nk // LANES):
            k_ref[:, j * LANES:(j + 1) * LANES] = rope(qkv[:, nq + j * LANES:nq + (j + 1) * LANES])


def _rope_tables():
    t = jnp.arange(DEC_SEQ)
    row = (t // GRID_W).astype(F32)
    col = (t % GRID_W).astype(F32)
    nf = HEAD_DIM // 4
    inv = ROPE_BASE ** (-jnp.arange(nf, dtype=F32) / nf)
    ar = row[:, None] * inv[None, :]
    ac = col[:, None] * inv[None, :]
    cos = jnp.concatenate([jnp.cos(ar), jnp.cos(ar), jnp.cos(ac), jnp.cos(ac)], axis=-1)
    sin = jnp.concatenate([-jnp.sin(ar), jnp.sin(ar), -jnp.sin(ac), jnp.sin(ac)], axis=-1)
    reps = LANES // HEAD_DIM
    return jnp.tile(cos, (1, reps)), jnp.tile(sin, (1, reps))


def _qkv(x, mod, g_pre, w_qkv, lead):
    cos, sin = _rope_tables()
    tab_spec = pl.BlockSpec(
        (ROW_TILE, LANES), lambda i: (jnp.maximum(i - PROMPT_TILES, 0) % TILES_PER_SAMPLE, 0))
    nk = N_KV_HEADS * HEAD_DIM
    vmem = D_MODEL * QKV_DIM * 2 + 6 * ROW_TILE * D_MODEL * 4 + 4 * ROW_TILE * QKV_DIM * 4
    return pl.pallas_call(
        _qkv_kernel,
        grid=(N_TOK // ROW_TILE,),
        in_specs=[
            _row_spec(D_MODEL),
            _mod_spec(),
            _const_spec((1, D_MODEL)),
            _const_spec((D_MODEL, QKV_DIM), lead),
            tab_spec,
            tab_spec,
        ],
        out_specs=[_row_spec(D_MODEL), _prompt_row_spec(nk), _prompt_row_spec(nk),
                   _sample_row_spec(nk), _sample_row_spec(nk)],
        out_shape=[
            jax.ShapeDtypeStruct((N_TOK, D_MODEL), BF16),
            jax.ShapeDtypeStruct((N_PROMPT, nk), F32),
            jax.ShapeDtypeStruct((N_PROMPT, nk), F32),
            jax.ShapeDtypeStruct((N_SAMPLE, nk), F32),
            jax.ShapeDtypeStruct((N_SAMPLE, nk), F32),
        ],
        compiler_params=pltpu.CompilerParams(
            dimension_semantics=("arbitrary",), vmem_limit_bytes=_vmem_limit(vmem)),
        name="attn_qkv",
    )(x, mod, g_pre.reshape(1, D_MODEL), w_qkv, cos, sin)


def _lane_halves(ref_or_val, hkv, rows):
    grp = ref_or_val[:, (hkv // 2) * LANES:(hkv // 2 + 1) * LANES]
    lane = lax.broadcasted_iota(jnp.int32, (rows, LANES), 1)
    in_low = lane < HEAD_DIM
    if hkv % 2 == 0:
        lo = jnp.where(in_low, grp, 0.0)
        hi = pltpu.roll(lo, HEAD_DIM, 1)
    else:
        hi = jnp.where(in_low, 0.0, grp)
        lo = pltpu.roll(hi, HEAD_DIM, 1)
    return lo.astype(BF16), hi.astype(BF16)


def _attend_many(sink_ref, problems, n_keys, valid):
    group = N_Q_HEADS // N_KV_HEADS
    n = len(problems)
    k_halves = [[_lane_halves(keys, hkv, n_keys) for hkv in range(N_KV_HEADS)] for _, keys, _ in problems]
    v_halves = [[_lane_halves(vals, hkv, n_keys) for hkv in range(N_KV_HEADS)] for _, _, vals in problems]
    units = [(p, h) for p in range(n) for h in range(N_Q_HEADS)]
    scores = {}
    for p, h in units:
        s = _dot_nt(problems[p][0][:, (h // 2) * LANES:(h // 2 + 1) * LANES], k_halves[p][h // group][h % 2])
        scores[p, h] = s if valid is None else jnp.where(valid, s, MASK_VALUE)
    maxes = {u: jnp.maximum(jnp.max(scores[u], axis=-1, keepdims=True), sink_ref[u[1]]) for u in units}
    exps = {u: jnp.exp(scores[u] - maxes[u]) for u in units}
    denoms = {u: jnp.sum(exps[u], axis=-1, keepdims=True) + jnp.exp(sink_ref[u[1]] - maxes[u]) for u in units}
    probs = {u: exps[u].astype(BF16) for u in units}
    inv = {u: 1.0 / denoms[u] for u in units}
    outs = []
    for p in range(n):
        out = []
        for j in range(N_Q_HEADS // 2):
            v_lo, v_hi = v_halves[p][(2 * j) // group]
            out.append((_dot(probs[p, 2 * j], v_lo) * inv[p, 2 * j]
                        + _dot(probs[p, 2 * j + 1], v_hi) * inv[p, 2 * j + 1]).astype(BF16))
        outs.append(jnp.concatenate(out, axis=1))
    return outs


def _attend(sink_ref, q_ref, keys, vals, n_keys, valid):
    return _attend_many(sink_ref, [(q_ref, keys, vals)], n_keys, valid)[0]


def _mixer_finish(o, x_ref, mod_ref, gpost_ref, w_ref, out_ref):
    out_ref[...] = _mod_out(x_ref[...], _dot(o, w_ref[...]), gpost_ref[...], mod_ref, 1, 1.0)


def _mixer_specs(rows, x_index, group_index):
    return [
        pl.BlockSpec((rows, D_MODEL), x_index),
        pl.BlockSpec((None, 9, D_MODEL), group_index),
        _const_spec((1, D_MODEL)),
        _const_spec((D_MODEL, D_MODEL)),
    ]


def _attn_prompt_kernel(sink_ref, q_ref, k_ref, v_ref, x_ref, mod_ref, gpost_ref, w_ref, out_ref):
    seq_rows = [slice(b * SEQ, (b + 1) * SEQ) for b in range(ATTN_PROMPT_SEQS)]
    outs = _attend_many(sink_ref, [(q_ref.at[r, :], k_ref[r, :], v_ref[r, :]) for r in seq_rows], SEQ, None)
    _mixer_finish(jnp.concatenate(outs, axis=0), x_ref, mod_ref, gpost_ref, w_ref, out_ref)


def _attn_prompt(sink, q, k, v, x, mod, g_post, w_out):
    nk = N_KV_HEADS * HEAD_DIM
    rows = ATTN_PROMPT_SEQS * SEQ
    return pl.pallas_call(
        _attn_prompt_kernel,
        grid=(BATCH // ATTN_PROMPT_SEQS,),
        in_specs=[
            pl.BlockSpec(memory_space=pltpu.SMEM),
            pl.BlockSpec((rows, D_MODEL), lambda b: (b, 0)),
            pl.BlockSpec((rows, nk), lambda b: (b, 0)),
            pl.BlockSpec((rows, nk), lambda b: (b, 0)),
        ] + _mixer_specs(rows, lambda b: (b, 0), lambda b: (0, 0, 0)),
        out_specs=pl.BlockSpec((rows, D_MODEL), lambda b: (b, 0)),
        out_shape=jax.ShapeDtypeStruct((N_PROMPT, D_MODEL), F32),
        compiler_params=pltpu.CompilerParams(
            dimension_semantics=("parallel",), vmem_limit_bytes=_vmem_limit(32 * 1024 * 1024)),
        name="attn_prompt",
    )(sink, q, k, v, x, mod, g_post.reshape(1, D_MODEL), w_out)


def _attn_sample_kernel(sink_ref, q_ref, k_ref, v_ref, ck_ref, cv_ref,
                        x_ref, mod_ref, gpost_ref, w_ref, out_ref):
    B = ATTN_BLOCK
    nblk = DEC_SEQ // B
    first = pl.program_id(1) * ATTN_Q_BLOCKS
    blocks = [jnp.clip(first + d, 0, nblk - 1) for d in range(-1, ATTN_Q_BLOCKS + 1)]
    starts = [pl.multiple_of(b * B, B) for b in blocks]
    keys = jnp.concatenate([k_ref[pl.ds(s, B), :] for s in starts] + [ck_ref[...]], axis=0)
    vals = jnp.concatenate([v_ref[pl.ds(s, B), :] for s in starts] + [cv_ref[...]], axis=0)
    n_lat = len(blocks) * B
    n_keys = n_lat + PAST_LEN
    rows = ATTN_Q_BLOCKS * B
    r = lax.broadcasted_iota(jnp.int32, (rows, n_keys), 0)
    c = lax.broadcasted_iota(jnp.int32, (rows, n_keys), 1)
    dist = B + r - c
    pos = (first - 1) * B + c
    lat_ok = (dist <= B) & (dist >= -B) & (pos >= 0) & (pos < DEC_SEQ)
    valid = (c >= n_lat) | lat_ok
    o = _attend(sink_ref, q_ref, keys, vals, n_keys, valid)
    _mixer_finish(o, x_ref, mod_ref, gpost_ref, w_ref, out_ref)


def _attn_sample(sink, q, k, v, cache_k, cache_v, x, mod, g_post, w_out):
    nk = N_KV_HEADS * HEAD_DIM
    rows = ATTN_Q_BLOCKS * ATTN_BLOCK
    n_tiles = DEC_SEQ // rows
    q_off = N_PROMPT // rows
    return pl.pallas_call(
        _attn_sample_kernel,
        grid=(DEC_BATCH, n_tiles),
        in_specs=[
            pl.BlockSpec(memory_space=pltpu.SMEM),
            pl.BlockSpec((rows, D_MODEL), lambda b, t: (q_off + b * n_tiles + t, 0)),
            pl.BlockSpec((DEC_SEQ, nk), lambda b, t: (b, 0)),
            pl.BlockSpec((DEC_SEQ, nk), lambda b, t: (b, 0)),
            pl.BlockSpec((None, PAST_LEN, nk), lambda b, t: (b, 0, 0)),
            pl.BlockSpec((None, PAST_LEN, nk), lambda b, t: (b, 0, 0)),
        ] + _mixer_specs(rows, lambda b, t: (q_off + b * n_tiles + t, 0), lambda b, t: (1 + b, 0, 0)),
        out_specs=pl.BlockSpec((rows, D_MODEL), lambda b, t: (b * n_tiles + t, 0)),
        out_shape=jax.ShapeDtypeStruct((N_SAMPLE, D_MODEL), F32),
        compiler_params=pltpu.CompilerParams(
            dimension_semantics=("parallel", "parallel"), vmem_limit_bytes=_vmem_limit(48 * 1024 * 1024)),
        name="attn_sample",
    )(sink, q, k, v, cache_k, cache_v, x, mod, g_post.reshape(1, D_MODEL), w_out)


def _mix_out_sample_kernel(o_ref, x_ref, mod_ref, gpost_ref, w_ref, out_ref):
    _mixer_finish(o_ref[...], x_ref, mod_ref, gpost_ref, w_ref, out_ref)


def _mix_out_sample(o_sample, x, mod, g_post, w):
    vmem = D_MODEL * D_MODEL * 2 + 12 * ROW_TILE * D_MODEL * 4
    return pl.pallas_call(
        _mix_out_sample_kernel,
        grid=(N_SAMPLE // ROW_TILE,),
        in_specs=[pl.BlockSpec((ROW_TILE, D_MODEL), lambda i: (i, 0))] + _mixer_specs(
            ROW_TILE, lambda i: (PROMPT_TILES + i, 0), lambda i: (1 + i // TILES_PER_SAMPLE, 0, 0)),
        out_specs=pl.BlockSpec((ROW_TILE, D_MODEL), lambda i: (i, 0)),
        out_shape=jax.ShapeDtypeStruct((N_SAMPLE, D_MODEL), F32),
        compiler_params=pltpu.CompilerParams(
            dimension_semantics=("parallel",), vmem_limit_bytes=_vmem_limit(vmem)),
        name="mixer_out_sample",
    )(o_sample, x, mod, g_post.reshape(1, D_MODEL), w)


def _chunk_masks():
    r = lax.broadcasted_iota(jnp.int32, (REC_GROUP, REC_GROUP), 0)
    c = lax.broadcasted_iota(jnp.int32, (REC_GROUP, REC_GROUP), 1)
    same = (r // CHUNK) == (c // CHUNK)
    return same & (c <= r), same & (c >= r)


def _rec_in_kernel(x_ref, mod_ref, gpre_ref, w_ref, lbf_ref, lbb_ref,
                   qdf_ref, kif_ref, qdb_ref, kib_ref, v_ref, sg_ref, decf_ref, decb_ref):
    G = REC_GROUP
    CB = REC_IN_COLS
    h = _mod_in(x_ref[...], gpre_ref[...], mod_ref, 1).astype(BF16)
    ri = lax.broadcasted_iota(jnp.int32, (CHUNK, CHUNK), 0)
    ci = lax.broadcasted_iota(jnp.int32, (CHUNK, CHUNK), 1)
    tris = tuple(jnp.concatenate([jnp.where(m, 1.0, 0.0).astype(BF16)] * 3, axis=1) for m in (ci <= ri, ci >= ri))
    outs = ((qdf_ref, kif_ref, decf_ref, lbf_ref), (qdb_ref, kib_ref, decb_ref, lbb_ref))

    def project(cb, part):
        return _dot(h, w_ref[:, part * D_MODEL + cb * CB:part * D_MODEL + (cb + 1) * CB])

    def finish_qvg(cb, yq, yv, yg):
        cols = slice(cb * CB, (cb + 1) * CB)
        v_ref[:, cols] = yv.astype(BF16)
        sg_ref[:, cols] = yg * _sigmoid(yg)
        return yq * _sigmoid(yq) * (REC_DK ** -0.5)

    def finish_dir(cb, d, z, qf):
        cols = slice(cb * CB, (cb + 1) * CB)
        qd_ref, ki_ref, dec_ref, lb_ref = outs[d]
        lb = lb_ref[:, cols]
        one_m_lb = 1.0 - lb
        e = jnp.exp(-jnp.abs(z))
        rcp = 1.0 / (1.0 + e)
        small = e * rcp
        pos = z >= 0
        key = one_m_lb * jnp.where(pos, small, rcp)
        logf = jnp.log(lb + one_m_lb * jnp.where(pos, rcp, small))
        hi = logf.astype(BF16)
        rest = logf - hi.astype(F32)
        mid = rest.astype(BF16)
        lo = (rest - mid.astype(F32)).astype(BF16)
        edge = CHUNK - 1 if d == 0 else 0
        for c in range(ROW_TILE // CHUNK):
            rows = slice(c * CHUNK, (c + 1) * CHUNK)
            bc = _dot(tris[d], jnp.concatenate([hi[rows], mid[rows], lo[rows]], axis=0))
            decay = jnp.exp(bc)
            qd_ref[rows, cols] = (qf[rows] * decay).astype(BF16)
            ki_ref[rows, cols] = (key[rows] * (1.0 / decay)).astype(BF16)
            dec_ref[c // (G // CHUNK), c % (G // CHUNK):c % (G // CHUNK) + 1, cols] = \
                decay[edge:edge + 1, :]

    n_blocks = D_MODEL // CB
    cur = [project(0, part) for part in range(5)]
    for cb in range(n_blocks):
        more = cb + 1 < n_blocks
        nxt = [None] * 5
        if more:
            nxt[0], nxt[1] = project(cb + 1, 0), project(cb + 1, 1)
        qf = finish_qvg(cb, cur[0], cur[1], cur[4])
        if more:
            nxt[2], nxt[3] = project(cb + 1, 2), project(cb + 1, 3)
        finish_dir(cb, 0, cur[2], qf)
        if more:
            nxt[4] = project(cb + 1, 4)
        finish_dir(cb, 1, cur[3], qf)
        cur = nxt


def _rec_in(x, mod, g_pre, w, lead, lb_f, lb_b):
    n_chunks = REC_GROUP // CHUNK
    groups_per_tile = ROW_TILE // REC_GROUP
    vmem = D_MODEL * REC_IN_DIM * 2 + 4 * ROW_TILE * D_MODEL * 4 + 2 * ROW_TILE * D_MODEL * (7 * 2 + 4) \
        + 12 * ROW_TILE * D_MODEL * 4
    act = jax.ShapeDtypeStruct((N_TOK, D_MODEL), BF16)
    dec = jax.ShapeDtypeStruct((N_TOK // REC_GROUP, n_chunks, D_MODEL), F32)
    dec_spec = pl.BlockSpec((groups_per_tile, n_chunks, D_MODEL), lambda i: (i, 0, 0))
    return pl.pallas_call(
        _rec_in_kernel,
        grid=(N_TOK // ROW_TILE,),
        in_specs=[
            _row_spec(D_MODEL),
            _mod_spec(),
            _const_spec((1, D_MODEL)),
            _const_spec((D_MODEL, REC_IN_DIM), lead),
            _const_spec((1, D_MODEL)),
            _const_spec((1, D_MODEL)),
        ],
        out_specs=[_row_spec(D_MODEL) for _ in range(6)] + [dec_spec, dec_spec],
        out_shape=[act] * 5 + [jax.ShapeDtypeStruct((N_TOK, D_MODEL), F32), dec, dec],
        compiler_params=pltpu.CompilerParams(
            dimension_semantics=("parallel",), vmem_limit_bytes=_vmem_limit(vmem)),
        name="rec_in",
    )(x, mod, g_pre.reshape(1, D_MODEL), w, lb_f.reshape(1, D_MODEL), lb_b.reshape(1, D_MODEL))


def _rec_kernel(*refs, seq_len, n_heads, has_init):
    qdf_ref, kif_ref, qdb_ref, kib_ref, v_ref, sg_ref, decf_ref, decb_ref, gn_ref = refs[:9]
    if has_init:
        s0_ref, o_ref = refs[9:]
        st_ref = mixer_refs = None
    else:
        *mixer_refs, out_ref, st_ref, o_ref = refs[9:]
        s0_ref = None
    G = REC_GROUP
    per_group = G // CHUNK
    n_chunks = seq_len // CHUNK
    mask_f, mask_b = _chunk_masks()

    def rows_of(c):
        return slice(c * CHUNK, (c + 1) * CHUNK)

    heads = range(n_heads)
    cols = [slice(hh * REC_DK, (hh + 1) * REC_DK) for hh in heads]
    groups = range(seq_len // G)

    def group_rows(g):
        return slice(g * G, (g + 1) * G)

    upd = [[_dot_tn(v_ref[rows_of(c), cs],
                    jnp.concatenate([kif_ref[rows_of(c), cs], kib_ref[rows_of(c), cs]], axis=1))
            for c in range(n_chunks)] for cs in cols]
    a = [[(jnp.where(mask_f, _dot_nt(qdf_ref[group_rows(g), cs], kif_ref[group_rows(g), cs]), 0.0)
           + jnp.where(mask_b, _dot_nt(qdb_ref[group_rows(g), cs], kib_ref[group_rows(g), cs]), 0.0)
           ).astype(BF16) for g in groups] for cs in cols]
    enter = []
    for hh in heads:
        cs = cols[hh]
        if has_init:
            s_f, s_b = s0_ref[0, hh].T, s0_ref[1, hh].T
        else:
            s_f = s_b = jnp.zeros((REC_DV, REC_DK), F32)
        ent = [None] * n_chunks
        for c in range(n_chunks):
            ent[c] = s_f.astype(BF16)
            dec = decf_ref[c // per_group, c % per_group:c % per_group + 1, cs]
            s_f = s_f * dec + upd[hh][c][:, :REC_DK] * dec
        for c in range(n_chunks - 1, -1, -1):
            ent[c] = jnp.concatenate([ent[c], s_b.astype(BF16)], axis=1)
            dec = decb_ref[c // per_group, c % per_group:c % per_group + 1, cs]
            s_b = s_b * dec + upd[hh][c][:, REC_DK:] * dec
        enter.append(ent)
        if st_ref is not None:
            st_ref[0, hh] = s_f.T
            st_ref[1, hh] = s_b.T
    o_intra = [[_dot(a[hh][g], v_ref[group_rows(g), cols[hh]]) for g in groups] for hh in heads]
    for hh in heads:
        cs = cols[hh]
        for c in range(n_chunks):
            q_cat = jnp.concatenate([qdf_ref[rows_of(c), cs], qdb_ref[rows_of(c), cs]], axis=1)
            o_c = o_intra[hh][c // per_group][rows_of(c % per_group)] + _dot_nt(q_cat, enter[hh][c])
            o_ref[rows_of(c), cs] = (_rms(o_c, gn_ref[hh]) * sg_ref[rows_of(c), cs]).astype(BF16)
    if mixer_refs is not None:
        _mixer_finish(o_ref[...], *mixer_refs, out_ref)


def _rec(rec_acts, g_norm, s0, mixer, *, seq_len, n_seq, row_block_off):
    hb = min(REC_HEADS, REC_STEP_ROWS_X_HEADS // seq_len)
    w = hb * REC_DK
    n_hp = REC_HEADS // hb
    n_groups = seq_len // REC_GROUP
    has_init = s0 is not None

    def act_spec():
        return pl.BlockSpec((seq_len, w), lambda b, h: (row_block_off + b, h))

    def dec_spec():
        return pl.BlockSpec((n_groups, REC_GROUP // CHUNK, w), lambda b, h: (row_block_off + b, 0, h))

    in_specs = [act_spec() for _ in range(6)] + [dec_spec(), dec_spec(),
                                                 pl.BlockSpec((hb, 1, REC_DV), lambda b, h: (h, 0, 0))]
    args = list(rec_acts) + [g_norm.reshape(REC_HEADS, 1, REC_DV)]
    state_spec = pl.BlockSpec((None, 2, hb, REC_DK, REC_DV), lambda b, h: (b, 0, h, 0, 0))
    scratch = []
    if has_init:
        in_specs.append(state_spec)
        args.append(s0)
        out_specs = pl.BlockSpec((seq_len, w), lambda b, h: (b, h))
        out_shape = jax.ShapeDtypeStruct((n_seq * seq_len, D_MODEL), BF16)
    else:
        assert n_hp == 1
        x, mod, g_post, w_out = mixer
        in_specs += _mixer_specs(seq_len, lambda b, h: (row_block_off + b, 0), lambda b, h: (0, 0, 0))
        args += [x, mod, g_post.reshape(1, D_MODEL), w_out]
        out_specs = [pl.BlockSpec((seq_len, D_MODEL), lambda b, h: (b, 0)), state_spec]
        out_shape = [jax.ShapeDtypeStruct((n_seq * seq_len, D_MODEL), F32),
                     jax.ShapeDtypeStruct((n_seq, 2, REC_HEADS, REC_DK, REC_DV), F32)]
        scratch = [pltpu.VMEM((seq_len, D_MODEL), BF16)]
    return pl.pallas_call(
        functools.partial(_rec_kernel, seq_len=seq_len, n_heads=hb, has_init=has_init),
        grid=(n_seq, n_hp),
        in_specs=in_specs,
        out_specs=out_specs,
        out_shape=out_shape,
        scratch_shapes=scratch,
        compiler_params=pltpu.CompilerParams(
            dimension_semantics=("parallel", "parallel"),
            vmem_limit_bytes=_vmem_limit(40 * seq_len * w * 4)),
        name="rec_scan_init" if has_init else "rec_scan",
    )(*args)


def kernel(x_prompt, x_sample, c, cache_k, cache_v, state_s, c_ctx, w_ada, b_ada, norm_pre, norm_post,
           w_ffn_in, w_ffn_out, w_qkv, w_attn_out, attn_sink, w_rec_in, rec_lb_logits, rec_norm, w_rec_out):
    x = (x_prompt.reshape(N_PROMPT, D_MODEL), x_sample.reshape(N_SAMPLE, D_MODEL))
    cond = jnp.concatenate([c_ctx[None], c, jnp.zeros((SUBLANES - N_COND, D_MODEL), F32)], axis=0)
    mod = _ada_mod(cond, w_ada, b_ada, 0)

    lb_soft = jax.nn.softmax(rec_lb_logits.astype(F32), axis=1)
    lb_all = jnp.cumsum(lb_soft, axis=1) - lb_soft[:, :1]

    nk = N_KV_HEADS * HEAD_DIM
    ffn_w = (w_ffn_in[0, 0].astype(BF16), w_ffn_out[0, 0].astype(BF16))
    new_k = new_v = new_s = None
    for i in range(DEPTH):
        j = i // 2
        mixer_params = ((w_qkv, (j,)), (w_attn_out, (j,))) if i % 2 == 0 else ((w_rec_in, (j,)), (w_rec_out, (j,)))
        x, cast = _ffn(x, mod, norm_pre[i, 0], norm_post[i, 0], *ffn_w, 0,
                       casts=mixer_params + ((w_ffn_in, (i, 1)), (w_ffn_out, (i, 1))))
        w_mix_in, w_mix_out, ffn_w = cast[0], cast[1], tuple(cast[2:])
        if i % 2 == 0:
            q, k_p, v_p, k_s, v_s = _qkv(x, mod, norm_pre[i, 1], w_mix_in, ())
            x_p = _attn_prompt(attn_sink[j], q, k_p, v_p, x, mod, norm_post[i, 1], w_mix_out)
            x_s = _attn_sample(attn_sink[j], q, k_s, v_s,
                               cache_k[:, j].reshape(DEC_BATCH, PAST_LEN, nk),
                               cache_v[:, j].reshape(DEC_BATCH, PAST_LEN, nk),
                               x, mod, norm_post[i, 1], w_mix_out)
            new_k = k_p.reshape(BATCH, 1, SEQ, N_KV_HEADS, HEAD_DIM)
            new_v = v_p.reshape(BATCH, 1, SEQ, N_KV_HEADS, HEAD_DIM)
        else:
            acts = _rec_in(x, mod, norm_pre[i, 1], w_mix_in, (), lb_all[0, i], lb_all[1, i])
            x_p, s_p = _rec(acts, rec_norm[j], None, (x, mod, norm_post[i, 1], w_mix_out),
                            seq_len=SEQ, n_seq=BATCH, row_block_off=0)
            o_s = _rec(acts, rec_norm[j], state_s[:, j], None,
                       seq_len=DEC_SEQ, n_seq=DEC_BATCH, row_block_off=N_PROMPT // DEC_SEQ)
            x_s = _mix_out_sample(o_s, x, mod, norm_post[i, 1], w_mix_out)
            new_s = s_p.reshape(BATCH, 1, 2, REC_HEADS, REC_DK, REC_DV)
        last = i == DEPTH - 1
        x, cast = _ffn((x_p, x_s), mod, norm_pre[i, 2], norm_post[i, 2], *ffn_w, 2, split_out=last,
                       casts=() if last else ((w_ffn_in, (i + 1, 0)), (w_ffn_out, (i + 1, 0))),
                       ada=None if last else (cond, w_ada, b_ada, i + 1))
        if not last:
            ffn_w, mod = tuple(cast[:2]), cast[2]
    y_prompt, y_sample = x
    return (y_prompt.reshape(BATCH, SEQ, D_MODEL), y_sample.reshape(DEC_BATCH, DEC_SEQ, D_MODEL),
            new_k, new_v, new_s)
```
